```python
import math
import jax
import jax.numpy as jnp
from jax import lax
import numpy as np

D_MODEL = 1024
BATCH = 8
SEQ = 4096
DEPTH = 1

N_META = 16
D_RNN = 1024
N_RNN_BLOCKS = 8
RNN_BLOCK = D_RNN // N_RNN_BLOCKS
CONV_WIDTH = 4
LRU_C = 8.0
N_HEADS = 8
HEAD_DIM = 64
V_DIM = 2 * HEAD_DIM
D_ATTN = N_HEADS * 2 * HEAD_DIM
Q_BLOCK = 128
N_EXPERTS = 256
TOP_K = 8
N_GROUPS = 8
TOPK_GROUPS = 4
D_EXPERT = 256
D_SHARED = 256
ROUTED_SCALE = 2.5
DISPATCH_BLOCK = 128
LN_EPS = 1e-5
IN_COLS = 2 * D_RNN + 3 * D_ATTN + 2 * D_MODEL

kernel_name = 'griffin_diffattn_moe_hybrid'


def _layer_norm(x, g, b):
    xf = x.astype(jnp.float32)
    mu = jnp.mean(xf, axis=-1, keepdims=True)
    var = jnp.mean(jnp.square(xf - mu), axis=-1, keepdims=True)
    y = (xf - mu) * lax.rsqrt(var + LN_EPS)
    return (y * g.astype(jnp.float32) + b.astype(jnp.float32)).astype(x.dtype)


def _rms_norm(x, g):
    xf = x.astype(jnp.float32)
    y = xf * lax.rsqrt(jnp.mean(jnp.square(xf), axis=-1, keepdims=True) + LN_EPS)
    return (y * g.astype(jnp.float32)).astype(x.dtype)


def _causal_depthwise_conv(x, w, b):
    out = lax.conv_general_dilated(
        x, w[:, None, :], window_strides=(1,), padding=[(CONV_WIDTH - 1, 0)],
        dimension_numbers=('NWC', 'WIO', 'NWC'), feature_group_count=x.shape[-1])
    return out + b


def _block_diag(x, w, b):
    xb = x.reshape(x.shape[:-1] + (N_RNN_BLOCKS, RNN_BLOCK))
    y = jnp.einsum('btnr,nrs->btns', xb, w)
    return y.reshape(x.shape) + b


def _lin_rec_combine(left, right):
    a1, b1 = left
    a2, b2 = right
    return a1 * a2, a2 * b1 + b2


def _rg_lru(x, w_a, b_a, w_i, b_i, lam):
    f32 = jnp.float32
    r = jax.nn.sigmoid(_block_diag(x, w_a, b_a).astype(f32))
    i = jax.nn.sigmoid(_block_diag(x, w_i, b_i).astype(f32))
    log_a = -LRU_C * r * jax.nn.softplus(-lam.astype(f32))
    a = jnp.exp(log_a)
    mult = jnp.sqrt(-jnp.expm1(2.0 * log_a))
    first = (jnp.arange(x.shape[1]) == 0)[None, :, None]
    mult = jnp.where(first, 1.0, mult)
    bterm = mult * (i * x.astype(f32))
    _, h = lax.associative_scan(_lin_rec_combine, (a, bterm), axis=1)
    return h.astype(x.dtype)


def _diff_attention(q, k, v, lam, lambda_init, subln_g):
    B, T = q.shape[0], q.shape[1]
    n_qb = T // Q_BLOCK
    scale = HEAD_DIM ** -0.5
    kpos = jnp.arange(T)

    def one_block(blk):
        q0 = blk * Q_BLOCK
        qb = lax.dynamic_slice_in_dim(q, q0, Q_BLOCK, axis=1)
        s = jnp.einsum('bqhmd,bkhmd->bhmqk', qb, k).astype(jnp.float32) * scale
        qpos = q0 + jnp.arange(Q_BLOCK)
        mask = kpos[None, :] <= qpos[:, None]
        p = jax.nn.softmax(jnp.where(mask, s, -jnp.inf), axis=-1)
        wgt = p[:, :, 0] - lam * p[:, :, 1]
        return jnp.einsum('bhqk,bkhe->bqhe', wgt.astype(v.dtype), v)

    o = lax.map(one_block, jnp.arange(n_qb))
    o = jnp.moveaxis(o, 0, 1).reshape(B, T, N_HEADS, V_DIM)
    o = _rms_norm(o, subln_g) * (1.0 - lambda_init)
    return o.reshape(B, T, D_ATTN)


def _swiglu(x, wg, wu, wd):
    return (jax.nn.silu(x @ wg) * (x @ wu)) @ wd


def _route(x, router_w, router_bias):
    n = x.shape[0]
    scores = jax.nn.sigmoid((x @ router_w).astype(jnp.float32))
    sel = scores + router_bias.astype(jnp.float32)
    grp = sel.reshape(n, N_GROUPS, N_EXPERTS // N_GROUPS)
    grp_score = jnp.sum(lax.top_k(grp, 2)[0], axis=-1)
    _, gidx = lax.top_k(grp_score, TOPK_GROUPS)
    gmask = jnp.any(gidx[..., None] == jnp.arange(N_GROUPS), axis=1)
    sel = jnp.where(gmask[:, :, None], grp, -jnp.inf).reshape(n, N_EXPERTS)
    _, eidx = lax.top_k(sel, TOP_K)
    w = jnp.take_along_axis(scores, eidx, axis=-1)
    w = w / jnp.sum(w, axis=-1, keepdims=True) * ROUTED_SCALE
    return eidx, w


def _routed_experts(x, eidx, w, w_gate, w_up, w_down):
    n, d = x.shape
    n_assign = n * TOP_K
    e_flat = eidx.reshape(n_assign)
    tok_flat = jnp.repeat(jnp.arange(n, dtype=jnp.int32), TOP_K)
    w_flat = w.reshape(n_assign)
    order = jnp.argsort(e_flat)
    e_sorted = e_flat[order]
    tok_sorted = tok_flat[order]
    w_sorted = w_flat[order]
    counts = jnp.bincount(e_flat, length=N_EXPERTS)
    padded = (counts + DISPATCH_BLOCK - 1) // DISPATCH_BLOCK * DISPATCH_BLOCK
    start = jnp.cumsum(counts) - counts
    pend = jnp.cumsum(padded)
    pstart = pend - padded
    dest = pstart[e_sorted] + (jnp.arange(n_assign) - start[e_sorted])
    n_blocks = -(-n_assign // DISPATCH_BLOCK) + N_EXPERTS
    n_slots = n_blocks * DISPATCH_BLOCK
    slot_tok = jnp.full((n_slots,), n, jnp.int32).at[dest].set(tok_sorted)
    slot_w = jnp.zeros((n_slots,), jnp.float32).at[dest].set(w_sorted)
    block_start = jnp.arange(n_blocks) * DISPATCH_BLOCK
    block_exp = jnp.minimum(jnp.searchsorted(pend, block_start, side='right'), N_EXPERTS - 1)
    x_pad = jnp.concatenate([x, jnp.zeros((1, d), x.dtype)], axis=0)

    def body(acc, blk):
        e, toks, wt = blk
        xb = x_pad[toks]
        yb = _swiglu(xb, w_gate[e], w_up[e], w_down[e]) * wt[:, None].astype(x.dtype)
        return acc.at[toks].add(yb.astype(acc.dtype)), None

    acc, _ = lax.scan(body, jnp.zeros((n + 1, d), x.dtype),
                      (block_exp, slot_tok.reshape(n_blocks, DISPATCH_BLOCK),
                       slot_w.reshape(n_blocks, DISPATCH_BLOCK)))
    return acc[:n]


def setup_inputs(seed: int = 0) -> dict:
    key = jax.random.key(seed)
    keys = jax.random.split(key, 40)
    ctr = [0]

    def nk():
        ctr[0] += 1
        return keys[ctr[0] - 1]

    f32 = jnp.float32
    L = DEPTH
    beta = (8.0 * DEPTH) ** -0.25

    def nrm(shape, fan_in, scale=1.0):
        return jax.random.normal(nk(), shape, f32) * (scale * fan_in ** -0.5)

    def gain(shape):
        return 1.0 + 0.02 * jax.random.normal(nk(), shape, f32)

    def small(shape, s=0.02):
        return s * jax.random.normal(nk(), shape, f32)

    x = jax.random.normal(nk(), (BATCH, SEQ, D_MODEL), f32)
    meta_tokens = jax.random.normal(nk(), (N_META, D_MODEL), f32)
    emb_ln_g = gain((D_MODEL,))
    emb_ln_b = small((D_MODEL,))
    v_lo = 2 * D_RNN + 2 * D_ATTN
    col_scale = jnp.ones((IN_COLS,), f32).at[v_lo:v_lo + D_ATTN].set(beta)
    w_in = nrm((L, D_MODEL, IN_COLS), D_MODEL) * col_scale
    b_gate = small((L, 2 * D_MODEL), 0.1)
    conv_w = nrm((L, CONV_WIDTH, D_RNN), CONV_WIDTH)
    conv_b = small((L, D_RNN))
    lru_w_a = nrm((L, N_RNN_BLOCKS, RNN_BLOCK, RNN_BLOCK), RNN_BLOCK)
    lru_b_a = small((L, D_RNN))
    lru_w_i = nrm((L, N_RNN_BLOCKS, RNN_BLOCK, RNN_BLOCK), RNN_BLOCK)
    lru_b_i = small((L, D_RNN))
    a8 = jax.random.uniform(nk(), (L, D_RNN), f32, 0.9, 0.999)
    s = a8 ** (1.0 / LRU_C)
    lru_lambda = jnp.log(s) - jnp.log1p(-s)
    lam_q1 = small((L, HEAD_DIM), 0.1)
    lam_k1 = small((L, HEAD_DIM), 0.1)
    lam_q2 = small((L, HEAD_DIM), 0.1)
    lam_k2 = small((L, HEAD_DIM), 0.1)
    subln_g = gain((L, V_DIM))
    w_rec_proj = nrm((L, D_RNN, D_MODEL), D_RNN, beta)
    w_attn_proj = nrm((L, D_ATTN, D_MODEL), D_ATTN, beta)
    w_o = nrm((L, D_MODEL, D_MODEL), D_MODEL, beta)
    ln1_g = gain((L, D_MODEL))
    ln1_b = small((L, D_MODEL))
    router_w = nrm((L, D_MODEL, N_EXPERTS), D_MODEL)
    router_bias = small((L, N_EXPERTS), 0.01)
    w_e_gate = nrm((L, N_EXPERTS, D_MODEL, D_EXPERT), D_MODEL)
    w_e_up = nrm((L, N_EXPERTS, D_MODEL, D_EXPERT), D_MODEL)
    w_e_down = nrm((L, N_EXPERTS, D_EXPERT, D_MODEL), D_EXPERT, beta)
    w_s_gate = nrm((L, D_MODEL, D_SHARED), D_MODEL)
    w_s_up = nrm((L, D_MODEL, D_SHARED), D_MODEL)
    w_s_down = nrm((L, D_SHARED, D_MODEL), D_SHARED, beta)
    ln2_g = gain((L, D_MODEL))
    ln2_b = small((L, D_MODEL))
    return {'x': x, 'meta_tokens': meta_tokens, 'emb_ln_g': emb_ln_g, 'emb_ln_b': emb_ln_b,
            'w_in': w_in, 'b_gate': b_gate, 'conv_w': conv_w, 'conv_b': conv_b,
            'lru_w_a': lru_w_a, 'lru_b_a': lru_b_a, 'lru_w_i': lru_w_i, 'lru_b_i': lru_b_i,
            'lru_lambda': lru_lambda, 'lam_q1': lam_q1, 'lam_k1': lam_k1, 'lam_q2': lam_q2,
            'lam_k2': lam_k2, 'subln_g': subln_g, 'w_rec_proj': w_rec_proj,
            'w_attn_proj': w_attn_proj, 'w_o': w_o, 'ln1_g': ln1_g, 'ln1_b': ln1_b,
            'router_w': router_w, 'router_bias': router_bias, 'w_e_gate': w_e_gate,
            'w_e_up': w_e_up, 'w_e_down': w_e_down, 'w_s_gate': w_s_gate, 'w_s_up': w_s_up,
            'w_s_down': w_s_down, 'ln2_g': ln2_g, 'ln2_b': ln2_b}


def reference(x, meta_tokens, emb_ln_g, emb_ln_b, w_in, b_gate, conv_w, conv_b,
              lru_w_a, lru_b_a, lru_w_i, lru_b_i, lru_lambda, lam_q1, lam_k1, lam_q2,
              lam_k2, subln_g, w_rec_proj, w_attn_proj, w_o, ln1_g, ln1_b, router_w,
              router_bias, w_e_gate, w_e_up, w_e_down, w_s_gate, w_s_up, w_s_down,
              ln2_g, ln2_b):
    f32 = jnp.float32
    B, S, D = x.shape
    T = N_META + S
    T_pad = -(-T // Q_BLOCK) * Q_BLOCK
    meta = jnp.broadcast_to(meta_tokens.astype(x.dtype)[None], (B, N_META, D))
    h = jnp.concatenate([meta, x], axis=1)
    h = jnp.pad(h, ((0, 0), (0, T_pad - T), (0, 0)))
    h = _layer_norm(h, emb_ln_g, emb_ln_b)
    alpha = (2.0 * DEPTH) ** 0.25
    splits = [D_RNN, 2 * D_RNN, 2 * D_RNN + D_ATTN, 2 * D_RNN + 2 * D_ATTN, 2 * D_RNN + 3 * D_ATTN]
    for l in range(DEPTH):
        lambda_init = 0.8 - 0.6 * math.exp(-0.3 * l)
        proj = h @ w_in[l]
        xr, yr, q, k, v, g = jnp.split(proj, splits, axis=-1)
        xr = _causal_depthwise_conv(xr, conv_w[l], conv_b[l])
        rec = _rg_lru(xr, lru_w_a[l], lru_b_a[l], lru_w_i[l], lru_b_i[l], lru_lambda[l])
        rec = rec * jax.nn.gelu(yr)
        lam = (jnp.exp(jnp.sum(lam_q1[l].astype(f32) * lam_k1[l].astype(f32)))
               - jnp.exp(jnp.sum(lam_q2[l].astype(f32) * lam_k2[l].astype(f32))) + lambda_init)
        att = _diff_attention(q.reshape(B, T_pad, N_HEADS, 2, HEAD_DIM),
                              k.reshape(B, T_pad, N_HEADS, 2, HEAD_DIM),
                              v.reshape(B, T_pad, N_HEADS, V_DIM), lam, lambda_init, subln_g[l])
        gates = jax.nn.sigmoid(g + b_gate[l])
        g_rec, g_att = jnp.split(gates, 2, axis=-1)
        mix = (g_rec * (rec @ w_rec_proj[l]) + g_att * (att @ w_attn_proj[l])) @ w_o[l]
        h = _layer_norm(alpha * h + mix, ln1_g[l], ln1_b[l])
        hf = h.reshape(B * T_pad, D)
        eidx, wts = _route(hf, router_w[l], router_bias[l])
        ffn = (_routed_experts(hf, eidx, wts, w_e_gate[l], w_e_up[l], w_e_down[l])
               + _swiglu(hf, w_s_gate[l], w_s_up[l], w_s_down[l]))
        h = _layer_norm(alpha * h + ffn.reshape(B, T_pad, D), ln2_g[l], ln2_b[l])
    return h[:, N_META:N_META + S]
```

```python
import functools
import math

import jax
import jax.numpy as jnp
from jax import lax
from jax.experimental import pallas as pl
from jax.experimental.pallas import tpu as pltpu

F32 = jnp.float32
BF16 = jnp.bfloat16

N_META = 16
D_MODEL = 1024
D_RNN = 1024
N_RNN_BLOCKS = 8
RNN_BLOCK = D_RNN // N_RNN_BLOCKS
CONV_WIDTH = 4
LRU_C = 8.0
N_HEADS = 8
HEAD_DIM = 64
V_DIM = 2 * HEAD_DIM
D_ATTN = N_HEADS * V_DIM
N_EXPERTS = 256
TOP_K = 8
N_GROUPS = 8
GROUP_SIZE = N_EXPERTS // N_GROUPS
TOPK_GROUPS = 4
D_EXPERT = 256
D_SHARED = 256
ROUTED_SCALE = 2.5
LN_EPS = 1e-5
DEPTH = 1
IN_COLS = 2 * D_RNN + 3 * D_ATTN + 2 * D_MODEL
ALPHA = (2.0 * DEPTH) ** 0.25
LAMBDA_INIT = 0.8 - 0.6 * math.exp(-0.3 * 0)

LANES = 128
SUBLANES = 8
VMEM_LIMIT = 56 * 1024 * 1024
NEG_BIG = -1e30

COL_XR, COL_YR, COL_Q, COL_K, COL_V, COL_GREC, COL_GATT = range(7)


def _cparams(sem):
    return pltpu.CompilerParams(dimension_semantics=sem, vmem_limit_bytes=VMEM_LIMIT)


def _layer_norm_rows(x, g, b):
    mu = jnp.mean(x, axis=-1, keepdims=True)
    xc = x - mu
    var = jnp.mean(xc * xc, axis=-1, keepdims=True)
    return xc * lax.rsqrt(var + LN_EPS) * g + b


def _ln_inproj_kernel(x_ref, g_ref, b_ref, w_ref, o_ref, xn_ref):
    @pl.when(pl.program_id(1) == 0)
    def _():
        xn_ref[...] = _layer_norm_rows(x_ref[...], g_ref[...], b_ref[...]).astype(BF16)

    o_ref[...] = jnp.dot(xn_ref[...], w_ref[...], preferred_element_type=F32).astype(o_ref.dtype)


def _ln_inproj(x2d, g, b, w_bf16):
    n, d = x2d.shape
    cols = w_bf16.shape[1]
    tm = min(n, 2048)
    tn = 512
    return pl.pallas_call(
        _ln_inproj_kernel,
        grid=(n // tm, cols // tn),
        in_specs=[
            pl.BlockSpec((tm, d), lambda i, j: (i, 0)),
            pl.BlockSpec((1, d), lambda i, j: (0, 0)),
            pl.BlockSpec((1, d), lambda i, j: (0, 0)),
            pl.BlockSpec((d, tn), lambda i, j: (0, j)),
        ],
        out_specs=pl.BlockSpec((tm, tn), lambda i, j: (i, j)),
        out_shape=jax.ShapeDtypeStruct((n, cols), BF16),
        scratch_shapes=[pltpu.VMEM((tm, d), BF16)],
        compiler_params=_cparams(("parallel", "arbitrary")),
        name="ln_inproj",
    )(x2d, g, b, w_bf16)


def _rglru_kernel(x_ref, y_ref, ctx0_ref, h0_ref, cw_ref, cb_ref, wai_ref, ba_ref, bi_ref,
                  lam_ref, o_ref, hl_ref, ctx_scr, a_scr, b_scr, hs_scr, h_scr, *, nb, tb, first):
    step = pl.program_id(0)
    nct = D_RNN // LANES

    @pl.when(step == 0)
    def _():
        for bb in range(nb):
            ctx_scr[bb] = ctx0_ref[...]
        h_scr[...] = h0_ref[...]

    lam = lam_ref[...]
    sp = jnp.maximum(-lam, 0.0) + jnp.log1p(jnp.exp(-jnp.abs(lam)))
    cw = cw_ref[...]
    cb = cb_ref[...]
    ba = ba_ref[...]
    bi = bi_ref[...]
    row = lax.broadcasted_iota(jnp.int32, (tb, 1), 0)

    def slab(bb, carry):
        xb = x_ref[bb].astype(F32)
        xcat = jnp.concatenate([ctx_scr[bb], xb], axis=0)
        u = cb + cw[CONV_WIDTH - 1:CONV_WIDTH, :] * xb
        for s in range(1, CONV_WIDTH):
            shifted = pltpu.roll(xcat, s, 0)[SUBLANES:, :]
            u = u + cw[CONV_WIDTH - 1 - s:CONV_WIDTH - s, :] * shifted
        ctx_scr[bb] = xb[tb - SUBLANES:, :]
        ub = u.astype(BF16)
        for n in range(N_RNN_BLOCKS):
            cs = slice(n * RNN_BLOCK, (n + 1) * RNN_BLOCK)
            pre = jnp.dot(ub[:, cs], wai_ref[n], preferred_element_type=F32)
            r = jax.nn.sigmoid(pre[:, :RNN_BLOCK] + ba[:, cs])
            gi = jax.nn.sigmoid(pre[:, RNN_BLOCK:] + bi[:, cs])
            log_a = -LRU_C * r * sp[:, cs]
            a = jnp.exp(log_a)
            mult = jnp.sqrt(1.0 - jnp.exp(2.0 * log_a))
            if first:
                mult = jnp.where((row == 0) & (step == 0), 1.0, mult)
            a_scr[n, pl.ds(bb, tb, stride=nb), :] = a
            b_scr[n, pl.ds(bb, tb, stride=nb), :] = mult * (gi * u[:, cs])
        return carry

    lax.fori_loop(0, nb, slab, 0)

    def tstep(t, h):
        base = pl.multiple_of(t * nb, nb)
        h = a_scr[:, pl.ds(base, nb), :] * h + b_scr[:, pl.ds(base, nb), :]
        hs_scr[:, pl.ds(base, nb), :] = h
        return h

    h_fin = lax.fori_loop(0, tb, tstep, h_scr[...], unroll=8)
    h_scr[...] = h_fin
    hl_ref[...] = h_fin

    def gate(bb, carry):
        hb = jnp.concatenate([hs_scr[n, pl.ds(bb, tb, stride=nb), :] for n in range(nct)], axis=1)
        o_ref[bb] = (hb * jax.nn.gelu(y_ref[bb].astype(F32))).astype(o_ref.dtype)
        return carry

    lax.fori_loop(0, nb, gate, 0)


def _rglru(proj3, ctx0, h0, cw, cb, wai, ba, bi, lam, *, first):
    nb, s, _ = proj3.shape
    tb = min(s, 128)
    c = D_RNN
    nct = c // LANES
    kern = functools.partial(_rglru_kernel, nb=nb, tb=tb, first=first)
    full = lambda shape: pl.BlockSpec(shape, lambda t: (0,) * len(shape))
    return pl.pallas_call(
        kern,
        grid=(s // tb,),
        in_specs=[
            pl.BlockSpec((nb, tb, c), lambda t: (0, t, COL_XR)),
            pl.BlockSpec((nb, tb, c), lambda t: (0, t, COL_YR)),
            full((SUBLANES, c)), full((nct, nb, LANES)), full((CONV_WIDTH, c)), full((1, c)),
            full((N_RNN_BLOCKS, RNN_BLOCK, 2 * RNN_BLOCK)), full((1, c)), full((1, c)), full((1, c)),
        ],
        out_specs=[pl.BlockSpec((nb, tb, c), lambda t: (0, t, 0)), full((nct, nb, LANES))],
        out_shape=[jax.ShapeDtypeStruct((nb, s, c), BF16), jax.ShapeDtypeStruct((nct, nb, LANES), F32)],
        scratch_shapes=[
            pltpu.VMEM((nb, SUBLANES, c), F32),
            pltpu.VMEM((nct, nb * tb, LANES), F32),
            pltpu.VMEM((nct, nb * tb, LANES), F32),
            pltpu.VMEM((nct, nb * tb, LANES), F32),
            pltpu.VMEM((nct, nb, LANES), F32),
        ],
        compiler_params=_cparams(("arbitrary",)),
        name="rglru_first" if first else "rglru",
    )(proj3, proj3, ctx0, h0, cw, cb, wai, ba, bi, lam)


def _attn_kernel(lq1_ref, lk1_ref, lq2_ref, lk2_ref, g_ref, q_ref, k_ref, v_ref, km_ref, vm_ref,
                 o_ref, m_scr, l_scr, acc_scr, *, tq):
    qi = pl.program_id(2)
    lane = lax.broadcasted_iota(jnp.int32, (1, V_DIM), 1)
    q = q_ref[...] * jnp.asarray(HEAD_DIM ** -0.5, BF16)
    zero = jnp.zeros_like(q)
    qs = (jnp.where(lane < HEAD_DIM, q, zero), jnp.where(lane >= HEAD_DIM, q, zero))

    m_scr[...] = jnp.full(m_scr.shape, NEG_BIG, F32)
    l_scr[...] = jnp.zeros(l_scr.shape, F32)
    acc_scr[...] = jnp.zeros(acc_scr.shape, F32)

    def update(kblk, vblk, mask):
        width = kblk.shape[0]
        for mp in range(2):
            s = lax.dot_general(qs[mp], kblk, (((1,), (1,)), ((), ())), preferred_element_type=F32)
            if mask is not None:
                s = jnp.where(mask, s, NEG_BIG)
            m_prev = m_scr[mp]
            m_next = jnp.maximum(m_prev, jnp.max(s, axis=1, keepdims=True))
            p = jnp.exp(s - jnp.tile(m_next, (1, width // LANES)))
            corr = jnp.exp(m_prev - m_next)
            l_scr[mp] = corr * l_scr[mp] + jnp.sum(p, axis=1, keepdims=True)
            acc_scr[mp] = corr * acc_scr[mp] + jnp.dot(p.astype(BF16), vblk, preferred_element_type=F32)
            m_scr[mp] = m_next

    mcol = lax.broadcasted_iota(jnp.int32, (tq, LANES), 1)
    update(km_ref[...], vm_ref[...], mcol < N_META)

    def body(j, carry):
        off = pl.multiple_of(j * tq, tq)
        update(k_ref[pl.ds(off, tq), :], v_ref[pl.ds(off, tq), :], None)
        return carry

    lax.fori_loop(0, qi, body, 0)

    off = pl.multiple_of(qi * tq, tq)
    r_i = lax.broadcasted_iota(jnp.int32, (tq, tq), 0)
    c_i = lax.broadcasted_iota(jnp.int32, (tq, tq), 1)
    update(k_ref[pl.ds(off, tq), :], v_ref[pl.ds(off, tq), :], c_i <= r_i)

    lam = (jnp.exp(jnp.sum(lq1_ref[...] * lk1_ref[...], axis=1, keepdims=True))
           - jnp.exp(jnp.sum(lq2_ref[...] * lk2_ref[...], axis=1, keepdims=True)) + LAMBDA_INIT)
    o = acc_scr[0] / l_scr[0] - lam * (acc_scr[1] / l_scr[1])
    o = o * lax.rsqrt(jnp.mean(o * o, axis=-1, keepdims=True) + LN_EPS) * g_ref[...]
    o_ref[...] = (o * (1.0 - LAMBDA_INIT)).astype(o_ref.dtype)


def _diff_attn(proj3, km, vm, lq1, lk1, lq2, lk2, g):
    nb, s, _ = proj3.shape
    tq = min(s, 512)
    hb = D_ATTN // V_DIM
    small = lambda shape: pl.BlockSpec(shape, lambda b, h, i: (0,) * len(shape))
    return pl.pallas_call(
        functools.partial(_attn_kernel, tq=tq),
        grid=(nb, N_HEADS, s // tq),
        in_specs=[
            small((1, HEAD_DIM)), small((1, HEAD_DIM)), small((1, HEAD_DIM)), small((1, HEAD_DIM)),
            small((1, V_DIM)),
            pl.BlockSpec((None, tq, V_DIM), lambda b, h, i: (b, i, COL_Q * hb + h)),
            pl.BlockSpec((None, s, V_DIM), lambda b, h, i: (b, 0, COL_K * hb + h)),
            pl.BlockSpec((None, s, V_DIM), lambda b, h, i: (b, 0, COL_V * hb + h)),
            pl.BlockSpec((LANES, V_DIM), lambda b, h, i: (0, h)),
            pl.BlockSpec((LANES, V_DIM), lambda b, h, i: (0, h)),
        ],
        out_specs=pl.BlockSpec((None, tq, V_DIM), lambda b, h, i: (b, i, h)),
        out_shape=jax.ShapeDtypeStruct((nb, s, D_ATTN), BF16),
        scratch_shapes=[
            pltpu.VMEM((2, tq, LANES), F32),
            pltpu.VMEM((2, tq, LANES), F32),
            pltpu.VMEM((2, tq, V_DIM), F32),
        ],
        compiler_params=_cparams(("parallel", "parallel", "arbitrary")),
        name="diff_attn",
    )(lq1, lk1, lq2, lk2, g, proj3, proj3, proj3, km, vm)


def _merge_kernel(x_ref, rec_ref, att_ref, gr_ref, ga_ref, bgr_ref, bga_ref, wr_ref, wa_ref, wo_ref,
                  eg_ref, eb_ref, g1_ref, b1_ref, h_ref, hb_ref):
    r = jnp.dot(rec_ref[...], wr_ref[...], preferred_element_type=F32)
    a = jnp.dot(att_ref[...], wa_ref[...], preferred_element_type=F32)
    g_rec = jax.nn.sigmoid(gr_ref[...].astype(F32) + bgr_ref[...])
    g_att = jax.nn.sigmoid(ga_ref[...].astype(F32) + bga_ref[...])
    mix = jnp.dot((g_rec * r + g_att * a).astype(BF16), wo_ref[...], preferred_element_type=F32)
    h0 = _layer_norm_rows(x_ref[...], eg_ref[...], eb_ref[...])
    h1 = _layer_norm_rows(ALPHA * h0 + mix, g1_ref[...], b1_ref[...])
    h_ref[...] = h1
    hb_ref[...] = h1.astype(BF16)


def _merge(x2d, rec2d, att2d, proj2d, bgr, bga, wr, wa, wo, eg, eb, g1, b1):
    n, d = x2d.shape
    tm = min(n, 512)
    row = lambda col: pl.BlockSpec((tm, d), lambda i: (i, col))
    vec = pl.BlockSpec((1, d), lambda i: (0, 0))
    mat = pl.BlockSpec((d, d), lambda i: (0, 0))
    return pl.pallas_call(
        _merge_kernel,
        grid=(n // tm,),
        in_specs=[row(0), row(0), row(0), row(COL_GREC), row(COL_GATT), vec, vec, mat, mat, mat,
                  vec, vec, vec, vec],
        out_specs=[row(0), row(0)],
        out_shape=[jax.ShapeDtypeStruct((n, d), F32), jax.ShapeDtypeStruct((n, d), BF16)],
        compiler_params=_cparams(("parallel",)),
        name="merge_ln1",
    )(x2d, rec2d, att2d, proj2d, proj2d, bgr, bga, wr, wa, wo, eg, eb, g1, b1)


def _router_kernel(h_ref, wt_ref, bias_ref, e_ref, w_ref):
    tm = h_ref.shape[0]
    logits = lax.dot_general(wt_ref[...], h_ref[...], (((1,), (1,)), ((), ())),
                             precision=lax.Precision.HIGHEST, preferred_element_type=F32)
    scores = jax.nn.sigmoid(logits)
    sel = scores + bias_ref[...]
    grp = sel.reshape(N_GROUPS, GROUP_SIZE, tm)
    gi = lax.broadcasted_iota(jnp.int32, grp.shape, 1)
    m1 = jnp.max(grp, axis=1, keepdims=True)
    first = jnp.min(jnp.where(grp == m1, gi, GROUP_SIZE), axis=1, keepdims=True)
    m2 = jnp.max(jnp.where(gi == first, -jnp.inf, grp), axis=1, keepdims=True)
    gscore = (m1 + m2).reshape(N_GROUPS, tm)

    gidx = lax.broadcasted_iota(jnp.int32, (N_GROUPS, tm), 0)
    keep = jnp.zeros((N_GROUPS, tm), jnp.bool_)
    cur = gscore
    for _ in range(TOPK_GROUPS):
        mx = jnp.max(cur, axis=0, keepdims=True)
        pick = gidx == jnp.min(jnp.where(cur == mx, gidx, N_GROUPS), axis=0, keepdims=True)
        keep = keep | pick
        cur = jnp.where(pick, -jnp.inf, cur)

    cur = jnp.where(keep[:, None, :], grp, -jnp.inf).reshape(N_EXPERTS, tm)
    eidx = lax.broadcasted_iota(jnp.int32, (N_EXPERTS, tm), 0)
    picked_e, picked_w = [], []
    for _ in range(TOP_K):
        mx = jnp.max(cur, axis=0, keepdims=True)
        ei = jnp.min(jnp.where(cur == mx, eidx, N_EXPERTS), axis=0, keepdims=True)
        pick = eidx == ei
        picked_e.append(ei)
        picked_w.append(jnp.sum(jnp.where(pick, scores, 0.0), axis=0, keepdims=True))
        cur = jnp.where(pick, -jnp.inf, cur)
    w = jnp.concatenate(picked_w, axis=0)
    e_ref[...] = jnp.concatenate(picked_e, axis=0)
    w_ref[...] = w / jnp.sum(w, axis=0, keepdims=True) * ROUTED_SCALE


def _router(h1, router_wt, bias_col):
    n, d = h1.shape
    tm = min(n, 512)
    return pl.pallas_call(
        _router_kernel,
        grid=(n // tm,),
        in_specs=[
            pl.BlockSpec((tm, d), lambda i: (i, 0)),
            pl.BlockSpec((N_EXPERTS, d), lambda i: (0, 0)),
            pl.BlockSpec((N_EXPERTS, 1), lambda i: (0, 0)),
        ],
        out_specs=[pl.BlockSpec((TOP_K, tm), lambda i: (0, i)), pl.BlockSpec((TOP_K, tm), lambda i: (0, i))],
        out_shape=[jax.ShapeDtypeStruct((TOP_K, n), jnp.int32), jax.ShapeDtypeStruct((TOP_K, n), F32)],
        compiler_params=_cparams(("parallel",)),
        name="router",
    )(h1, router_wt, bias_col)


EXPERT_BLOCK = 256


def _experts_kernel(bexp_ref, nused_ref, x_ref, sw_ref, wg_ref, wu_ref, wd_ref, o_ref):
    @pl.when(pl.program_id(0) < nused_ref[0])
    def _():
        x = x_ref[...]
        g = jnp.dot(x, wg_ref[...].astype(BF16), preferred_element_type=F32)
        u = jnp.dot(x, wu_ref[...].astype(BF16), preferred_element_type=F32)
        hmid = (jax.nn.silu(g) * u).astype(BF16)
        y = jnp.dot(hmid, wd_ref[...].astype(BF16), preferred_element_type=F32)
        o_ref[...] = (y * sw_ref[...]).astype(o_ref.dtype)


def _experts(block_exp, n_used, xs, slot_w, wg, wu, wd):
    n_slots, d = xs.shape
    n_blocks = n_slots // EXPERT_BLOCK
    blk = lambda i, be, nu: (jnp.minimum(i, nu[0] - 1), 0)
    wsel = lambda i, be, nu: (be[jnp.minimum(i, nu[0] - 1)], 0, 0)
    grid_spec = pltpu.PrefetchScalarGridSpec(
        num_scalar_prefetch=2,
        grid=(n_blocks,),
        in_specs=[
            pl.BlockSpec((EXPERT_BLOCK, d), blk),
            pl.BlockSpec((EXPERT_BLOCK, 1), blk),
            pl.BlockSpec((None, d, D_EXPERT), wsel),
            pl.BlockSpec((None, d, D_EXPERT), wsel),
            pl.BlockSpec((None, D_EXPERT, d), wsel),
        ],
        out_specs=pl.BlockSpec((EXPERT_BLOCK, d), blk),
    )
    return pl.pallas_call(
        _experts_kernel,
        grid_spec=grid_spec,
        out_shape=jax.ShapeDtypeStruct((n_slots, d), BF16),
        compiler_params=_cparams(("arbitrary",)),
        name="experts",
    )(block_exp, n_used, xs, slot_w, wg, wu, wd)


def _final_kernel(h_ref, hb_ref, routed_ref, wgu_ref, wd_ref, g_ref, b_ref, o_ref):
    gu = jnp.dot(hb_ref[...], wgu_ref[...], preferred_element_type=F32)
    hmid = (jax.nn.silu(gu[:, :D_SHARED]) * gu[:, D_SHARED:]).astype(BF16)
    shared = jnp.dot(hmid, wd_ref[...], preferred_element_type=F32)
    ffn = routed_ref[...].astype(F32) + shared
    o_ref[...] = _layer_norm_rows(ALPHA * h_ref[...] + ffn, g_ref[...], b_ref[...])


def _final(h1, h1b, routed, wgu, wd, g, b):
    n, d = h1.shape
    tm = min(n, 512)
    row = pl.BlockSpec((tm, d), lambda i: (i, 0))
    vec = pl.BlockSpec((1, d), lambda i: (0, 0))
    return pl.pallas_call(
        _final_kernel,
        grid=(n // tm,),
        in_specs=[row, row, row, pl.BlockSpec((d, 2 * D_SHARED), lambda i: (0, 0)),
                  pl.BlockSpec((D_SHARED, d), lambda i: (0, 0)), vec, vec],
        out_specs=row,
        out_shape=jax.ShapeDtypeStruct((n, d), F32),
        compiler_params=_cparams(("parallel",)),
        name="shared_ln2",
    )(h1, h1b, routed, wgu, wd, g, b)


def _dispatch(eidx):
    k, n = eidx.shape
    n_assign = k * n
    e_flat = eidx.reshape(n_assign)
    order = jnp.argsort(e_flat).astype(jnp.int32)
    e_sorted = e_flat[order]
    counts = jnp.bincount(e_flat, length=N_EXPERTS).astype(jnp.int32)
    padded = (counts + EXPERT_BLOCK - 1) // EXPERT_BLOCK * EXPERT_BLOCK
    start = jnp.cumsum(counts) - counts
    pend = jnp.cumsum(padded)
    pstart = pend - padded
    dest = pstart[e_sorted] + (jnp.arange(n_assign, dtype=jnp.int32) - start[e_sorted])
    n_blocks = n_assign // EXPERT_BLOCK + N_EXPERTS
    n_slots = n_blocks * EXPERT_BLOCK
    slot_src = jnp.zeros((n_slots,), jnp.int32).at[dest].set(order)
    slot_valid = jnp.zeros((n_slots,), jnp.bool_).at[dest].set(True)
    slot_of = jnp.zeros((n_assign,), jnp.int32).at[order].set(dest)
    block_start = jnp.arange(n_blocks, dtype=jnp.int32) * EXPERT_BLOCK
    block_exp = jnp.minimum(jnp.searchsorted(pend, block_start, side='right'), N_EXPERTS - 1).astype(jnp.int32)
    n_used = (pend[-1] // EXPERT_BLOCK).astype(jnp.int32).reshape(1)
    return slot_src, slot_valid, slot_of, block_exp, n_used


def kernel(x, meta_tokens, emb_ln_g, emb_ln_b, w_in, b_gate, conv_w, conv_b, lru_w_a, lru_b_a, lru_w_i, lru_b_i, lru_lambda, lam_q1, lam_k1, lam_q2, lam_k2, subln_g, w_rec_proj, w_attn_proj, w_o, ln1_g, ln1_b, router_w, router_bias, w_e_gate, w_e_up, w_e_down, w_s_gate, w_s_up, w_s_down, ln2_g, ln2_b):
    nb, s, d = x.shape
    n = nb * s
    row = lambda v: v.reshape(1, -1).astype(F32)
    l = 0

    x2d = x.reshape(n, d)
    w_in_b = w_in[l].astype(BF16)
    eg, eb = row(emb_ln_g), row(emb_ln_b)
    proj = _ln_inproj(x2d, eg, eb, w_in_b)
    projm = _ln_inproj(meta_tokens.astype(F32), eg, eb, w_in_b)
    proj3 = proj.reshape(nb, s, IN_COLS)

    wai = jnp.concatenate([lru_w_a[l], lru_w_i[l]], axis=-1).astype(BF16)
    lru_args = (conv_w[l].astype(F32), row(conv_b[l]), wai, row(lru_b_a[l]), row(lru_b_i[l]),
                row(lru_lambda[l]))
    zeros8 = jnp.zeros((SUBLANES, D_RNN), F32)
    nct = D_RNN // LANES
    _, h_meta = _rglru(projm.reshape(1, N_META, IN_COLS), zeros8, jnp.zeros((nct, 1, LANES), F32),
                       *lru_args, first=True)
    ctx0 = projm[N_META - SUBLANES:, :D_RNN].astype(F32)
    rec, _ = _rglru(proj3, ctx0, jnp.broadcast_to(h_meta, (nct, nb, LANES)), *lru_args, first=False)

    pad = ((0, LANES - N_META), (0, 0))
    km = jnp.pad(projm[:, COL_K * D_ATTN:(COL_K + 1) * D_ATTN], pad)
    vm = jnp.pad(projm[:, COL_V * D_ATTN:(COL_V + 1) * D_ATTN], pad)
    att = _diff_attn(proj3, km, vm, row(lam_q1[l]), row(lam_k1[l]), row(lam_q2[l]), row(lam_k2[l]),
                     row(subln_g[l]))

    bg = b_gate[l].astype(F32)
    h1, h1b = _merge(x2d, rec.reshape(n, D_RNN), att.reshape(n, D_ATTN), proj,
                     bg[:D_MODEL].reshape(1, -1), bg[D_MODEL:].reshape(1, -1),
                     w_rec_proj[l].astype(BF16), w_attn_proj[l].astype(BF16), w_o[l].astype(BF16),
                     eg, eb, row(ln1_g[l]), row(ln1_b[l]))

    eidx, wts = _router(h1, router_w[l].T.astype(F32), router_bias[l].reshape(-1, 1).astype(F32))

    slot_src, slot_valid, slot_of, block_exp, n_used = _dispatch(eidx)
    slot_tok = slot_src % n
    slot_w = jnp.where(slot_valid, wts.reshape(-1)[slot_src], 0.0).reshape(-1, 1)
    xs = jnp.take(h1b, slot_tok, axis=0)
    ys = _experts(block_exp, n_used, xs, slot_w, w_e_gate[l], w_e_up[l], w_e_down[l])
    routed = jnp.sum(jnp.take(ys, slot_of.reshape(TOP_K, n), axis=0).astype(F32), axis=0)

    wgu = jnp.concatenate([w_s_gate[l], w_s_up[l]], axis=-1).astype(BF16)
    out = _final(h1, h1b, routed, wgu, w_s_down[l].astype(BF16), row(ln2_g[l]), row(ln2_b[l]))
    return out.reshape(nb, s, d)
```

```python
import functools
import math

import jax
import jax.numpy as jnp
from jax import lax
from jax.experimental import pallas as pl
from jax.experimental.pallas import tpu as pltpu
from jax.experimental.pallas import tpu_sc as plsc

F32 = jnp.float32
BF16 = jnp.bfloat16
U32 = jnp.uint32

N_META = 16
D_MODEL = 1024
D_RNN = 1024
N_RNN_BLOCKS = 8
RNN_BLOCK = D_RNN // N_RNN_BLOCKS
CONV_WIDTH = 4
LRU_C = 8.0
N_HEADS = 8
HEAD_DIM = 64
V_DIM = 2 * HEAD_DIM
D_ATTN = N_HEADS * V_DIM
N_EXPERTS = 256
TOP_K = 8
N_GROUPS = 8
GROUP_SIZE = N_EXPERTS // N_GROUPS
TOPK_GROUPS = 4
D_EXPERT = 256
D_SHARED = 256
ROUTED_SCALE = 2.5
LN_EPS = 1e-5
DEPTH = 1
IN_COLS = 2 * D_RNN + 3 * D_ATTN + 2 * D_MODEL
ALPHA = (2.0 * DEPTH) ** 0.25
LAMBDA_INIT = 0.8 - 0.6 * math.exp(-0.3 * 0)

LANES = 128
SUBLANES = 8
VMEM_LIMIT = 56 * 1024 * 1024
NEG_BIG = -1e30

COL_XR, COL_YR, COL_Q, COL_K, COL_V, COL_GREC, COL_GATT = range(7)


def _cparams(sem):
    return pltpu.CompilerParams(dimension_semantics=sem, vmem_limit_bytes=VMEM_LIMIT)


def _layer_norm_rows(x, g, b):
    mu = jnp.mean(x, axis=-1, keepdims=True)
    xc = x - mu
    var = jnp.mean(xc * xc, axis=-1, keepdims=True)
    return xc * lax.rsqrt(var + LN_EPS) * g + b


def _ln_inproj_kernel(x_ref, g_ref, b_ref, w_ref, o_ref, xn_ref):
    @pl.when(pl.program_id(1) == 0)
    def _():
        xn_ref[...] = _layer_norm_rows(x_ref[...], g_ref[...], b_ref[...]).astype(BF16)

    o_ref[...] = jnp.dot(xn_ref[...], w_ref[...], preferred_element_type=F32).astype(o_ref.dtype)


def _ln_inproj(x2d, g, b, w_bf16):
    n, d = x2d.shape
    cols = w_bf16.shape[1]
    tm = min(n, 2048)
    tn = 512
    return pl.pallas_call(
        _ln_inproj_kernel,
        grid=(n // tm, cols // tn),
        in_specs=[
            pl.BlockSpec((tm, d), lambda i, j: (i, 0)),
            pl.BlockSpec((1, d), lambda i, j: (0, 0)),
            pl.BlockSpec((1, d), lambda i, j: (0, 0)),
            pl.BlockSpec((d, tn), lambda i, j: (0, j)),
        ],
        out_specs=pl.BlockSpec((tm, tn), lambda i, j: (i, j)),
        out_shape=jax.ShapeDtypeStruct((n, cols), BF16),
        scratch_shapes=[pltpu.VMEM((tm, d), BF16)],
        compiler_params=_cparams(("parallel", "arbitrary")),
        name="ln_inproj",
    )(x2d, g, b, w_bf16)


def _rglru_kernel(x_ref, y_ref, ctx0_ref, h0_ref, cw_ref, cb_ref, wai_ref, ba_ref, bi_ref,
                  lam_ref, o_ref, hl_ref, ctx_scr, a_scr, b_scr, hs_scr, h_scr, *, nb, tb, first):
    step = pl.program_id(0)
    nct = D_RNN // LANES

    @pl.when(step == 0)
    def _():
        for bb in range(nb):
            ctx_scr[bb] = ctx0_ref[...]
        h_scr[...] = h0_ref[...]

    lam = lam_ref[...]
    sp = jnp.maximum(-lam, 0.0) + jnp.log1p(jnp.exp(-jnp.abs(lam)))
    cw = cw_ref[...]
    cb = cb_ref[...]
    ba = ba_ref[...]
    bi = bi_ref[...]
    row = lax.broadcasted_iota(jnp.int32, (tb, 1), 0)

    def slab(bb, carry):
        xb = x_ref[bb].astype(F32)
        xcat = jnp.concatenate([ctx_scr[bb], xb], axis=0)
        u = cb + cw[CONV_WIDTH - 1:CONV_WIDTH, :] * xb
        for s in range(1, CONV_WIDTH):
            shifted = pltpu.roll(xcat, s, 0)[SUBLANES:, :]
            u = u + cw[CONV_WIDTH - 1 - s:CONV_WIDTH - s, :] * shifted
        ctx_scr[bb] = xb[tb - SUBLANES:, :]
        ub = u.astype(BF16)
        for n in range(N_RNN_BLOCKS):
            cs = slice(n * RNN_BLOCK, (n + 1) * RNN_BLOCK)
            pre = jnp.dot(ub[:, cs], wai_ref[n], preferred_element_type=F32)
            r = jax.nn.sigmoid(pre[:, :RNN_BLOCK] + ba[:, cs])
            gi = jax.nn.sigmoid(pre[:, RNN_BLOCK:] + bi[:, cs])
            log_a = -LRU_C * r * sp[:, cs]
            a = jnp.exp(log_a)
            mult = jnp.sqrt(1.0 - jnp.exp(2.0 * log_a))
            if first:
                mult = jnp.where((row == 0) & (step == 0), 1.0, mult)
            a_scr[n, pl.ds(bb, tb, stride=nb), :] = a
            b_scr[n, pl.ds(bb, tb, stride=nb), :] = mult * (gi * u[:, cs])
        return carry

    lax.fori_loop(0, nb, slab, 0)

    def tstep(t, h):
        base = pl.multiple_of(t * nb, nb)
        h = a_scr[:, pl.ds(base, nb), :] * h + b_scr[:, pl.ds(base, nb), :]
        hs_scr[:, pl.ds(base, nb), :] = h
        return h

    h_fin = lax.fori_loop(0, tb, tstep, h_scr[...], unroll=8)
    h_scr[...] = h_fin
    hl_ref[...] = h_fin

    def gate(bb, carry):
        hb = jnp.concatenate([hs_scr[n, pl.ds(bb, tb, stride=nb), :] for n in range(nct)], axis=1)
        o_ref[bb] = (hb * jax.nn.gelu(y_ref[bb].astype(F32))).astype(o_ref.dtype)
        return carry

    lax.fori_loop(0, nb, gate, 0)


def _rglru(proj3, ctx0, h0, cw, cb, wai, ba, bi, lam, *, first):
    nb, s, _ = proj3.shape
    tb = min(s, 128)
    c = D_RNN
    nct = c // LANES
    kern = functools.partial(_rglru_kernel, nb=nb, tb=tb, first=first)
    full = lambda shape: pl.BlockSpec(shape, lambda t: (0,) * len(shape))
    return pl.pallas_call(
        kern,
        grid=(s // tb,),
        in_specs=[
            pl.BlockSpec((nb, tb, c), lambda t: (0, t, COL_XR)),
            pl.BlockSpec((nb, tb, c), lambda t: (0, t, COL_YR)),
            full((SUBLANES, c)), full((nct, nb, LANES)), full((CONV_WIDTH, c)), full((1, c)),
            full((N_RNN_BLOCKS, RNN_BLOCK, 2 * RNN_BLOCK)), full((1, c)), full((1, c)), full((1, c)),
        ],
        out_specs=[pl.BlockSpec((nb, tb, c), lambda t: (0, t, 0)), full((nct, nb, LANES))],
        out_shape=[jax.ShapeDtypeStruct((nb, s, c), BF16), jax.ShapeDtypeStruct((nct, nb, LANES), F32)],
        scratch_shapes=[
            pltpu.VMEM((nb, SUBLANES, c), F32),
            pltpu.VMEM((nct, nb * tb, LANES), F32),
            pltpu.VMEM((nct, nb * tb, LANES), F32),
            pltpu.VMEM((nct, nb * tb, LANES), F32),
            pltpu.VMEM((nct, nb, LANES), F32),
        ],
        compiler_params=_cparams(("arbitrary",)),
        name="rglru_first" if first else "rglru",
    )(proj3, proj3, ctx0, h0, cw, cb, wai, ba, bi, lam)


def _attn_kernel(lq1_ref, lk1_ref, lq2_ref, lk2_ref, g_ref, q_ref, k_ref, v_ref, km_ref, vm_ref,
                 o_ref, m_scr, l_scr, acc_scr, *, tq):
    qi = pl.program_id(2)
    lane = lax.broadcasted_iota(jnp.int32, (1, V_DIM), 1)
    q = q_ref[...] * jnp.asarray(HEAD_DIM ** -0.5, BF16)
    zero = jnp.zeros_like(q)
    qs = (jnp.where(lane < HEAD_DIM, q, zero), jnp.where(lane >= HEAD_DIM, q, zero))

    m_scr[...] = jnp.full(m_scr.shape, NEG_BIG, F32)
    l_scr[...] = jnp.zeros(l_scr.shape, F32)
    acc_scr[...] = jnp.zeros(acc_scr.shape, F32)

    def update(kblk, vblk, mask):
        width = kblk.shape[0]
        for mp in range(2):
            s = lax.dot_general(qs[mp], kblk, (((1,), (1,)), ((), ())), preferred_element_type=F32)
            if mask is not None:
                s = jnp.where(mask, s, NEG_BIG)
            m_prev = m_scr[mp]
            m_next = jnp.maximum(m_prev, jnp.max(s, axis=1, keepdims=True))
            p = jnp.exp(s - jnp.tile(m_next, (1, width // LANES)))
            corr = jnp.exp(m_prev - m_next)
            l_scr[mp] = corr * l_scr[mp] + jnp.sum(p, axis=1, keepdims=True)
            acc_scr[mp] = corr * acc_scr[mp] + jnp.dot(p.astype(BF16), vblk, preferred_element_type=F32)
            m_scr[mp] = m_next

    mcol = lax.broadcasted_iota(jnp.int32, (tq, LANES), 1)
    update(km_ref[...], vm_ref[...], mcol < N_META)

    def body(j, carry):
        off = pl.multiple_of(j * tq, tq)
        update(k_ref[pl.ds(off, tq), :], v_ref[pl.ds(off, tq), :], None)
        return carry

    lax.fori_loop(0, qi, body, 0)

    off = pl.multiple_of(qi * tq, tq)
    r_i = lax.broadcasted_iota(jnp.int32, (tq, tq), 0)
    c_i = lax.broadcasted_iota(jnp.int32, (tq, tq), 1)
    update(k_ref[pl.ds(off, tq), :], v_ref[pl.ds(off, tq), :], c_i <= r_i)

    lam = (jnp.exp(jnp.sum(lq1_ref[...] * lk1_ref[...], axis=1, keepdims=True))
           - jnp.exp(jnp.sum(lq2_ref[...] * lk2_ref[...], axis=1, keepdims=True)) + LAMBDA_INIT)
    o = acc_scr[0] / l_scr[0] - lam * (acc_scr[1] / l_scr[1])
    o = o * lax.rsqrt(jnp.mean(o * o, axis=-1, keepdims=True) + LN_EPS) * g_ref[...]
    o_ref[...] = (o * (1.0 - LAMBDA_INIT)).astype(o_ref.dtype)


def _diff_attn(proj3, km, vm, lq1, lk1, lq2, lk2, g):
    nb, s, _ = proj3.shape
    tq = min(s, 512)
    hb = D_ATTN // V_DIM
    small = lambda shape: pl.BlockSpec(shape, lambda b, h, i: (0,) * len(shape))
    return pl.pallas_call(
        functools.partial(_attn_kernel, tq=tq),
        grid=(nb, N_HEADS, s // tq),
        in_specs=[
            small((1, HEAD_DIM)), small((1, HEAD_DIM)), small((1, HEAD_DIM)), small((1, HEAD_DIM)),
            small((1, V_DIM)),
            pl.BlockSpec((None, tq, V_DIM), lambda b, h, i: (b, i, COL_Q * hb + h)),
            pl.BlockSpec((None, s, V_DIM), lambda b, h, i: (b, 0, COL_K * hb + h)),
            pl.BlockSpec((None, s, V_DIM), lambda b, h, i: (b, 0, COL_V * hb + h)),
            pl.BlockSpec((LANES, V_DIM), lambda b, h, i: (0, h)),
            pl.BlockSpec((LANES, V_DIM), lambda b, h, i: (0, h)),
        ],
        out_specs=pl.BlockSpec((None, tq, V_DIM), lambda b, h, i: (b, i, h)),
        out_shape=jax.ShapeDtypeStruct((nb, s, D_ATTN), BF16),
        scratch_shapes=[
            pltpu.VMEM((2, tq, LANES), F32),
            pltpu.VMEM((2, tq, LANES), F32),
            pltpu.VMEM((2, tq, V_DIM), F32),
        ],
        compiler_params=_cparams(("parallel", "parallel", "arbitrary")),
        name="diff_attn",
    )(lq1, lk1, lq2, lk2, g, proj3, proj3, proj3, km, vm)


def _merge_kernel(x_ref, rec_ref, att_ref, gr_ref, ga_ref, bgr_ref, bga_ref, wr_ref, wa_ref, wo_ref,
                  eg_ref, eb_ref, g1_ref, b1_ref, h_ref, hp_ref):
    r = jnp.dot(rec_ref[...], wr_ref[...], preferred_element_type=F32)
    a = jnp.dot(att_ref[...], wa_ref[...], preferred_element_type=F32)
    g_rec = jax.nn.sigmoid(gr_ref[...].astype(F32) + bgr_ref[...])
    g_att = jax.nn.sigmoid(ga_ref[...].astype(F32) + bga_ref[...])
    mix = jnp.dot((g_rec * r + g_att * a).astype(BF16), wo_ref[...], preferred_element_type=F32)
    h0 = _layer_norm_rows(x_ref[...], eg_ref[...], eb_ref[...])
    h1 = _layer_norm_rows(ALPHA * h0 + mix, g1_ref[...], b1_ref[...])
    h_ref[...] = h1
    hp_ref[...] = _pack_bf16_pairs(h1)


def _pack_bf16_pairs(v):
    half = v.shape[1] // 2
    lo = lax.bitcast_convert_type(v[:, :half].astype(BF16).astype(F32), U32) >> 16
    hi = lax.bitcast_convert_type(v[:, half:].astype(BF16).astype(F32), U32)
    return lo | hi


def _unpack_bf16_pairs(p):
    lo = lax.bitcast_convert_type(p << 16, F32)
    hi = lax.bitcast_convert_type(p & jnp.uint32(0xFFFF0000), F32)
    return lo, hi


def _merge(x2d, rec2d, att2d, proj2d, bgr, bga, wr, wa, wo, eg, eb, g1, b1):
    n, d = x2d.shape
    tm = min(n, 512)
    row = lambda col: pl.BlockSpec((tm, d), lambda i: (i, col))
    vec = pl.BlockSpec((1, d), lambda i: (0, 0))
    mat = pl.BlockSpec((d, d), lambda i: (0, 0))
    return pl.pallas_call(
        _merge_kernel,
        grid=(n // tm,),
        in_specs=[row(0), row(0), row(0), row(COL_GREC), row(COL_GATT), vec, vec, mat, mat, mat,
                  vec, vec, vec, vec],
        out_specs=[row(0), pl.BlockSpec((tm, d // 2), lambda i: (i, 0))],
        out_shape=[jax.ShapeDtypeStruct((n, d), F32), jax.ShapeDtypeStruct((n, d // 2), U32)],
        compiler_params=_cparams(("parallel",)),
        name="merge_ln1",
    )(x2d, rec2d, att2d, proj2d, proj2d, bgr, bga, wr, wa, wo, eg, eb, g1, b1)


def _router_kernel(h_ref, wt_ref, bias_ref, e_ref, w_ref, r_ref, c_ref, cnt_scr):
    tm = h_ref.shape[0]

    @pl.when(pl.program_id(0) == 0)
    def _():
        cnt_scr[...] = jnp.zeros(cnt_scr.shape, F32)

    logits = lax.dot_general(wt_ref[...], h_ref[...], (((1,), (1,)), ((), ())),
                             precision=lax.Precision.HIGHEST, preferred_element_type=F32)
    scores = jax.nn.sigmoid(logits)
    sel = scores + bias_ref[...]
    grp = sel.reshape(N_GROUPS, GROUP_SIZE, tm)
    gi = lax.broadcasted_iota(jnp.int32, grp.shape, 1)
    m1 = jnp.max(grp, axis=1, keepdims=True)
    first = jnp.min(jnp.where(grp == m1, gi, GROUP_SIZE), axis=1, keepdims=True)
    m2 = jnp.max(jnp.where(gi == first, -jnp.inf, grp), axis=1, keepdims=True)
    gscore = (m1 + m2).reshape(N_GROUPS, tm)

    gidx = lax.broadcasted_iota(jnp.int32, (N_GROUPS, tm), 0)
    keep = jnp.zeros((N_GROUPS, tm), jnp.bool_)
    cur = gscore
    for _ in range(TOPK_GROUPS):
        mx = jnp.max(cur, axis=0, keepdims=True)
        pick = gidx == jnp.min(jnp.where(cur == mx, gidx, N_GROUPS), axis=0, keepdims=True)
        keep = keep | pick
        cur = jnp.where(pick, -jnp.inf, cur)

    cur = jnp.where(keep[:, None, :], grp, -jnp.inf).reshape(N_EXPERTS, tm)
    eidx = lax.broadcasted_iota(jnp.int32, (N_EXPERTS, tm), 0)
    picked_e, picked_w, picks = [], [], []
    for _ in range(TOP_K):
        mx = jnp.max(cur, axis=0, keepdims=True)
        ei = jnp.min(jnp.where(cur == mx, eidx, N_EXPERTS), axis=0, keepdims=True)
        pick = eidx == ei
        picked_e.append(ei)
        picked_w.append(jnp.sum(jnp.where(pick, scores, 0.0), axis=0, keepdims=True))
        picks.append(pick)
        cur = jnp.where(pick, -jnp.inf, cur)
    w = jnp.concatenate(picked_w, axis=0)
    e_ref[...] = jnp.concatenate(picked_e, axis=0)
    w_ref[...] = w / jnp.sum(w, axis=0, keepdims=True) * ROUTED_SCALE

    member = functools.reduce(jnp.logical_or, picks)
    t_row = lax.broadcasted_iota(jnp.int32, (tm, tm), 0)
    t_col = lax.broadcasted_iota(jnp.int32, (tm, tm), 1)
    earlier = jnp.where(t_row < t_col, 1.0, 0.0).astype(BF16)
    member_f = jnp.where(member, 1.0, 0.0)
    before = cnt_scr[...] + jnp.dot(member_f.astype(BF16), earlier, preferred_element_type=F32)
    ranks = [jnp.sum(jnp.where(p, before, 0.0), axis=0, keepdims=True) for p in picks]
    r_ref[...] = jnp.concatenate(ranks, axis=0).astype(jnp.int32)
    total = cnt_scr[...] + jnp.sum(member_f, axis=1, keepdims=True)
    cnt_scr[...] = total
    c_ref[...] = total.astype(jnp.int32)


def _router(h1, router_wt, bias_col):
    n, d = h1.shape
    tm = min(n, 512)
    tok = pl.BlockSpec((TOP_K, tm), lambda i: (0, i))
    return pl.pallas_call(
        _router_kernel,
        grid=(n // tm,),
        in_specs=[
            pl.BlockSpec((tm, d), lambda i: (i, 0)),
            pl.BlockSpec((N_EXPERTS, d), lambda i: (0, 0)),
            pl.BlockSpec((N_EXPERTS, 1), lambda i: (0, 0)),
        ],
        out_specs=[tok, tok, tok, pl.BlockSpec((N_EXPERTS, 1), lambda i: (0, 0))],
        out_shape=[jax.ShapeDtypeStruct((TOP_K, n), jnp.int32), jax.ShapeDtypeStruct((TOP_K, n), F32),
                   jax.ShapeDtypeStruct((TOP_K, n), jnp.int32),
                   jax.ShapeDtypeStruct((N_EXPERTS, 1), jnp.int32)],
        scratch_shapes=[pltpu.VMEM((N_EXPERTS, 1), F32)],
        compiler_params=_cparams(("arbitrary",)),
        name="router",
    )(h1, router_wt, bias_col)


def _slots_kernel(e_ref, r_ref, ps_ref, o_ref):
    tm = e_ref.shape[1]
    eidx = lax.broadcasted_iota(jnp.int32, (N_EXPERTS, tm), 0)
    e = e_ref[...]
    ps = ps_ref[...]
    base = [jnp.sum(jnp.where(eidx == e[k:k + 1, :], ps, 0), axis=0, keepdims=True) for k in range(TOP_K)]
    o_ref[...] = jnp.concatenate(base, axis=0) + r_ref[...]


def _slots(eidx, rank, pstart_col):
    k, n = eidx.shape
    tm = min(n, 2048)
    tok = pl.BlockSpec((k, tm), lambda i: (0, i))
    return pl.pallas_call(
        _slots_kernel,
        grid=(n // tm,),
        in_specs=[tok, tok, pl.BlockSpec((N_EXPERTS, 1), lambda i: (0, 0))],
        out_specs=tok,
        out_shape=jax.ShapeDtypeStruct((k, n), jnp.int32),
        compiler_params=_cparams(("parallel",)),
        name="slots",
    )(eidx, rank, pstart_col)


SC_CORES = 2
SC_SUBCORES = 16
SC_WINDOW = 64


def _sc_mesh():
    return plsc.VectorSubcoreMesh(core_axis_name="core", subcore_axis_name="subcore")


def _sc_scatter_rows(rows, slot_flat, n_slots):
    n, d = rows.shape
    m = slot_flat.shape[0]
    per_worker = m // (SC_CORES * SC_SUBCORES)
    assert per_worker % SC_WINDOW == 0 and n % SC_WINDOW == 0

    @pl.kernel(out_type=jax.ShapeDtypeStruct((n_slots, d), rows.dtype), mesh=_sc_mesh(),
               scratch_types=[pltpu.VMEM((SC_WINDOW,), jnp.int32), pltpu.VMEM((SC_WINDOW, d), rows.dtype),
                              pltpu.SemaphoreType.DMA])
    def scatter(x_hbm, i_hbm, o_hbm, idx_v, rows_v, sem):
        wid = lax.axis_index("subcore") * SC_CORES + lax.axis_index("core")
        base = wid * per_worker

        @pl.loop(0, per_worker // SC_WINDOW)
        def _(c):
            off = pl.multiple_of(base + c * SC_WINDOW, SC_WINDOW)
            src = pl.multiple_of(lax.rem(off, n), SC_WINDOW)
            pltpu.sync_copy(i_hbm.at[pl.ds(off, SC_WINDOW)], idx_v)
            pltpu.sync_copy(x_hbm.at[pl.ds(src, SC_WINDOW)], rows_v)
            pltpu.async_copy(rows_v, o_hbm.at[idx_v], sem).wait()

    return scatter(rows, slot_flat)


def _sc_gather_rows(table, slot_flat):
    d = table.shape[1]
    m = slot_flat.shape[0]
    per_worker = m // (SC_CORES * SC_SUBCORES)
    assert per_worker % SC_WINDOW == 0

    @pl.kernel(out_type=jax.ShapeDtypeStruct((m, d), table.dtype), mesh=_sc_mesh(),
               scratch_types=[pltpu.VMEM((SC_WINDOW,), jnp.int32), pltpu.VMEM((SC_WINDOW, d), table.dtype),
                              pltpu.SemaphoreType.DMA])
    def gather(t_hbm, i_hbm, o_hbm, idx_v, rows_v, sem):
        wid = lax.axis_index("subcore") * SC_CORES + lax.axis_index("core")
        base = wid * per_worker

        @pl.loop(0, per_worker // SC_WINDOW)
        def _(c):
            off = pl.multiple_of(base + c * SC_WINDOW, SC_WINDOW)
            pltpu.sync_copy(i_hbm.at[pl.ds(off, SC_WINDOW)], idx_v)
            pltpu.async_copy(t_hbm.at[idx_v], rows_v, sem).wait()
            pltpu.sync_copy(rows_v, o_hbm.at[pl.ds(off, SC_WINDOW)])

    return gather(table, slot_flat)


EXPERT_BLOCK = 256


def _experts_kernel(bexp_ref, bvalid_ref, nused_ref, x_ref, wg_ref, wu_ref, wd_ref, o_ref):
    i = pl.program_id(0)

    @pl.when(i < nused_ref[0])
    def _():
        lo, hi = _unpack_bf16_pairs(x_ref[...])
        x = jnp.concatenate([lo, hi], axis=1)
        rowid = lax.broadcasted_iota(jnp.int32, (EXPERT_BLOCK, 1), 0)
        x = jnp.where(rowid < bvalid_ref[i], x, 0.0).astype(BF16)
        g = jnp.dot(x, wg_ref[...].astype(BF16), preferred_element_type=F32)
        u = jnp.dot(x, wu_ref[...].astype(BF16), preferred_element_type=F32)
        hmid = (jax.nn.silu(g) * u).astype(BF16)
        y = jnp.dot(hmid, wd_ref[...].astype(BF16), preferred_element_type=F32)
        o_ref[...] = _pack_bf16_pairs(y)


def _experts(block_exp, block_valid, n_used, xs, wg, wu, wd):
    n_slots, dh = xs.shape
    d = 2 * dh
    n_blocks = n_slots // EXPERT_BLOCK
    blk = lambda i, be, bv, nu: (jnp.minimum(i, nu[0] - 1), 0)
    wsel = lambda i, be, bv, nu: (be[jnp.minimum(i, nu[0] - 1)], 0, 0)
    grid_spec = pltpu.PrefetchScalarGridSpec(
        num_scalar_prefetch=3,
        grid=(n_blocks,),
        in_specs=[
            pl.BlockSpec((EXPERT_BLOCK, dh), blk),
            pl.BlockSpec((None, d, D_EXPERT), wsel),
            pl.BlockSpec((None, d, D_EXPERT), wsel),
            pl.BlockSpec((None, D_EXPERT, d), wsel),
        ],
        out_specs=pl.BlockSpec((EXPERT_BLOCK, dh), blk),
    )
    return pl.pallas_call(
        _experts_kernel,
        grid_spec=grid_spec,
        out_shape=jax.ShapeDtypeStruct((n_slots, dh), U32),
        compiler_params=_cparams(("arbitrary",)),
        name="experts",
    )(block_exp, block_valid, n_used, xs, wg, wu, wd)


def _final_kernel(h_ref, y_ref, w_ref, wgu_ref, wd_ref, g_ref, b_ref, o_ref):
    h1 = h_ref[...]
    gu = jnp.dot(h1.astype(BF16), wgu_ref[...], preferred_element_type=F32)
    hmid = (jax.nn.silu(gu[:, :D_SHARED]) * gu[:, D_SHARED:]).astype(BF16)
    shared = jnp.dot(hmid, wd_ref[...], preferred_element_type=F32)
    w = w_ref[...]
    r_lo = r_hi = None
    for k in range(TOP_K):
        lo, hi = _unpack_bf16_pairs(y_ref[k])
        wk = w[:, k:k + 1]
        r_lo = wk * lo if r_lo is None else r_lo + wk * lo
        r_hi = wk * hi if r_hi is None else r_hi + wk * hi
    ffn = jnp.concatenate([r_lo, r_hi], axis=1) + shared
    o_ref[...] = _layer_norm_rows(ALPHA * h1 + ffn, g_ref[...], b_ref[...])


def _final(h1, ytok, wtok, wgu, wd, g, b):
    n, d = h1.shape
    tm = min(n, 512)
    row = pl.BlockSpec((tm, d), lambda i: (i, 0))
    vec = pl.BlockSpec((1, d), lambda i: (0, 0))
    return pl.pallas_call(
        _final_kernel,
        grid=(n // tm,),
        in_specs=[row, pl.BlockSpec((TOP_K, tm, d // 2), lambda i: (0, i, 0)),
                  pl.BlockSpec((tm, TOP_K), lambda i: (i, 0)),
                  pl.BlockSpec((d, 2 * D_SHARED), lambda i: (0, 0)),
                  pl.BlockSpec((D_SHARED, d), lambda i: (0, 0)), vec, vec],
        out_specs=row,
        out_shape=jax.ShapeDtypeStruct((n, d), F32),
        compiler_params=_cparams(("parallel",)),
        name="shared_ln2",
    )(h1, ytok, wtok, wgu, wd, g, b)


def _segments(counts, n_assign):
    counts = counts.reshape(N_EXPERTS)
    padded = (counts + EXPERT_BLOCK - 1) // EXPERT_BLOCK * EXPERT_BLOCK
    pend = jnp.cumsum(padded)
    pstart = pend - padded
    n_blocks = n_assign // EXPERT_BLOCK + N_EXPERTS
    block_start = jnp.arange(n_blocks, dtype=jnp.int32) * EXPERT_BLOCK
    block_exp = jnp.minimum(jnp.searchsorted(pend, block_start, side='right'), N_EXPERTS - 1).astype(jnp.int32)
    block_valid = jnp.clip(counts[block_exp] - (block_start - pstart[block_exp]), 0, EXPERT_BLOCK).astype(jnp.int32)
    n_used = (pend[-1] // EXPERT_BLOCK).astype(jnp.int32).reshape(1)
    return pstart.astype(jnp.int32), block_exp, block_valid, n_used, n_blocks * EXPERT_BLOCK


def kernel(x, meta_tokens, emb_ln_g, emb_ln_b, w_in, b_gate, conv_w, conv_b, lru_w_a, lru_b_a, lru_w_i, lru_b_i, lru_lambda, lam_q1, lam_k1, lam_q2, lam_k2, subln_g, w_rec_proj, w_attn_proj, w_o, ln1_g, ln1_b, router_w, router_bias, w_e_gate, w_e_up, w_e_down, w_s_gate, w_s_up, w_s_down, ln2_g, ln2_b):
    nb, s, d = x.shape
    n = nb * s
    row = lambda v: v.reshape(1, -1).astype(F32)
    l = 0

    x2d = x.reshape(n, d)
    w_in_b = w_in[l].astype(BF16)
    eg, eb = row(emb_ln_g), row(emb_ln_b)
    proj = _ln_inproj(x2d, eg, eb, w_in_b)
    projm = _ln_inproj(meta_tokens.astype(F32), eg, eb, w_in_b)
    proj3 = proj.reshape(nb, s, IN_COLS)

    wai = jnp.concatenate([lru_w_a[l], lru_w_i[l]], axis=-1).astype(BF16)
    lru_args = (conv_w[l].astype(F32), row(conv_b[l]), wai, row(lru_b_a[l]), row(lru_b_i[l]),
                row(lru_lambda[l]))
    zeros8 = jnp.zeros((SUBLANES, D_RNN), F32)
    nct = D_RNN // LANES
    _, h_meta = _rglru(projm.reshape(1, N_META, IN_COLS), zeros8, jnp.zeros((nct, 1, LANES), F32),
                       *lru_args, first=True)
    ctx0 = projm[N_META - SUBLANES:, :D_RNN].astype(F32)
    rec, _ = _rglru(proj3, ctx0, jnp.broadcast_to(h_meta, (nct, nb, LANES)), *lru_args, first=False)

    pad = ((0, LANES - N_META), (0, 0))
    km = jnp.pad(projm[:, COL_K * D_ATTN:(COL_K + 1) * D_ATTN], pad)
    vm = jnp.pad(projm[:, COL_V * D_ATTN:(COL_V + 1) * D_ATTN], pad)
    att = _diff_attn(proj3, km, vm, row(lam_q1[l]), row(lam_k1[l]), row(lam_q2[l]), row(lam_k2[l]),
                     row(subln_g[l]))

    bg = b_gate[l].astype(F32)
    h1, h1p = _merge(x2d, rec.reshape(n, D_RNN), att.reshape(n, D_ATTN), proj,
                     bg[:D_MODEL].reshape(1, -1), bg[D_MODEL:].reshape(1, -1),
                     w_rec_proj[l].astype(BF16), w_attn_proj[l].astype(BF16), w_o[l].astype(BF16),
                     eg, eb, row(ln1_g[l]), row(ln1_b[l]))

    eidx, wts, rank, counts = _router(h1, router_w[l].T.astype(F32),
                                      router_bias[l].reshape(-1, 1).astype(F32))
    pstart, block_exp, block_valid, n_used, n_slots = _segments(counts, TOP_K * n)
    slot_flat = _slots(eidx, rank, pstart.reshape(-1, 1)).reshape(TOP_K * n)

    xs = _sc_scatter_rows(h1p, slot_flat, n_slots)
    ys = _experts(block_exp, block_valid, n_used, xs, w_e_gate[l], w_e_up[l], w_e_down[l])
    ytok = _sc_gather_rows(ys, slot_flat).reshape(TOP_K, n, d // 2)

    wgu = jnp.concatenate([w_s_gate[l], w_s_up[l]], axis=-1).astype(BF16)
    out = _final(h1, ytok, wts.T, wgu, w_s_down[l].astype(BF16), row(ln2_g[l]), row(ln2_b[l]))
    return out.reshape(nb, s, d)
```

```python
import functools
import math

import jax
import jax.numpy as jnp
from jax import lax
from jax.experimental import pallas as pl
from jax.experimental.pallas import tpu as pltpu
from jax.experimental.pallas import tpu_sc as plsc

F32 = jnp.float32
BF16 = jnp.bfloat16
U32 = jnp.uint32

N_META = 16
D_MODEL = 1024
D_RNN = 1024
N_RNN_BLOCKS = 8
RNN_BLOCK = D_RNN // N_RNN_BLOCKS
CONV_WIDTH = 4
LRU_C = 8.0
N_HEADS = 8
HEAD_DIM = 64
V_DIM = 2 * HEAD_DIM
D_ATTN = N_HEADS * V_DIM
N_EXPERTS = 256
TOP_K = 8
N_GROUPS = 8
GROUP_SIZE = N_EXPERTS // N_GROUPS
TOPK_GROUPS = 4
D_EXPERT = 256
D_SHARED = 256
ROUTED_SCALE = 2.5
LN_EPS = 1e-5
DEPTH = 1
IN_COLS = 2 * D_RNN + 3 * D_ATTN + 2 * D_MODEL
ALPHA = (2.0 * DEPTH) ** 0.25
LAMBDA_INIT = 0.8 - 0.6 * math.exp(-0.3 * 0)

LANES = 128
SUBLANES = 8
VMEM_LIMIT = 56 * 1024 * 1024
NEG_BIG = -1e30
LOG2_E = math.log2(math.e)

COL_XR, COL_YR, COL_Q, COL_K, COL_V, COL_GREC, COL_GATT = range(7)


def _cparams(sem):
    return pltpu.CompilerParams(dimension_semantics=sem, vmem_limit_bytes=VMEM_LIMIT)


def _layer_norm_rows(x, g, b):
    mu = jnp.mean(x, axis=-1, keepdims=True)
    xc = x - mu
    var = jnp.mean(xc * xc, axis=-1, keepdims=True)
    return xc * lax.rsqrt(var + LN_EPS) * g + b


def _ln_inproj_kernel(x_ref, g_ref, b_ref, w_ref, o_ref, xn_ref):
    @pl.when(pl.program_id(1) == 0)
    def _():
        xn_ref[...] = _layer_norm_rows(x_ref[...], g_ref[...], b_ref[...]).astype(BF16)

    o_ref[...] = jnp.dot(xn_ref[...], w_ref[...], preferred_element_type=F32).astype(o_ref.dtype)


def _ln_inproj(x2d, g, b, w_bf16):
    n, d = x2d.shape
    cols = w_bf16.shape[1]
    tm = min(n, 2048)
    tn = 512
    return pl.pallas_call(
        _ln_inproj_kernel,
        grid=(n // tm, cols // tn),
        in_specs=[
            pl.BlockSpec((tm, d), lambda i, j: (i, 0)),
            pl.BlockSpec((1, d), lambda i, j: (0, 0)),
            pl.BlockSpec((1, d), lambda i, j: (0, 0)),
            pl.BlockSpec((d, tn), lambda i, j: (0, j)),
        ],
        out_specs=pl.BlockSpec((tm, tn), lambda i, j: (i, j)),
        out_shape=jax.ShapeDtypeStruct((n, cols), BF16),
        scratch_shapes=[pltpu.VMEM((tm, d), BF16)],
        compiler_params=_cparams(("parallel", "arbitrary")),
        name="ln_inproj",
    )(x2d, g, b, w_bf16)


def _rglru_kernel(x_ref, y_ref, ctx0_ref, h0_ref, cw_ref, cb_ref, wai_ref, ba_ref, bi_ref,
                  lam_ref, o_ref, hl_ref, ctx_scr, a_scr, b_scr, hs_scr, h_scr, *, nb, tb, first):
    step = pl.program_id(0)
    nct = D_RNN // LANES

    @pl.when(step == 0)
    def _():
        for bb in range(nb):
            ctx_scr[bb] = ctx0_ref[...]
        h_scr[...] = h0_ref[...]

    lam = lam_ref[...]
    sp = jnp.maximum(-lam, 0.0) + jnp.log1p(jnp.exp(-jnp.abs(lam)))
    cw = cw_ref[...]
    cb = cb_ref[...]
    ba = ba_ref[...]
    bi = bi_ref[...]
    row = lax.broadcasted_iota(jnp.int32, (tb, 1), 0)

    def slab(bb, carry):
        xb = x_ref[bb].astype(F32)
        xcat = jnp.concatenate([ctx_scr[bb], xb], axis=0)
        u = cb + cw[CONV_WIDTH - 1:CONV_WIDTH, :] * xb
        for s in range(1, CONV_WIDTH):
            shifted = pltpu.roll(xcat, s, 0)[SUBLANES:, :]
            u = u + cw[CONV_WIDTH - 1 - s:CONV_WIDTH - s, :] * shifted
        ctx_scr[bb] = xb[tb - SUBLANES:, :]
        ub = u.astype(BF16)
        for n in range(N_RNN_BLOCKS):
            cs = slice(n * RNN_BLOCK, (n + 1) * RNN_BLOCK)
            pre = jnp.dot(ub[:, cs], wai_ref[n], preferred_element_type=F32)
            r = jax.nn.sigmoid(pre[:, :RNN_BLOCK] + ba[:, cs])
            gi = jax.nn.sigmoid(pre[:, RNN_BLOCK:] + bi[:, cs])
            log_a = -LRU_C * r * sp[:, cs]
            a = jnp.exp(log_a)
            mult = jnp.sqrt(1.0 - jnp.exp(2.0 * log_a))
            if first:
                mult = jnp.where((row == 0) & (step == 0), 1.0, mult)
            a_scr[n, pl.ds(bb, tb, stride=nb), :] = a
            b_scr[n, pl.ds(bb, tb, stride=nb), :] = mult * (gi * u[:, cs])
        return carry

    lax.fori_loop(0, nb, slab, 0)

    def tstep(t, h):
        base = pl.multiple_of(t * nb, nb)
        h = a_scr[:, pl.ds(base, nb), :] * h + b_scr[:, pl.ds(base, nb), :]
        hs_scr[:, pl.ds(base, nb), :] = h
        return h

    h_fin = lax.fori_loop(0, tb, tstep, h_scr[...], unroll=8)
    h_scr[...] = h_fin
    hl_ref[...] = h_fin

    def gate(bb, carry):
        hb = jnp.concatenate([hs_scr[n, pl.ds(bb, tb, stride=nb), :] for n in range(nct)], axis=1)
        o_ref[bb] = (hb * jax.nn.gelu(y_ref[bb].astype(F32))).astype(o_ref.dtype)
        return carry

    lax.fori_loop(0, nb, gate, 0)


def _rglru(proj3, ctx0, h0, cw, cb, wai, ba, bi, lam, *, first):
    nb, s, _ = proj3.shape
    tb = min(s, 128)
    c = D_RNN
    nct = c // LANES
    kern = functools.partial(_rglru_kernel, nb=nb, tb=tb, first=first)
    full = lambda shape: pl.BlockSpec(shape, lambda t: (0,) * len(shape))
    return pl.pallas_call(
        kern,
        grid=(s // tb,),
        in_specs=[
            pl.BlockSpec((nb, tb, c), lambda t: (0, t, COL_XR)),
            pl.BlockSpec((nb, tb, c), lambda t: (0, t, COL_YR)),
            full((SUBLANES, c)), full((nct, nb, LANES)), full((CONV_WIDTH, c)), full((1, c)),
            full((N_RNN_BLOCKS, RNN_BLOCK, 2 * RNN_BLOCK)), full((1, c)), full((1, c)), full((1, c)),
        ],
        out_specs=[pl.BlockSpec((nb, tb, c), lambda t: (0, t, 0)), full((nct, nb, LANES))],
        out_shape=[jax.ShapeDtypeStruct((nb, s, c), BF16), jax.ShapeDtypeStruct((nct, nb, LANES), F32)],
        scratch_shapes=[
            pltpu.VMEM((nb, SUBLANES, c), F32),
            pltpu.VMEM((nct, nb * tb, LANES), F32),
            pltpu.VMEM((nct, nb * tb, LANES), F32),
            pltpu.VMEM((nct, nb * tb, LANES), F32),
            pltpu.VMEM((nct, nb, LANES), F32),
        ],
        compiler_params=_cparams(("arbitrary",)),
        name="rglru_first" if first else "rglru",
    )(proj3, proj3, ctx0, h0, cw, cb, wai, ba, bi, lam)


def _attn_kernel(lq1_ref, lk1_ref, lq2_ref, lk2_ref, g_ref, q_ref, k_ref, v_ref, km_ref, vm_ref,
                 o_ref, m_scr, acc_scr, s_scr, *, tq):
    qi = pl.program_id(2)
    lane = lax.broadcasted_iota(jnp.int32, (1, V_DIM), 1)
    q = (q_ref[...].astype(F32) * (HEAD_DIM ** -0.5 * LOG2_E)).astype(BF16)
    zero = jnp.zeros_like(q)
    q2 = jnp.concatenate([jnp.where(lane < HEAD_DIM, q, zero), jnp.where(lane >= HEAD_DIM, q, zero)], axis=0)

    m_scr[...] = jnp.full(m_scr.shape, NEG_BIG, F32)
    acc_scr[...] = jnp.zeros(acc_scr.shape, F32)

    def scores(kblk):
        return lax.dot_general(q2, kblk, (((1,), (1,)), ((), ())), preferred_element_type=F32)

    def consume(s, vblk, mask):
        width = s.shape[1]
        vext = jnp.concatenate([vblk, jnp.ones_like(vblk)], axis=1)
        if mask is not None:
            s = jnp.where(mask, s, NEG_BIG)
        m_prev = m_scr[...]
        m_next = jnp.maximum(m_prev, jnp.max(s, axis=1, keepdims=True))
        p = jnp.exp2(s - jnp.tile(m_next, (1, width // LANES)))
        corr = jnp.exp2(m_prev - m_next)
        acc_scr[...] = (jnp.tile(corr, (1, 2)) * acc_scr[...]
                        + jnp.dot(p.astype(BF16), vext, preferred_element_type=F32))
        m_scr[...] = m_next

    def kblock(i):
        return k_ref[pl.ds(pl.multiple_of(i * tq, tq), tq), :]

    def vblock(i):
        return v_ref[pl.ds(pl.multiple_of(i * tq, tq), tq), :]

    mcol = lax.broadcasted_iota(jnp.int32, (2 * tq, LANES), 1)
    consume(scores(km_ref[...]), vm_ref[...], mcol < N_META)

    s_scr[0] = scores(kblock(0))

    def body(j, carry):
        s_scr[1] = scores(kblock(2 * j + 1))
        consume(s_scr[0], vblock(2 * j), None)
        s_scr[0] = scores(kblock(2 * j + 2))
        consume(s_scr[1], vblock(2 * j + 1), None)
        return carry

    lax.fori_loop(0, qi // 2, body, 0)

    r_i = lax.broadcasted_iota(jnp.int32, (2 * tq, tq), 0)
    r_i = jnp.where(r_i >= tq, r_i - tq, r_i)
    c_i = lax.broadcasted_iota(jnp.int32, (2 * tq, tq), 1)
    causal = c_i <= r_i

    @pl.when(qi % 2 == 0)
    def _():
        consume(s_scr[0], vblock(qi), causal)

    @pl.when(qi % 2 == 1)
    def _():
        s_scr[1] = scores(kblock(qi))
        consume(s_scr[0], vblock(qi - 1), None)
        consume(s_scr[1], vblock(qi), causal)

    lam = (jnp.exp(jnp.sum(lq1_ref[...] * lk1_ref[...], axis=1, keepdims=True))
           - jnp.exp(jnp.sum(lq2_ref[...] * lk2_ref[...], axis=1, keepdims=True)) + LAMBDA_INIT)
    a1, a2 = acc_scr[:tq, :], acc_scr[tq:, :]
    o = a1[:, :V_DIM] / a1[:, V_DIM:] - lam * (a2[:, :V_DIM] / a2[:, V_DIM:])
    o = o * lax.rsqrt(jnp.mean(o * o, axis=-1, keepdims=True) + LN_EPS) * g_ref[...]
    o_ref[...] = (o * (1.0 - LAMBDA_INIT)).astype(o_ref.dtype)


def _diff_attn(proj3, km, vm, lq1, lk1, lq2, lk2, g):
    nb, s, _ = proj3.shape
    tq = min(s, 512)
    hb = D_ATTN // V_DIM
    small = lambda shape: pl.BlockSpec(shape, lambda b, h, i: (0,) * len(shape))
    return pl.pallas_call(
        functools.partial(_attn_kernel, tq=tq),
        grid=(nb, N_HEADS, s // tq),
        in_specs=[
            small((1, HEAD_DIM)), small((1, HEAD_DIM)), small((1, HEAD_DIM)), small((1, HEAD_DIM)),
            small((1, V_DIM)),
            pl.BlockSpec((None, tq, V_DIM), lambda b, h, i: (b, i, COL_Q * hb + h)),
            pl.BlockSpec((None, s, V_DIM), lambda b, h, i: (b, 0, COL_K * hb + h)),
            pl.BlockSpec((None, s, V_DIM), lambda b, h, i: (b, 0, COL_V * hb + h)),
            pl.BlockSpec((LANES, V_DIM), lambda b, h, i: (0, h)),
            pl.BlockSpec((LANES, V_DIM), lambda b, h, i: (0, h)),
        ],
        out_specs=pl.BlockSpec((None, tq, V_DIM), lambda b, h, i: (b, i, h)),
        out_shape=jax.ShapeDtypeStruct((nb, s, D_ATTN), BF16),
        scratch_shapes=[
            pltpu.VMEM((2 * tq, LANES), F32),
            pltpu.VMEM((2 * tq, 2 * V_DIM), F32),
            pltpu.VMEM((2, 2 * tq, tq), F32),
        ],
        compiler_params=_cparams(("parallel", "parallel", "arbitrary")),
        name="diff_attn",
    )(lq1, lk1, lq2, lk2, g, proj3, proj3, proj3, km, vm)


def _merge_kernel(x_ref, rec_ref, att_ref, gr_ref, ga_ref, bgr_ref, bga_ref, wr_ref, wa_ref, wo_ref,
                  eg_ref, eb_ref, g1_ref, b1_ref, h_ref, hp_ref):
    r = jnp.dot(rec_ref[...], wr_ref[...], preferred_element_type=F32)
    a = jnp.dot(att_ref[...], wa_ref[...], preferred_element_type=F32)
    g_rec = jax.nn.sigmoid(gr_ref[...].astype(F32) + bgr_ref[...])
    g_att = jax.nn.sigmoid(ga_ref[...].astype(F32) + bga_ref[...])
    mix = jnp.dot((g_rec * r + g_att * a).astype(BF16), wo_ref[...], preferred_element_type=F32)
    h0 = _layer_norm_rows(x_ref[...], eg_ref[...], eb_ref[...])
    h1 = _layer_norm_rows(ALPHA * h0 + mix, g1_ref[...], b1_ref[...])
    h_ref[...] = h1
    hp_ref[...] = _pack_bf16_pairs(h1)


def _pack_bf16_pairs(v):
    half = v.shape[1] // 2
    lo = lax.bitcast_convert_type(v[:, :half].astype(BF16).astype(F32), U32) >> 16
    hi = lax.bitcast_convert_type(v[:, half:].astype(BF16).astype(F32), U32)
    return lo | hi


def _unpack_bf16_pairs(p):
    lo = lax.bitcast_convert_type(p << 16, F32)
    hi = lax.bitcast_convert_type(p & jnp.uint32(0xFFFF0000), F32)
    return lo, hi


def _merge(x2d, rec2d, att2d, proj2d, bgr, bga, wr, wa, wo, eg, eb, g1, b1):
    n, d = x2d.shape
    tm = min(n, 512)
    row = lambda col: pl.BlockSpec((tm, d), lambda i: (i, col))
    vec = pl.BlockSpec((1, d), lambda i: (0, 0))
    mat = pl.BlockSpec((d, d), lambda i: (0, 0))
    return pl.pallas_call(
        _merge_kernel,
        grid=(n // tm,),
        in_specs=[row(0), row(0), row(0), row(COL_GREC), row(COL_GATT), vec, vec, mat, mat, mat,
                  vec, vec, vec, vec],
        out_specs=[row(0), pl.BlockSpec((tm, d // 2), lambda i: (i, 0))],
        out_shape=[jax.ShapeDtypeStruct((n, d), F32), jax.ShapeDtypeStruct((n, d // 2), U32)],
        compiler_params=_cparams(("parallel",)),
        name="merge_ln1",
    )(x2d, rec2d, att2d, proj2d, proj2d, bgr, bga, wr, wa, wo, eg, eb, g1, b1)


def _router_kernel(h_ref, wt_ref, bias_ref, e_ref, w_ref, r_ref, c_ref, cnt_scr):
    tm = h_ref.shape[0]

    @pl.when(pl.program_id(0) == 0)
    def _():
        cnt_scr[...] = jnp.zeros(cnt_scr.shape, F32)

    logits = lax.dot_general(wt_ref[...], h_ref[...], (((1,), (1,)), ((), ())),
                             precision=lax.Precision.HIGHEST, preferred_element_type=F32)
    scores = jax.nn.sigmoid(logits)
    sel = scores + bias_ref[...]
    grp = sel.reshape(N_GROUPS, GROUP_SIZE, tm)
    gi = lax.broadcasted_iota(jnp.int32, grp.shape, 1)
    m1 = jnp.max(grp, axis=1, keepdims=True)
    first = jnp.min(jnp.where(grp == m1, gi, GROUP_SIZE), axis=1, keepdims=True)
    m2 = jnp.max(jnp.where(gi == first, -jnp.inf, grp), axis=1, keepdims=True)
    gscore = (m1 + m2).reshape(N_GROUPS, tm)

    gidx = lax.broadcasted_iota(jnp.int32, (N_GROUPS, tm), 0)
    keep = jnp.zeros((N_GROUPS, tm), jnp.bool_)
    cur = gscore
    for _ in range(TOPK_GROUPS):
        mx = jnp.max(cur, axis=0, keepdims=True)
        pick = gidx == jnp.min(jnp.where(cur == mx, gidx, N_GROUPS), axis=0, keepdims=True)
        keep = keep | pick
        cur = jnp.where(pick, -jnp.inf, cur)

    cur = jnp.where(keep[:, None, :], grp, -jnp.inf).reshape(N_EXPERTS, tm)
    eidx = lax.broadcasted_iota(jnp.int32, (N_EXPERTS, tm), 0)
    picked_e, picked_w, picks = [], [], []
    for _ in range(TOP_K):
        mx = jnp.max(cur, axis=0, keepdims=True)
        ei = jnp.min(jnp.where(cur == mx, eidx, N_EXPERTS), axis=0, keepdims=True)
        pick = eidx == ei
        picked_e.append(ei)
        picked_w.append(jnp.sum(jnp.where(pick, scores, 0.0), axis=0, keepdims=True))
        picks.append(pick)
        cur = jnp.where(pick, -jnp.inf, cur)
    w = jnp.concatenate(picked_w, axis=0)
    e_ref[...] = jnp.concatenate(picked_e, axis=0)
    w_ref[...] = w / jnp.sum(w, axis=0, keepdims=True) * ROUTED_SCALE

    member = functools.reduce(jnp.logical_or, picks)
    t_row = lax.broadcasted_iota(jnp.int32, (tm, tm), 0)
    t_col = lax.broadcasted_iota(jnp.int32, (tm, tm), 1)
    earlier = jnp.where(t_row < t_col, 1.0, 0.0).astype(BF16)
    member_f = jnp.where(member, 1.0, 0.0)
    before = cnt_scr[...] + jnp.dot(member_f.astype(BF16), earlier, preferred_element_type=F32)
    ranks = [jnp.sum(jnp.where(p, before, 0.0), axis=0, keepdims=True) for p in picks]
    r_ref[...] = jnp.concatenate(ranks, axis=0).astype(jnp.int32)
    total = cnt_scr[...] + jnp.sum(member_f, axis=1, keepdims=True)
    cnt_scr[...] = total
    c_ref[...] = total.astype(jnp.int32)


def _router(h1, router_wt, bias_col):
    n, d = h1.shape
    tm = min(n, 512)
    tok = pl.BlockSpec((TOP_K, tm), lambda i: (0, i))
    return pl.pallas_call(
        _router_kernel,
        grid=(n // tm,),
        in_specs=[
            pl.BlockSpec((tm, d), lambda i: (i, 0)),
            pl.BlockSpec((N_EXPERTS, d), lambda i: (0, 0)),
            pl.BlockSpec((N_EXPERTS, 1), lambda i: (0, 0)),
        ],
        out_specs=[tok, tok, tok, pl.BlockSpec((N_EXPERTS, 1), lambda i: (0, 0))],
        out_shape=[jax.ShapeDtypeStruct((TOP_K, n), jnp.int32), jax.ShapeDtypeStruct((TOP_K, n), F32),
                   jax.ShapeDtypeStruct((TOP_K, n), jnp.int32),
                   jax.ShapeDtypeStruct((N_EXPERTS, 1), jnp.int32)],
        scratch_shapes=[pltpu.VMEM((N_EXPERTS, 1), F32)],
        compiler_params=_cparams(("arbitrary",)),
        name="router",
    )(h1, router_wt, bias_col)


def _slots_kernel(e_ref, r_ref, ps_ref, o_ref):
    tm = e_ref.shape[1]
    eidx = lax.broadcasted_iota(jnp.int32, (N_EXPERTS, tm), 0)
    e = e_ref[...]
    ps = ps_ref[...]
    base = [jnp.sum(jnp.where(eidx == e[k:k + 1, :], ps, 0), axis=0, keepdims=True) for k in range(TOP_K)]
    o_ref[...] = jnp.concatenate(base, axis=0) + r_ref[...]


def _slots(eidx, rank, pstart_col):
    k, n = eidx.shape
    tm = min(n, 2048)
    tok = pl.BlockSpec((k, tm), lambda i: (0, i))
    return pl.pallas_call(
        _slots_kernel,
        grid=(n // tm,),
        in_specs=[tok, tok, pl.BlockSpec((N_EXPERTS, 1), lambda i: (0, 0))],
        out_specs=tok,
        out_shape=jax.ShapeDtypeStruct((k, n), jnp.int32),
        compiler_params=_cparams(("parallel",)),
        name="slots",
    )(eidx, rank, pstart_col)


SC_CORES = 2
SC_SUBCORES = 16
SC_WINDOW = 64


def _sc_mesh():
    return plsc.VectorSubcoreMesh(core_axis_name="core", subcore_axis_name="subcore")


def _sc_windows(slot_flat):
    m = slot_flat.shape[0]
    n_win = m // SC_WINDOW // (SC_CORES * SC_SUBCORES)
    assert n_win * SC_WINDOW * SC_CORES * SC_SUBCORES == m and n_win % 2 == 0
    return n_win, slot_flat.reshape(m // SC_WINDOW, SC_WINDOW)


def _sc_scratch(n_win, d, dtype):
    return [pltpu.VMEM((n_win, SC_WINDOW), jnp.int32), pltpu.VMEM((2, SC_WINDOW, d), dtype),
            pltpu.SemaphoreType.DMA, pltpu.SemaphoreType.DMA]


def _sc_scatter_rows(rows, slot_flat, n_slots):
    n, d = rows.shape
    n_win, slot_win = _sc_windows(slot_flat)
    assert n % SC_WINDOW == 0

    @pl.kernel(out_type=jax.ShapeDtypeStruct((n_slots, d), rows.dtype), mesh=_sc_mesh(),
               scratch_types=_sc_scratch(n_win, d, rows.dtype))
    def scatter(x_hbm, i_hbm, o_hbm, idx_v, rows_v, sem0, sem1):
        wid = lax.axis_index("subcore") * SC_CORES + lax.axis_index("core")
        first = wid * n_win
        pltpu.sync_copy(i_hbm.at[pl.ds(first, n_win)], idx_v)
        sems = (sem0, sem1)

        @pl.loop(0, n_win // 2)
        def _(pair):
            copies = []
            for b in range(2):
                c = pair * 2 + b
                src = lax.rem((first + c) * SC_WINDOW, n)
                pltpu.sync_copy(x_hbm.at[pl.ds(src, SC_WINDOW)], rows_v.at[b])
                cp = pltpu.make_async_copy(rows_v.at[b], o_hbm.at[idx_v.at[c]], sems[b])
                cp.start()
                copies.append(cp)
            for cp in copies:
                cp.wait()

    return scatter(rows, slot_win)


def _sc_gather_rows(table, slot_flat):
    d = table.shape[1]
    m = slot_flat.shape[0]
    n_win, slot_win = _sc_windows(slot_flat)

    @pl.kernel(out_type=jax.ShapeDtypeStruct((m, d), table.dtype), mesh=_sc_mesh(),
               scratch_types=_sc_scratch(n_win, d, table.dtype))
    def gather(t_hbm, i_hbm, o_hbm, idx_v, rows_v, sem0, sem1):
        wid = lax.axis_index("subcore") * SC_CORES + lax.axis_index("core")
        first = wid * n_win
        pltpu.sync_copy(i_hbm.at[pl.ds(first, n_win)], idx_v)
        sems = (sem0, sem1)

        def fetch(c, b):
            return pltpu.make_async_copy(t_hbm.at[idx_v.at[c]], rows_v.at[b], sems[b])

        def flush(c, b):
            pltpu.sync_copy(rows_v.at[b], o_hbm.at[pl.ds((first + c) * SC_WINDOW, SC_WINDOW)])

        fetch(0, 0).start()

        @pl.loop(0, n_win // 2)
        def _(pair):
            c = pair * 2
            fetch(c + 1, 1).start()
            fetch(c, 0).wait()
            flush(c, 0)

            @pl.when(c + 2 < n_win)
            def _():
                fetch(c + 2, 0).start()

            fetch(c + 1, 1).wait()
            flush(c + 1, 1)

    return gather(table, slot_win)


EXPERT_BLOCK = 512


def _experts_kernel(bexp_ref, bvalid_ref, nused_ref, x_ref, wg_ref, wu_ref, wd_ref, o_ref):
    i = pl.program_id(0)

    @pl.when(i < nused_ref[0])
    def _():
        lo, hi = _unpack_bf16_pairs(x_ref[...])
        x = jnp.concatenate([lo, hi], axis=1)
        rowid = lax.broadcasted_iota(jnp.int32, (EXPERT_BLOCK, 1), 0)
        x = jnp.where(rowid < bvalid_ref[i], x, 0.0).astype(BF16)
        g = jnp.dot(x, wg_ref[...].astype(BF16), preferred_element_type=F32)
        u = jnp.dot(x, wu_ref[...].astype(BF16), preferred_element_type=F32)
        hmid = (jax.nn.silu(g) * u).astype(BF16)
        y = jnp.dot(hmid, wd_ref[...].astype(BF16), preferred_element_type=F32)
        o_ref[...] = _pack_bf16_pairs(y)


def _experts(block_exp, block_valid, n_used, xs, wg, wu, wd):
    n_slots, dh = xs.shape
    d = 2 * dh
    n_blocks = n_slots // EXPERT_BLOCK
    blk = lambda i, be, bv, nu: (jnp.minimum(i, nu[0] - 1), 0)
    wsel = lambda i, be, bv, nu: (be[jnp.minimum(i, nu[0] - 1)], 0, 0)
    grid_spec = pltpu.PrefetchScalarGridSpec(
        num_scalar_prefetch=3,
        grid=(n_blocks,),
        in_specs=[
            pl.BlockSpec((EXPERT_BLOCK, dh), blk),
            pl.BlockSpec((None, d, D_EXPERT), wsel),
            pl.BlockSpec((None, d, D_EXPERT), wsel),
            pl.BlockSpec((None, D_EXPERT, d), wsel),
        ],
        out_specs=pl.BlockSpec((EXPERT_BLOCK, dh), blk),
    )
    return pl.pallas_call(
        _experts_kernel,
        grid_spec=grid_spec,
        out_shape=jax.ShapeDtypeStruct((n_slots, dh), U32),
        compiler_params=_cparams(("arbitrary",)),
        name="experts",
    )(block_exp, block_valid, n_used, xs, wg, wu, wd)


def _final_kernel(h_ref, y_ref, w_ref, wgu_ref, wd_ref, g_ref, b_ref, o_ref):
    h1 = h_ref[...]
    gu = jnp.dot(h1.astype(BF16), wgu_ref[...], preferred_element_type=F32)
    hmid = (jax.nn.silu(gu[:, :D_SHARED]) * gu[:, D_SHARED:]).astype(BF16)
    shared = jnp.dot(hmid, wd_ref[...], preferred_element_type=F32)
    w = w_ref[...]
    r_lo = r_hi = None
    for k in range(TOP_K):
        lo, hi = _unpack_bf16_pairs(y_ref[k])
        wk = w[:, k:k + 1]
        r_lo = wk * lo if r_lo is None else r_lo + wk * lo
        r_hi = wk * hi if r_hi is None else r_hi + wk * hi
    ffn = jnp.concatenate([r_lo, r_hi], axis=1) + shared
    o_ref[...] = _layer_norm_rows(ALPHA * h1 + ffn, g_ref[...], b_ref[...])


def _final(h1, ytok, wtok, wgu, wd, g, b):
    n, d = h1.shape
    tm = min(n, 512)
    row = pl.BlockSpec((tm, d), lambda i: (i, 0))
    vec = pl.BlockSpec((1, d), lambda i: (0, 0))
    return pl.pallas_call(
        _final_kernel,
        grid=(n // tm,),
        in_specs=[row, pl.BlockSpec((TOP_K, tm, d // 2), lambda i: (0, i, 0)),
                  pl.BlockSpec((tm, TOP_K), lambda i: (i, 0)),
                  pl.BlockSpec((d, 2 * D_SHARED), lambda i: (0, 0)),
                  pl.BlockSpec((D_SHARED, d), lambda i: (0, 0)), vec, vec],
        out_specs=row,
        out_shape=jax.ShapeDtypeStruct((n, d), F32),
        compiler_params=_cparams(("parallel",)),
        name="shared_ln2",
    )(h1, ytok, wtok, wgu, wd, g, b)


def _segments(counts, n_assign):
    counts = counts.reshape(N_EXPERTS)
    padded = (counts + EXPERT_BLOCK - 1) // EXPERT_BLOCK * EXPERT_BLOCK
    pend = jnp.cumsum(padded)
    pstart = pend - padded
    n_blocks = n_assign // EXPERT_BLOCK + N_EXPERTS
    block_start = jnp.arange(n_blocks, dtype=jnp.int32) * EXPERT_BLOCK
    block_exp = jnp.minimum(jnp.searchsorted(pend, block_start, side='right'), N_EXPERTS - 1).astype(jnp.int32)
    block_valid = jnp.clip(counts[block_exp] - (block_start - pstart[block_exp]), 0, EXPERT_BLOCK).astype(jnp.int32)
    n_used = (pend[-1] // EXPERT_BLOCK).astype(jnp.int32).reshape(1)
    return pstart.astype(jnp.int32), block_exp, block_valid, n_used, n_blocks * EXPERT_BLOCK


def kernel(x, meta_tokens, emb_ln_g, emb_ln_b, w_in, b_gate, conv_w, conv_b, lru_w_a, lru_b_a, lru_w_i, lru_b_i, lru_lambda, lam_q1, lam_k1, lam_q2, lam_k2, subln_g, w_rec_proj, w_attn_proj, w_o, ln1_g, ln1_b, router_w, router_bias, w_e_gate, w_e_up, w_e_down, w_s_gate, w_s_up, w_s_down, ln2_g, ln2_b):
    nb, s, d = x.shape
    n = nb * s
    row = lambda v: v.reshape(1, -1).astype(F32)
    l = 0

    x2d = x.reshape(n, d)
    w_in_b = w_in[l].astype(BF16)
    eg, eb = row(emb_ln_g), row(emb_ln_b)
    proj = _ln_inproj(x2d, eg, eb, w_in_b)
    projm = _ln_inproj(meta_tokens.astype(F32), eg, eb, w_in_b)
    proj3 = proj.reshape(nb, s, IN_COLS)

    wai = jnp.concatenate([lru_w_a[l], lru_w_i[l]], axis=-1).astype(BF16)
    lru_args = (conv_w[l].astype(F32), row(conv_b[l]), wai, row(lru_b_a[l]), row(lru_b_i[l]),
                row(lru_lambda[l]))
    zeros8 = jnp.zeros((SUBLANES, D_RNN), F32)
    nct = D_RNN // LANES
    _, h_meta = _rglru(projm.reshape(1, N_META, IN_COLS), zeros8, jnp.zeros((nct, 1, LANES), F32),
                       *lru_args, first=True)
    ctx0 = projm[N_META - SUBLANES:, :D_RNN].astype(F32)
    rec, _ = _rglru(proj3, ctx0, jnp.broadcast_to(h_meta, (nct, nb, LANES)), *lru_args, first=False)

    pad = ((0, LANES - N_META), (0, 0))
    km = jnp.pad(projm[:, COL_K * D_ATTN:(COL_K + 1) * D_ATTN], pad)
    vm = jnp.pad(projm[:, COL_V * D_ATTN:(COL_V + 1) * D_ATTN], pad)
    att = _diff_attn(proj3, km, vm, row(lam_q1[l]), row(lam_k1[l]), row(lam_q2[l]), row(lam_k2[l]),
                     row(subln_g[l]))

    bg = b_gate[l].astype(F32)
    h1, h1p = _merge(x2d, rec.reshape(n, D_RNN), att.reshape(n, D_ATTN), proj,
                     bg[:D_MODEL].reshape(1, -1), bg[D_MODEL:].reshape(1, -1),
                     w_rec_proj[l].astype(BF16), w_attn_proj[l].astype(BF16), w_o[l].astype(BF16),
                     eg, eb, row(ln1_g[l]), row(ln1_b[l]))

    eidx, wts, rank, counts = _router(h1, router_w[l].T.astype(F32),
                                      router_bias[l].reshape(-1, 1).astype(F32))
    pstart, block_exp, block_valid, n_used, n_slots = _segments(counts, TOP_K * n)
    slot_flat = _slots(eidx, rank, pstart.reshape(-1, 1)).reshape(TOP_K * n)

    xs = _sc_scatter_rows(h1p, slot_flat, n_slots)
    ys = _experts(block_exp, block_valid, n_used, xs, w_e_gate[l], w_e_up[l], w_e_down[l])
    ytok = _sc_gather_rows(ys, slot_flat).reshape(TOP_K, n, d // 2)

    wgu = jnp.concatenate([w_s_gate[l], w_s_up[l]], axis=-1).astype(BF16)
    out = _final(h1, ytok, wts.T, wgu, w_s_down[l].astype(BF16), row(ln2_g[l]), row(ln2_b[l]))
    return out.reshape(nb, s, d)
```

```python
import functools
import math

import jax
import jax.numpy as jnp
from jax import lax
from jax.experimental import pallas as pl
from jax.experimental.pallas import tpu as pltpu
from jax.experimental.pallas import tpu_sc as plsc

F32 = jnp.float32
BF16 = jnp.bfloat16
U32 = jnp.uint32

N_META = 16
D_MODEL = 1024
D_RNN = 1024
N_RNN_BLOCKS = 8
RNN_BLOCK = D_RNN // N_RNN_BLOCKS
CONV_WIDTH = 4
LRU_C = 8.0
N_HEADS = 8
HEAD_DIM = 64
V_DIM = 2 * HEAD_DIM
D_ATTN = N_HEADS * V_DIM
N_EXPERTS = 256
TOP_K = 8
N_GROUPS = 8
GROUP_SIZE = N_EXPERTS // N_GROUPS
TOPK_GROUPS = 4
D_EXPERT = 256
D_SHARED = 256
ROUTED_SCALE = 2.5
LN_EPS = 1e-5
DEPTH = 1
IN_COLS = 2 * D_RNN + 3 * D_ATTN + 2 * D_MODEL
ALPHA = (2.0 * DEPTH) ** 0.25
LAMBDA_INIT = 0.8 - 0.6 * math.exp(-0.3 * 0)

LANES = 128
SUBLANES = 8
VMEM_LIMIT = 56 * 1024 * 1024
NEG_BIG = -1e30
LOG2_E = math.log2(math.e)

COL_XR, COL_YR, COL_Q, COL_K, COL_V, COL_GREC, COL_GATT = range(7)


def _cparams(sem):
    return pltpu.CompilerParams(dimension_semantics=sem, vmem_limit_bytes=VMEM_LIMIT)


def _layer_norm_rows(x, g, b):
    mu = jnp.mean(x, axis=-1, keepdims=True)
    xc = x - mu
    var = jnp.mean(xc * xc, axis=-1, keepdims=True)
    return xc * lax.rsqrt(var + LN_EPS) * g + b


def _ln_inproj_kernel(x_ref, g_ref, b_ref, w_ref, o_ref, xn_ref):
    @pl.when(pl.program_id(1) == 0)
    def _():
        xn_ref[...] = _layer_norm_rows(x_ref[...], g_ref[...], b_ref[...]).astype(BF16)

    o_ref[...] = jnp.dot(xn_ref[...], w_ref[...], preferred_element_type=F32).astype(o_ref.dtype)


def _ln_inproj(x2d, g, b, w_bf16):
    n, d = x2d.shape
    cols = w_bf16.shape[1]
    tm = min(n, 2048)
    tn = 1024
    return pl.pallas_call(
        _ln_inproj_kernel,
        grid=(n // tm, cols // tn),
        in_specs=[
            pl.BlockSpec((tm, d), lambda i, j: (i, 0)),
            pl.BlockSpec((1, d), lambda i, j: (0, 0)),
            pl.BlockSpec((1, d), lambda i, j: (0, 0)),
            pl.BlockSpec((d, tn), lambda i, j: (0, j)),
        ],
        out_specs=pl.BlockSpec((tm, tn), lambda i, j: (i, j)),
        out_shape=jax.ShapeDtypeStruct((n, cols), BF16),
        scratch_shapes=[pltpu.VMEM((tm, d), BF16)],
        compiler_params=_cparams(("parallel", "arbitrary")),
        name="ln_inproj",
    )(x2d, g, b, w_bf16)


def _rglru_kernel(x_ref, y_ref, ctx0_ref, h0_ref, cw_ref, cb_ref, wai_ref, ba_ref, bi_ref,
                  lam_ref, o_ref, hl_ref, ctx_scr, a_scr, b_scr, hs_scr, h_scr, *, nb, tb, first):
    step = pl.program_id(0)
    nct = D_RNN // LANES

    @pl.when(step == 0)
    def _():
        for bb in range(nb):
            ctx_scr[bb] = ctx0_ref[...]
        h_scr[...] = h0_ref[...]

    lam = lam_ref[...]
    sp = jnp.maximum(-lam, 0.0) + jnp.log1p(jnp.exp(-jnp.abs(lam)))
    cw = cw_ref[...]
    cb = cb_ref[...]
    ba = ba_ref[...]
    bi = bi_ref[...]
    row = lax.broadcasted_iota(jnp.int32, (tb, 1), 0)

    def slab(bb, carry):
        xb = x_ref[bb].astype(F32)
        xcat = jnp.concatenate([ctx_scr[bb], xb], axis=0)
        u = cb + cw[CONV_WIDTH - 1:CONV_WIDTH, :] * xb
        for s in range(1, CONV_WIDTH):
            shifted = pltpu.roll(xcat, s, 0)[SUBLANES:, :]
            u = u + cw[CONV_WIDTH - 1 - s:CONV_WIDTH - s, :] * shifted
        ctx_scr[bb] = xb[tb - SUBLANES:, :]
        ub = u.astype(BF16)
        for n in range(N_RNN_BLOCKS):
            cs = slice(n * RNN_BLOCK, (n + 1) * RNN_BLOCK)
            pre = jnp.dot(ub[:, cs], wai_ref[n], preferred_element_type=F32)
            r = jax.nn.sigmoid(pre[:, :RNN_BLOCK] + ba[:, cs])
            gi = jax.nn.sigmoid(pre[:, RNN_BLOCK:] + bi[:, cs])
            log_a = -LRU_C * r * sp[:, cs]
            a = jnp.exp(log_a)
            mult = jnp.sqrt(1.0 - a * a)
            if first:
                mult = jnp.where((row == 0) & (step == 0), 1.0, mult)
            a_scr[n, pl.ds(bb, tb, stride=nb), :] = a
            b_scr[n, pl.ds(bb, tb, stride=nb), :] = mult * (gi * u[:, cs])
        return carry

    lax.fori_loop(0, nb, slab, 0)

    def tstep(t, h):
        base = pl.multiple_of(t * nb, nb)
        h = a_scr[:, pl.ds(base, nb), :] * h + b_scr[:, pl.ds(base, nb), :]
        hs_scr[:, pl.ds(base, nb), :] = h
        return h

    h_fin = lax.fori_loop(0, tb, tstep, h_scr[...], unroll=8)
    h_scr[...] = h_fin
    hl_ref[...] = h_fin

    def gate(bb, carry):
        hb = jnp.concatenate([hs_scr[n, pl.ds(bb, tb, stride=nb), :] for n in range(nct)], axis=1)
        o_ref[bb] = (hb * jax.nn.gelu(y_ref[bb].astype(F32))).astype(o_ref.dtype)
        return carry

    lax.fori_loop(0, nb, gate, 0)


def _rglru(proj3, ctx0, h0, cw, cb, wai, ba, bi, lam, *, first):
    nb, s, _ = proj3.shape
    tb = min(s, 128)
    c = D_RNN
    nct = c // LANES
    kern = functools.partial(_rglru_kernel, nb=nb, tb=tb, first=first)
    full = lambda shape: pl.BlockSpec(shape, lambda t: (0,) * len(shape))
    return pl.pallas_call(
        kern,
        grid=(s // tb,),
        in_specs=[
            pl.BlockSpec((nb, tb, c), lambda t: (0, t, COL_XR)),
            pl.BlockSpec((nb, tb, c), lambda t: (0, t, COL_YR)),
            full((SUBLANES, c)), full((nct, nb, LANES)), full((CONV_WIDTH, c)), full((1, c)),
            full((N_RNN_BLOCKS, RNN_BLOCK, 2 * RNN_BLOCK)), full((1, c)), full((1, c)), full((1, c)),
        ],
        out_specs=[pl.BlockSpec((nb, tb, c), lambda t: (0, t, 0)), full((nct, nb, LANES))],
        out_shape=[jax.ShapeDtypeStruct((nb, s, c), BF16), jax.ShapeDtypeStruct((nct, nb, LANES), F32)],
        scratch_shapes=[
            pltpu.VMEM((nb, SUBLANES, c), F32),
            pltpu.VMEM((nct, nb * tb, LANES), F32),
            pltpu.VMEM((nct, nb * tb, LANES), F32),
            pltpu.VMEM((nct, nb * tb, LANES), F32),
            pltpu.VMEM((nct, nb, LANES), F32),
        ],
        compiler_params=_cparams(("arbitrary",)),
        name="rglru_first" if first else "rglru",
    )(proj3, proj3, ctx0, h0, cw, cb, wai, ba, bi, lam)


def _attn_kernel(lq1_ref, lk1_ref, lq2_ref, lk2_ref, g_ref, q_ref, k_ref, v_ref, km_ref, vm_ref,
                 *rest, tq, n_cast):
    cast_in, o_ref, cast_out = rest[:n_cast], rest[n_cast], rest[n_cast + 1:2 * n_cast + 1]
    m_scr, acc_scr, s_scr = rest[2 * n_cast + 1:]
    for src, dst in zip(cast_in, cast_out):
        dst[...] = src[...].astype(BF16)

    qi = pl.program_id(2)
    lane = lax.broadcasted_iota(jnp.int32, (1, V_DIM), 1)
    q = (q_ref[...].astype(F32) * (HEAD_DIM ** -0.5 * LOG2_E)).astype(BF16)
    zero = jnp.zeros_like(q)
    q2 = jnp.concatenate([jnp.where(lane < HEAD_DIM, q, zero), jnp.where(lane >= HEAD_DIM, q, zero)], axis=0)

    m_scr[...] = jnp.full(m_scr.shape, NEG_BIG, F32)
    acc_scr[...] = jnp.zeros(acc_scr.shape, F32)

    def scores(kblk):
        return lax.dot_general(q2, kblk, (((1,), (1,)), ((), ())), preferred_element_type=F32)

    def consume(s, vblk, mask):
        width = s.shape[1]
        vext = jnp.concatenate([vblk, jnp.ones_like(vblk)], axis=1)
        if mask is not None:
            s = jnp.where(mask, s, NEG_BIG)
        m_prev = m_scr[...]
        m_next = jnp.maximum(m_prev, jnp.max(s, axis=1, keepdims=True))
        p = jnp.exp2(s - jnp.tile(m_next, (1, width // LANES)))
        corr = jnp.exp2(m_prev - m_next)
        acc_scr[...] = (jnp.tile(corr, (1, 2)) * acc_scr[...]
                        + jnp.dot(p.astype(BF16), vext, preferred_element_type=F32))
        m_scr[...] = m_next

    def kblock(i):
        return k_ref[pl.ds(pl.multiple_of(i * tq, tq), tq), :]

    def vblock(i):
        return v_ref[pl.ds(pl.multiple_of(i * tq, tq), tq), :]

    mcol = lax.broadcasted_iota(jnp.int32, (2 * tq, LANES), 1)
    consume(scores(km_ref[...]), vm_ref[...], mcol < N_META)

    s_scr[0] = scores(kblock(0))

    def body(j, carry):
        s_scr[1] = scores(kblock(2 * j + 1))
        consume(s_scr[0], vblock(2 * j), None)
        s_scr[0] = scores(kblock(2 * j + 2))
        consume(s_scr[1], vblock(2 * j + 1), None)
        return carry

    lax.fori_loop(0, qi // 2, body, 0)

    r_i = lax.broadcasted_iota(jnp.int32, (2 * tq, tq), 0)
    r_i = jnp.where(r_i >= tq, r_i - tq, r_i)
    c_i = lax.broadcasted_iota(jnp.int32, (2 * tq, tq), 1)
    causal = c_i <= r_i

    @pl.when(qi % 2 == 0)
    def _():
        consume(s_scr[0], vblock(qi), causal)

    @pl.when(qi % 2 == 1)
    def _():
        s_scr[1] = scores(kblock(qi))
        consume(s_scr[0], vblock(qi - 1), None)
        consume(s_scr[1], vblock(qi), causal)

    lam = (jnp.exp(jnp.sum(lq1_ref[...] * lk1_ref[...], axis=1, keepdims=True))
           - jnp.exp(jnp.sum(lq2_ref[...] * lk2_ref[...], axis=1, keepdims=True)) + LAMBDA_INIT)
    a1, a2 = acc_scr[:tq, :], acc_scr[tq:, :]
    o = a1[:, :V_DIM] / a1[:, V_DIM:] - lam * (a2[:, :V_DIM] / a2[:, V_DIM:])
    o = o * lax.rsqrt(jnp.mean(o * o, axis=-1, keepdims=True) + LN_EPS) * g_ref[...]
    o_ref[...] = (o * (1.0 - LAMBDA_INIT)).astype(o_ref.dtype)


def _diff_attn(proj3, km, vm, lq1, lk1, lq2, lk2, g, to_bf16):
    nb, s, _ = proj3.shape
    tq = min(s, 512)
    nq = s // tq
    steps = nb * N_HEADS * nq
    hb = D_ATTN // V_DIM
    small = lambda shape: pl.BlockSpec(shape, lambda b, h, i: (0,) * len(shape))
    chunked = [w.reshape(steps, w.size // w.shape[-1] // steps, w.shape[-1]) for w in to_bf16]
    chunk_specs = [pl.BlockSpec((None,) + c.shape[1:], lambda b, h, i: ((b * N_HEADS + h) * nq + i, 0, 0))
                   for c in chunked]
    outs = pl.pallas_call(
        functools.partial(_attn_kernel, tq=tq, n_cast=len(chunked)),
        grid=(nb, N_HEADS, nq),
        in_specs=[
            small((1, HEAD_DIM)), small((1, HEAD_DIM)), small((1, HEAD_DIM)), small((1, HEAD_DIM)),
            small((1, V_DIM)),
            pl.BlockSpec((None, tq, V_DIM), lambda b, h, i: (b, i, COL_Q * hb + h)),
            pl.BlockSpec((None, s, V_DIM), lambda b, h, i: (b, 0, COL_K * hb + h)),
            pl.BlockSpec((None, s, V_DIM), lambda b, h, i: (b, 0, COL_V * hb + h)),
            pl.BlockSpec((LANES, V_DIM), lambda b, h, i: (0, h)),
            pl.BlockSpec((LANES, V_DIM), lambda b, h, i: (0, h)),
        ] + chunk_specs,
        out_specs=[pl.BlockSpec((None, tq, V_DIM), lambda b, h, i: (b, i, h))] + chunk_specs,
        out_shape=[jax.ShapeDtypeStruct((nb, s, D_ATTN), BF16)]
                  + [jax.ShapeDtypeStruct(c.shape, BF16) for c in chunked],
        scratch_shapes=[
            pltpu.VMEM((2 * tq, LANES), F32),
            pltpu.VMEM((2 * tq, 2 * V_DIM), F32),
            pltpu.VMEM((2, 2 * tq, tq), F32),
        ],
        compiler_params=_cparams(("parallel", "parallel", "arbitrary")),
        name="diff_attn",
    )(lq1, lk1, lq2, lk2, g, proj3, proj3, proj3, km, vm, *chunked)
    return outs[0], [o.reshape(w.shape) for o, w in zip(outs[1:], to_bf16)]


def _merge_kernel(x_ref, rec_ref, att_ref, gr_ref, ga_ref, bgr_ref, bga_ref, wr_ref, wa_ref, wo_ref,
                  eg_ref, eb_ref, g1_ref, b1_ref, h_ref, hp_ref):
    r = jnp.dot(rec_ref[...], wr_ref[...], preferred_element_type=F32)
    a = jnp.dot(att_ref[...], wa_ref[...], preferred_element_type=F32)
    g_rec = jax.nn.sigmoid(gr_ref[...].astype(F32) + bgr_ref[...])
    g_att = jax.nn.sigmoid(ga_ref[...].astype(F32) + bga_ref[...])
    mix = jnp.dot((g_rec * r + g_att * a).astype(BF16), wo_ref[...], preferred_element_type=F32)
    h0 = _layer_norm_rows(x_ref[...], eg_ref[...], eb_ref[...])
    h1 = _layer_norm_rows(ALPHA * h0 + mix, g1_ref[...], b1_ref[...])
    h_ref[...] = h1
    hp_ref[...] = _pack_bf16_pairs(h1)


def _pack_bf16_pairs(v):
    half = v.shape[1] // 2
    lo = lax.bitcast_convert_type(v[:, :half].astype(BF16).astype(F32), U32) >> 16
    hi = lax.bitcast_convert_type(v[:, half:].astype(BF16).astype(F32), U32)
    return lo | hi


def _unpack_bf16_pairs(p):
    lo = lax.bitcast_convert_type(p << 16, F32)
    hi = lax.bitcast_convert_type(p & jnp.uint32(0xFFFF0000), F32)
    return lo, hi


def _merge(x2d, rec2d, att2d, proj2d, bgr, bga, wr, wa, wo, eg, eb, g1, b1):
    n, d = x2d.shape
    tm = min(n, 512)
    row = lambda col: pl.BlockSpec((tm, d), lambda i: (i, col))
    vec = pl.BlockSpec((1, d), lambda i: (0, 0))
    mat = pl.BlockSpec((d, d), lambda i: (0, 0))
    return pl.pallas_call(
        _merge_kernel,
        grid=(n // tm,),
        in_specs=[row(0), row(0), row(0), row(COL_GREC), row(COL_GATT), vec, vec, mat, mat, mat,
                  vec, vec, vec, vec],
        out_specs=[row(0), pl.BlockSpec((tm, d // 2), lambda i: (i, 0))],
        out_shape=[jax.ShapeDtypeStruct((n, d), F32), jax.ShapeDtypeStruct((n, d // 2), U32)],
        compiler_params=_cparams(("parallel",)),
        name="merge_ln1",
    )(x2d, rec2d, att2d, proj2d, proj2d, bgr, bga, wr, wa, wo, eg, eb, g1, b1)


def _router_kernel(h_ref, wt_ref, bias_ref, e_ref, w_ref, r_ref, c_ref, cnt_scr):
    tm = h_ref.shape[0]

    @pl.when(pl.program_id(0) == 0)
    def _():
        cnt_scr[...] = jnp.zeros(cnt_scr.shape, F32)

    logits = lax.dot_general(wt_ref[...], h_ref[...], (((1,), (1,)), ((), ())),
                             precision=lax.Precision.HIGHEST, preferred_element_type=F32)
    scores = jax.nn.sigmoid(logits)
    sel = scores + bias_ref[...]
    grp = sel.reshape(N_GROUPS, GROUP_SIZE, tm)
    gi = lax.broadcasted_iota(jnp.int32, grp.shape, 1)
    m1 = jnp.max(grp, axis=1, keepdims=True)
    first = jnp.min(jnp.where(grp == m1, gi, GROUP_SIZE), axis=1, keepdims=True)
    m2 = jnp.max(jnp.where(gi == first, -jnp.inf, grp), axis=1, keepdims=True)
    gscore = (m1 + m2).reshape(N_GROUPS, tm)

    gidx = lax.broadcasted_iota(jnp.int32, (N_GROUPS, tm), 0)
    keep = jnp.zeros((N_GROUPS, tm), jnp.bool_)
    cur = gscore
    for _ in range(TOPK_GROUPS):
        mx = jnp.max(cur, axis=0, keepdims=True)
        pick = gidx == jnp.min(jnp.where(cur == mx, gidx, N_GROUPS), axis=0, keepdims=True)
        keep = keep | pick
        cur = jnp.where(pick, -jnp.inf, cur)

    cur = jnp.where(keep[:, None, :], grp, -jnp.inf).reshape(N_EXPERTS, tm)
    eidx = lax.broadcasted_iota(jnp.int32, (N_EXPERTS, tm), 0)
    picked_e, picked_w, picks = [], [], []
    for _ in range(TOP_K):
        mx = jnp.max(cur, axis=0, keepdims=True)
        ei = jnp.min(jnp.where(cur == mx, eidx, N_EXPERTS), axis=0, keepdims=True)
        pick = eidx == ei
        picked_e.append(ei)
        picked_w.append(jnp.sum(jnp.where(pick, scores, 0.0), axis=0, keepdims=True))
        picks.append(pick)
        cur = jnp.where(pick, -jnp.inf, cur)
    w = jnp.concatenate(picked_w, axis=0)
    e_ref[...] = jnp.concatenate(picked_e, axis=0)
    w_ref[...] = w / jnp.sum(w, axis=0, keepdims=True) * ROUTED_SCALE

    member = functools.reduce(jnp.logical_or, picks)
    t_row = lax.broadcasted_iota(jnp.int32, (tm, tm), 0)
    t_col = lax.broadcasted_iota(jnp.int32, (tm, tm), 1)
    earlier = jnp.where(t_row < t_col, 1.0, 0.0).astype(BF16)
    member_f = jnp.where(member, 1.0, 0.0)
    before = cnt_scr[...] + jnp.dot(member_f.astype(BF16), earlier, preferred_element_type=F32)
    ranks = [jnp.sum(jnp.where(p, before, 0.0), axis=0, keepdims=True) for p in picks]
    r_ref[...] = jnp.concatenate(ranks, axis=0).astype(jnp.int32)
    total = cnt_scr[...] + jnp.sum(member_f, axis=1, keepdims=True)
    cnt_scr[...] = total
    c_ref[...] = total.astype(jnp.int32)


def _router(h1, router_wt, bias_col):
    n, d = h1.shape
    tm = min(n, 512)
    tok = pl.BlockSpec((TOP_K, tm), lambda i: (0, i))
    return pl.pallas_call(
        _router_kernel,
        grid=(n // tm,),
        in_specs=[
            pl.BlockSpec((tm, d), lambda i: (i, 0)),
            pl.BlockSpec((N_EXPERTS, d), lambda i: (0, 0)),
            pl.BlockSpec((N_EXPERTS, 1), lambda i: (0, 0)),
        ],
        out_specs=[tok, tok, tok, pl.BlockSpec((N_EXPERTS, 1), lambda i: (0, 0))],
        out_shape=[jax.ShapeDtypeStruct((TOP_K, n), jnp.int32), jax.ShapeDtypeStruct((TOP_K, n), F32),
                   jax.ShapeDtypeStruct((TOP_K, n), jnp.int32),
                   jax.ShapeDtypeStruct((N_EXPERTS, 1), jnp.int32)],
        scratch_shapes=[pltpu.VMEM((N_EXPERTS, 1), F32)],
        compiler_params=_cparams(("arbitrary",)),
        name="router",
    )(h1, router_wt, bias_col)


def _slots_kernel(e_ref, r_ref, ps_ref, o_ref):
    tm = e_ref.shape[1]
    eidx = lax.broadcasted_iota(jnp.int32, (N_EXPERTS, tm), 0)
    e = e_ref[...]
    ps = ps_ref[...]
    base = [jnp.sum(jnp.where(eidx == e[k:k + 1, :], ps, 0), axis=0, keepdims=True) for k in range(TOP_K)]
    o_ref[...] = jnp.concatenate(base, axis=0) + r_ref[...]


def _slots(eidx, rank, pstart_col):
    k, n = eidx.shape
    tm = min(n, 2048)
    tok = pl.BlockSpec((k, tm), lambda i: (0, i))
    return pl.pallas_call(
        _slots_kernel,
        grid=(n // tm,),
        in_specs=[tok, tok, pl.BlockSpec((N_EXPERTS, 1), lambda i: (0, 0))],
        out_specs=tok,
        out_shape=jax.ShapeDtypeStruct((k, n), jnp.int32),
        compiler_params=_cparams(("parallel",)),
        name="slots",
    )(eidx, rank, pstart_col)


SC_CORES = 2
SC_SUBCORES = 16
SC_WINDOW = 64


def _sc_mesh():
    return plsc.VectorSubcoreMesh(core_axis_name="core", subcore_axis_name="subcore")


def _sc_windows(slot_flat):
    m = slot_flat.shape[0]
    n_win = m // SC_WINDOW // (SC_CORES * SC_SUBCORES)
    assert n_win * SC_WINDOW * SC_CORES * SC_SUBCORES == m and n_win % 2 == 0
    return n_win, slot_flat.reshape(m // SC_WINDOW, SC_WINDOW)


def _sc_scratch(n_win, d, dtype):
    return [pltpu.VMEM((n_win, SC_WINDOW), jnp.int32), pltpu.VMEM((2, SC_WINDOW, d), dtype),
            pltpu.SemaphoreType.DMA, pltpu.SemaphoreType.DMA]


def _sc_scatter_rows(rows, slots, n_slots):
    n, d = rows.shape
    top_k = slots.shape[0]
    n_win = n // SC_WINDOW // (SC_CORES * SC_SUBCORES)
    assert n_win * SC_WINDOW * SC_CORES * SC_SUBCORES == n and n_win % 2 == 0
    slot_win = slots.reshape(top_k, n // SC_WINDOW, SC_WINDOW)

    @pl.kernel(out_type=jax.ShapeDtypeStruct((n_slots, d), rows.dtype), mesh=_sc_mesh(),
               scratch_types=[pltpu.VMEM((top_k, n_win, SC_WINDOW), jnp.int32),
                              pltpu.VMEM((2, SC_WINDOW, d), rows.dtype),
                              pltpu.SemaphoreType.DMA, pltpu.SemaphoreType.DMA])
    def scatter(x_hbm, i_hbm, o_hbm, idx_v, rows_v, sem0, sem1):
        wid = lax.axis_index("subcore") * SC_CORES + lax.axis_index("core")
        first = wid * n_win
        for k in range(top_k):
            pltpu.sync_copy(i_hbm.at[k, pl.ds(first, n_win)], idx_v.at[k])
        sems = (sem0, sem1)

        @pl.loop(0, n_win // 2)
        def _(pair):
            copies = []
            for b in range(2):
                w = pair * 2 + b
                pltpu.sync_copy(x_hbm.at[pl.ds((first + w) * SC_WINDOW, SC_WINDOW)], rows_v.at[b])
                for k in range(top_k):
                    cp = pltpu.make_async_copy(rows_v.at[b], o_hbm.at[idx_v.at[k, w]], sems[b])
                    cp.start()
                    copies.append(cp)
            for cp in copies:
                cp.wait()

    return scatter(rows, slot_win)


def _sc_gather_rows(table, slot_flat):
    d = table.shape[1]
    m = slot_flat.shape[0]
    n_win, slot_win = _sc_windows(slot_flat)

    @pl.kernel(out_type=jax.ShapeDtypeStruct((m, d), table.dtype), mesh=_sc_mesh(),
               scratch_types=_sc_scratch(n_win, d, table.dtype))
    def gather(t_hbm, i_hbm, o_hbm, idx_v, rows_v, sem0, sem1):
        wid = lax.axis_index("subcore") * SC_CORES + lax.axis_index("core")
        first = wid * n_win
        pltpu.sync_copy(i_hbm.at[pl.ds(first, n_win)], idx_v)
        sems = (sem0, sem1)

        def fetch(c, b):
            return pltpu.make_async_copy(t_hbm.at[idx_v.at[c]], rows_v.at[b], sems[b])

        def flush(c, b):
            pltpu.sync_copy(rows_v.at[b], o_hbm.at[pl.ds((first + c) * SC_WINDOW, SC_WINDOW)])

        fetch(0, 0).start()

        @pl.loop(0, n_win // 2)
        def _(pair):
            c = pair * 2
            fetch(c + 1, 1).start()
            fetch(c, 0).wait()
            flush(c, 0)

            @pl.when(c + 2 < n_win)
            def _():
                fetch(c + 2, 0).start()

            fetch(c + 1, 1).wait()
            flush(c + 1, 1)

    return gather(table, slot_win)


EXPERT_BLOCK = 512


def _experts_kernel(bexp_ref, bvalid_ref, nused_ref, x_ref, wg_ref, wu_ref, wd_ref, o_ref):
    i = pl.program_id(0)

    @pl.when(i < nused_ref[0])
    def _():
        lo, hi = _unpack_bf16_pairs(x_ref[...])
        x = jnp.concatenate([lo, hi], axis=1)
        rowid = lax.broadcasted_iota(jnp.int32, (EXPERT_BLOCK, 1), 0)
        x = jnp.where(rowid < bvalid_ref[i], x, 0.0).astype(BF16)
        g = jnp.dot(x, wg_ref[...], preferred_element_type=F32)
        u = jnp.dot(x, wu_ref[...], preferred_element_type=F32)
        hmid = (jax.nn.silu(g) * u).astype(BF16)
        y = jnp.dot(hmid, wd_ref[...], preferred_element_type=F32)
        o_ref[...] = _pack_bf16_pairs(y)


def _experts(block_exp, block_valid, n_used, xs, wg, wu, wd):
    n_slots, dh = xs.shape
    d = 2 * dh
    n_blocks = n_slots // EXPERT_BLOCK
    blk = lambda i, be, bv, nu: (jnp.minimum(i, nu[0] - 1), 0)
    wsel = lambda i, be, bv, nu: (be[jnp.minimum(i, nu[0] - 1)], 0, 0)
    grid_spec = pltpu.PrefetchScalarGridSpec(
        num_scalar_prefetch=3,
        grid=(n_blocks,),
        in_specs=[
            pl.BlockSpec((EXPERT_BLOCK, dh), blk),
            pl.BlockSpec((None, d, D_EXPERT), wsel),
            pl.BlockSpec((None, d, D_EXPERT), wsel),
            pl.BlockSpec((None, D_EXPERT, d), wsel),
        ],
        out_specs=pl.BlockSpec((EXPERT_BLOCK, dh), blk),
    )
    return pl.pallas_call(
        _experts_kernel,
        grid_spec=grid_spec,
        out_shape=jax.ShapeDtypeStruct((n_slots, dh), U32),
        compiler_params=_cparams(("arbitrary",)),
        name="experts",
    )(block_exp, block_valid, n_used, xs, wg, wu, wd)


def _final_kernel(h_ref, y_ref, w_ref, wgu_ref, wd_ref, g_ref, b_ref, o_ref):
    h1 = h_ref[...]
    gu = jnp.dot(h1.astype(BF16), wgu_ref[...], preferred_element_type=F32)
    hmid = (jax.nn.silu(gu[:, :D_SHARED]) * gu[:, D_SHARED:]).astype(BF16)
    shared = jnp.dot(hmid, wd_ref[...], preferred_element_type=F32)
    w = w_ref[...]
    r_lo = r_hi = None
    for k in range(TOP_K):
        lo, hi = _unpack_bf16_pairs(y_ref[k])
        wk = w[:, k:k + 1]
        r_lo = wk * lo if r_lo is None else r_lo + wk * lo
        r_hi = wk * hi if r_hi is None else r_hi + wk * hi
    ffn = jnp.concatenate([r_lo, r_hi], axis=1) + shared
    o_ref[...] = _layer_norm_rows(ALPHA * h1 + ffn, g_ref[...], b_ref[...])


def _final(h1, ytok, wtok, wgu, wd, g, b):
    n, d = h1.shape
    tm = min(n, 512)
    row = pl.BlockSpec((tm, d), lambda i: (i, 0))
    vec = pl.BlockSpec((1, d), lambda i: (0, 0))
    return pl.pallas_call(
        _final_kernel,
        grid=(n // tm,),
        in_specs=[row, pl.BlockSpec((TOP_K, tm, d // 2), lambda i: (0, i, 0)),
                  pl.BlockSpec((tm, TOP_K), lambda i: (i, 0)),
                  pl.BlockSpec((d, 2 * D_SHARED), lambda i: (0, 0)),
                  pl.BlockSpec((D_SHARED, d), lambda i: (0, 0)), vec, vec],
        out_specs=row,
        out_shape=jax.ShapeDtypeStruct((n, d), F32),
        compiler_params=_cparams(("parallel",)),
        name="shared_ln2",
    )(h1, ytok, wtok, wgu, wd, g, b)


def _segments(counts, n_assign):
    counts = counts.reshape(N_EXPERTS)
    padded = (counts + EXPERT_BLOCK - 1) // EXPERT_BLOCK * EXPERT_BLOCK
    pend = jnp.cumsum(padded)
    pstart = pend - padded
    n_blocks = n_assign // EXPERT_BLOCK + N_EXPERTS
    block_start = jnp.arange(n_blocks, dtype=jnp.int32) * EXPERT_BLOCK
    block_exp = jnp.minimum(jnp.searchsorted(pend, block_start, side='right'), N_EXPERTS - 1).astype(jnp.int32)
    block_valid = jnp.clip(counts[block_exp] - (block_start - pstart[block_exp]), 0, EXPERT_BLOCK).astype(jnp.int32)
    n_used = (pend[-1] // EXPERT_BLOCK).astype(jnp.int32).reshape(1)
    return pstart.astype(jnp.int32), block_exp, block_valid, n_used, n_blocks * EXPERT_BLOCK


def kernel(x, meta_tokens, emb_ln_g, emb_ln_b, w_in, b_gate, conv_w, conv_b, lru_w_a, lru_b_a, lru_w_i, lru_b_i, lru_lambda, lam_q1, lam_k1, lam_q2, lam_k2, subln_g, w_rec_proj, w_attn_proj, w_o, ln1_g, ln1_b, router_w, router_bias, w_e_gate, w_e_up, w_e_down, w_s_gate, w_s_up, w_s_down, ln2_g, ln2_b):
    nb, s, d = x.shape
    n = nb * s
    row = lambda v: v.reshape(1, -1).astype(F32)
    l = 0

    x2d = x.reshape(n, d)
    w_in_b = w_in[l].astype(BF16)
    eg, eb = row(emb_ln_g), row(emb_ln_b)
    proj = _ln_inproj(x2d, eg, eb, w_in_b)
    projm = _ln_inproj(meta_tokens.astype(F32), eg, eb, w_in_b)
    proj3 = proj.reshape(nb, s, IN_COLS)

    wai = jnp.concatenate([lru_w_a[l], lru_w_i[l]], axis=-1).astype(BF16)
    lru_args = (conv_w[l].astype(F32), row(conv_b[l]), wai, row(lru_b_a[l]), row(lru_b_i[l]),
                row(lru_lambda[l]))
    zeros8 = jnp.zeros((SUBLANES, D_RNN), F32)
    nct = D_RNN // LANES
    _, h_meta = _rglru(projm.reshape(1, N_META, IN_COLS), zeros8, jnp.zeros((nct, 1, LANES), F32),
                       *lru_args, first=True)
    ctx0 = projm[N_META - SUBLANES:, :D_RNN].astype(F32)
    rec, _ = _rglru(proj3, ctx0, jnp.broadcast_to(h_meta, (nct, nb, LANES)), *lru_args, first=False)

    pad = ((0, LANES - N_META), (0, 0))
    km = jnp.pad(projm[:, COL_K * D_ATTN:(COL_K + 1) * D_ATTN], pad)
    vm = jnp.pad(projm[:, COL_V * D_ATTN:(COL_V + 1) * D_ATTN], pad)
    att, (wg_b, wu_b, wd_b) = _diff_attn(
        proj3, km, vm, row(lam_q1[l]), row(lam_k1[l]), row(lam_q2[l]), row(lam_k2[l]), row(subln_g[l]),
        to_bf16=(w_e_gate[l], w_e_up[l], w_e_down[l]))

    bg = b_gate[l].astype(F32)
    h1, h1p = _merge(x2d, rec.reshape(n, D_RNN), att.reshape(n, D_ATTN), proj,
                     bg[:D_MODEL].reshape(1, -1), bg[D_MODEL:].reshape(1, -1),
                     w_rec_proj[l].astype(BF16), w_attn_proj[l].astype(BF16), w_o[l].astype(BF16),
                     eg, eb, row(ln1_g[l]), row(ln1_b[l]))

    eidx, wts, rank, counts = _router(h1, router_w[l].T.astype(F32),
                                      router_bias[l].reshape(-1, 1).astype(F32))
    pstart, block_exp, block_valid, n_used, n_slots = _segments(counts, TOP_K * n)
    slots = _slots(eidx, rank, pstart.reshape(-1, 1))

    xs = _sc_scatter_rows(h1p, slots, n_slots)
    ys = _experts(block_exp, block_valid, n_used, xs, wg_b, wu_b, wd_b)
    ytok = _sc_gather_rows(ys, slots.reshape(TOP_K * n)).reshape(TOP_K, n, d // 2)

    wgu = jnp.concatenate([w_s_gate[l], w_s_up[l]], axis=-1).astype(BF16)
    out = _final(h1, ytok, wts.T, wgu, w_s_down[l].astype(BF16), row(ln2_g[l]), row(ln2_b[l]))
    return out.reshape(nb, s, d)
```

```python
import functools
import math

import jax
import jax.numpy as jnp
from jax import lax
from jax.experimental import pallas as pl
from jax.experimental.pallas import tpu as pltpu
from jax.experimental.pallas import tpu_sc as plsc

F32 = jnp.float32
BF16 = jnp.bfloat16
U32 = jnp.uint32

N_META = 16
D_MODEL = 1024
D_RNN = 1024
N_RNN_BLOCKS = 8
RNN_BLOCK = D_RNN // N_RNN_BLOCKS
CONV_WIDTH = 4
LRU_C = 8.0
N_HEADS = 8
HEAD_DIM = 64
V_DIM = 2 * HEAD_DIM
D_ATTN = N_HEADS * V_DIM
N_EXPERTS = 256
TOP_K = 8
N_GROUPS = 8
GROUP_SIZE = N_EXPERTS // N_GROUPS
TOPK_GROUPS = 4
D_EXPERT = 256
D_SHARED = 256
ROUTED_SCALE = 2.5
LN_EPS = 1e-5
DEPTH = 1
IN_COLS = 2 * D_RNN + 3 * D_ATTN + 2 * D_MODEL
ALPHA = (2.0 * DEPTH) ** 0.25
LAMBDA_INIT = 0.8 - 0.6 * math.exp(-0.3 * 0)

LANES = 128
SUBLANES = 8
VMEM_LIMIT = 56 * 1024 * 1024
NEG_BIG = -1e30
LOG2_E = math.log2(math.e)

COL_XR, COL_YR, COL_Q, COL_K, COL_V, COL_GREC, COL_GATT = range(7)


def _cparams(sem):
    return pltpu.CompilerParams(dimension_semantics=sem, vmem_limit_bytes=VMEM_LIMIT)


def _layer_norm_rows(x, g, b):
    mu = jnp.mean(x, axis=-1, keepdims=True)
    xc = x - mu
    var = jnp.mean(xc * xc, axis=-1, keepdims=True)
    return xc * lax.rsqrt(var + LN_EPS) * g + b


def _ln_inproj_kernel(x_ref, g_ref, b_ref, w_ref, o_ref, xn_ref):
    @pl.when(pl.program_id(1) == 0)
    def _():
        xn_ref[...] = _layer_norm_rows(x_ref[...], g_ref[...], b_ref[...]).astype(BF16)

    o_ref[...] = jnp.dot(xn_ref[...], w_ref[...], preferred_element_type=F32).astype(o_ref.dtype)


def _ln_inproj(x2d, g, b, w_bf16):
    n, d = x2d.shape
    cols = w_bf16.shape[1]
    tm = min(n, 2048)
    tn = 1024
    return pl.pallas_call(
        _ln_inproj_kernel,
        grid=(n // tm, cols // tn),
        in_specs=[
            pl.BlockSpec((tm, d), lambda i, j: (i, 0)),
            pl.BlockSpec((1, d), lambda i, j: (0, 0)),
            pl.BlockSpec((1, d), lambda i, j: (0, 0)),
            pl.BlockSpec((d, tn), lambda i, j: (0, j)),
        ],
        out_specs=pl.BlockSpec((tm, tn), lambda i, j: (i, j)),
        out_shape=jax.ShapeDtypeStruct((n, cols), BF16),
        scratch_shapes=[pltpu.VMEM((tm, d), BF16)],
        compiler_params=_cparams(("parallel", "arbitrary")),
        name="ln_inproj",
    )(x2d, g, b, w_bf16)


def _rglru_kernel(x_ref, y_ref, ctx0_ref, h0_ref, cw_ref, cb_ref, wai_ref, ba_ref, bi_ref,
                  lam_ref, o_ref, hl_ref, ctx_scr, a_scr, b_scr, hs_scr, h_scr, *, nb, tb, first):
    step = pl.program_id(0)
    nct = D_RNN // LANES

    @pl.when(step == 0)
    def _():
        for bb in range(nb):
            ctx_scr[bb] = ctx0_ref[...]
        h_scr[...] = h0_ref[...]

    lam = lam_ref[...]
    sp = jnp.maximum(-lam, 0.0) + jnp.log1p(jnp.exp(-jnp.abs(lam)))
    cw = cw_ref[...]
    cb = cb_ref[...]
    ba = ba_ref[...]
    bi = bi_ref[...]
    row = lax.broadcasted_iota(jnp.int32, (tb, 1), 0)

    def slab(bb, carry):
        xb = x_ref[bb].astype(F32)
        xcat = jnp.concatenate([ctx_scr[bb], xb], axis=0)
        u = cb + cw[CONV_WIDTH - 1:CONV_WIDTH, :] * xb
        for s in range(1, CONV_WIDTH):
            shifted = pltpu.roll(xcat, s, 0)[SUBLANES:, :]
            u = u + cw[CONV_WIDTH - 1 - s:CONV_WIDTH - s, :] * shifted
        ctx_scr[bb] = xb[tb - SUBLANES:, :]
        ub = u.astype(BF16)
        for n in range(N_RNN_BLOCKS):
            cs = slice(n * RNN_BLOCK, (n + 1) * RNN_BLOCK)
            pre = jnp.dot(ub[:, cs], wai_ref[n], preferred_element_type=F32)
            r = jax.nn.sigmoid(pre[:, :RNN_BLOCK] + ba[:, cs])
            gi = jax.nn.sigmoid(pre[:, RNN_BLOCK:] + bi[:, cs])
            log_a = -LRU_C * r * sp[:, cs]
            a = jnp.exp(log_a)
            mult = jnp.sqrt(1.0 - a * a)
            if first:
                mult = jnp.where((row == 0) & (step == 0), 1.0, mult)
            a_scr[n, pl.ds(bb, tb, stride=nb), :] = a
            b_scr[n, pl.ds(bb, tb, stride=nb), :] = mult * (gi * u[:, cs])
        return carry

    lax.fori_loop(0, nb, slab, 0)

    def tstep(t, h):
        base = pl.multiple_of(t * nb, nb)
        h = a_scr[:, pl.ds(base, nb), :] * h + b_scr[:, pl.ds(base, nb), :]
        hs_scr[:, pl.ds(base, nb), :] = h
        return h

    h_fin = lax.fori_loop(0, tb, tstep, h_scr[...], unroll=8)
    h_scr[...] = h_fin
    hl_ref[...] = h_fin

    def gate(bb, carry):
        hb = jnp.concatenate([hs_scr[n, pl.ds(bb, tb, stride=nb), :] for n in range(nct)], axis=1)
        o_ref[bb] = (hb * jax.nn.gelu(y_ref[bb].astype(F32))).astype(o_ref.dtype)
        return carry

    lax.fori_loop(0, nb, gate, 0)


def _rglru(proj3, ctx0, h0, cw, cb, wai, ba, bi, lam, *, first):
    nb, s, _ = proj3.shape
    tb = min(s, 128)
    c = D_RNN
    nct = c // LANES
    kern = functools.partial(_rglru_kernel, nb=nb, tb=tb, first=first)
    full = lambda shape: pl.BlockSpec(shape, lambda t: (0,) * len(shape))
    return pl.pallas_call(
        kern,
        grid=(s // tb,),
        in_specs=[
            pl.BlockSpec((nb, tb, c), lambda t: (0, t, COL_XR)),
            pl.BlockSpec((nb, tb, c), lambda t: (0, t, COL_YR)),
            full((SUBLANES, c)), full((nct, nb, LANES)), full((CONV_WIDTH, c)), full((1, c)),
            full((N_RNN_BLOCKS, RNN_BLOCK, 2 * RNN_BLOCK)), full((1, c)), full((1, c)), full((1, c)),
        ],
        out_specs=[pl.BlockSpec((nb, tb, c), lambda t: (0, t, 0)), full((nct, nb, LANES))],
        out_shape=[jax.ShapeDtypeStruct((nb, s, c), BF16), jax.ShapeDtypeStruct((nct, nb, LANES), F32)],
        scratch_shapes=[
            pltpu.VMEM((nb, SUBLANES, c), F32),
            pltpu.VMEM((nct, nb * tb, LANES), F32),
            pltpu.VMEM((nct, nb * tb, LANES), F32),
            pltpu.VMEM((nct, nb * tb, LANES), F32),
            pltpu.VMEM((nct, nb, LANES), F32),
        ],
        compiler_params=_cparams(("arbitrary",)),
        name="rglru_first" if first else "rglru",
    )(proj3, proj3, ctx0, h0, cw, cb, wai, ba, bi, lam)


ATTN_Q_PER_STEP = 4


def _attn_schedule(nq):
    assert nq % ATTN_Q_PER_STEP == 0
    qids, slots, kblocks = [], [], []
    for j in range(nq // ATTN_Q_PER_STEP):
        ids = [2 * j, nq - 1 - 2 * j, 2 * j + 1, nq - 2 - 2 * j]
        items = [(sl, kb) for sl, q in enumerate(ids) for kb in range(q)] + list(enumerate(ids))
        qids += ids
        slots += [it[0] for it in items]
        kblocks += [it[1] for it in items]
    as_i32 = lambda v: jnp.asarray(v, jnp.int32)
    return as_i32(qids), as_i32(slots), as_i32(kblocks), 2 * (nq - 1)


def _attn_kernel(qid_ref, slot_ref, kb_ref, lq1_ref, lk1_ref, lq2_ref, lk2_ref, g_ref, q_ref, k_ref, v_ref,
                 km_ref, vm_ref, *rest, tq, n_cast, n_full):
    cast_in, o_ref, cast_out = rest[:n_cast], rest[n_cast], rest[n_cast + 1:2 * n_cast + 1]
    q2_scr, m_scr, acc_scr, s_scr = rest[2 * n_cast + 1:]
    for src, dst in zip(cast_in, cast_out):
        dst[...] = src[...].astype(BF16)

    step = pl.program_id(2)
    n_items = n_full + ATTN_Q_PER_STEP
    lane = lax.broadcasted_iota(jnp.int32, (1, V_DIM), 1)

    def qrows(sl):
        return pl.ds(pl.multiple_of(qid_ref[step * ATTN_Q_PER_STEP + sl] * tq, tq), tq)

    for sl in range(ATTN_Q_PER_STEP):
        q = (q_ref[qrows(sl), :].astype(F32) * (HEAD_DIM ** -0.5 * LOG2_E)).astype(BF16)
        zero = jnp.zeros_like(q)
        q2_scr[sl] = jnp.concatenate([jnp.where(lane < HEAD_DIM, q, zero),
                                      jnp.where(lane >= HEAD_DIM, q, zero)], axis=0)
    m_scr[...] = jnp.full(m_scr.shape, NEG_BIG, F32)
    acc_scr[...] = jnp.zeros(acc_scr.shape, F32)

    def scores(sl, kblk):
        return lax.dot_general(q2_scr[sl], kblk, (((1,), (1,)), ((), ())), preferred_element_type=F32)

    def consume(sl, s, vblk, mask):
        width = s.shape[1]
        vext = jnp.concatenate([vblk, jnp.ones_like(vblk)], axis=1)
        if mask is not None:
            s = jnp.where(mask, s, NEG_BIG)
        m_prev = m_scr[sl]
        m_next = jnp.maximum(m_prev, jnp.max(s, axis=1, keepdims=True))
        p = jnp.exp2(s - jnp.tile(m_next, (1, width // LANES)))
        corr = jnp.exp2(m_prev - m_next)
        acc_scr[sl] = (jnp.tile(corr, (1, 2)) * acc_scr[sl]
                       + jnp.dot(p.astype(BF16), vext, preferred_element_type=F32))
        m_scr[sl] = m_next

    def kv_rows(kb):
        return pl.ds(pl.multiple_of(kb * tq, tq), tq)

    mcol = lax.broadcasted_iota(jnp.int32, (2 * tq, LANES), 1)
    for sl in range(ATTN_Q_PER_STEP):
        consume(sl, scores(sl, km_ref[...]), vm_ref[...], mcol < N_META)

    r_i = lax.broadcasted_iota(jnp.int32, (2 * tq, tq), 0)
    r_i = jnp.where(r_i >= tq, r_i - tq, r_i)
    c_i = lax.broadcasted_iota(jnp.int32, (2 * tq, tq), 1)
    causal = c_i <= r_i

    def item(t):
        return slot_ref[step * n_items + t], kb_ref[step * n_items + t]

    sl0, kb0 = item(0)
    s_scr[0] = scores(sl0, k_ref[kv_rows(kb0), :])
    for t in range(n_items):
        sl, kb = item(t)
        if t + 1 < n_items:
            sl_next, kb_next = item(t + 1)
            s_scr[(t + 1) % 2] = scores(sl_next, k_ref[kv_rows(kb_next), :])
        consume(sl, s_scr[t % 2], v_ref[kv_rows(kb), :], causal if t >= n_full else None)

    lam = (jnp.exp(jnp.sum(lq1_ref[...] * lk1_ref[...], axis=1, keepdims=True))
           - jnp.exp(jnp.sum(lq2_ref[...] * lk2_ref[...], axis=1, keepdims=True)) + LAMBDA_INIT)
    for sl in range(ATTN_Q_PER_STEP):
        a1, a2 = acc_scr[sl, :tq, :], acc_scr[sl, tq:, :]
        o = a1[:, :V_DIM] / a1[:, V_DIM:] - lam * (a2[:, :V_DIM] / a2[:, V_DIM:])
        o = o * lax.rsqrt(jnp.mean(o * o, axis=-1, keepdims=True) + LN_EPS) * g_ref[...]
        o_ref[qrows(sl), :] = (o * (1.0 - LAMBDA_INIT)).astype(o_ref.dtype)


def _diff_attn(proj3, km, vm, lq1, lk1, lq2, lk2, g, to_bf16):
    nb, s, _ = proj3.shape
    tq = min(s // ATTN_Q_PER_STEP, 512)
    nq = s // tq
    qids, slots, kblocks, n_full = _attn_schedule(nq)
    n_steps = nq // ATTN_Q_PER_STEP
    steps = nb * N_HEADS * n_steps
    hb = D_ATTN // V_DIM
    small = lambda shape: pl.BlockSpec(shape, lambda b, h, j, *_: (0,) * len(shape))
    seq = lambda col: pl.BlockSpec((None, s, V_DIM), lambda b, h, j, *_: (b, 0, col * hb + h))
    meta = pl.BlockSpec((LANES, V_DIM), lambda b, h, j, *_: (0, h))
    chunked = [w.reshape(steps, w.size // w.shape[-1] // steps, w.shape[-1]) for w in to_bf16]
    chunk_specs = [pl.BlockSpec((None,) + c.shape[1:],
                                lambda b, h, j, *_: ((b * N_HEADS + h) * n_steps + j, 0, 0)) for c in chunked]
    grid_spec = pltpu.PrefetchScalarGridSpec(
        num_scalar_prefetch=3,
        grid=(nb, N_HEADS, n_steps),
        in_specs=[small((1, HEAD_DIM)), small((1, HEAD_DIM)), small((1, HEAD_DIM)), small((1, HEAD_DIM)),
                  small((1, V_DIM)), seq(COL_Q), seq(COL_K), seq(COL_V), meta, meta] + chunk_specs,
        out_specs=[pl.BlockSpec((None, s, V_DIM), lambda b, h, j, *_: (b, 0, h))] + chunk_specs,
        scratch_shapes=[
            pltpu.VMEM((ATTN_Q_PER_STEP, 2 * tq, V_DIM), BF16),
            pltpu.VMEM((ATTN_Q_PER_STEP, 2 * tq, LANES), F32),
            pltpu.VMEM((ATTN_Q_PER_STEP, 2 * tq, 2 * V_DIM), F32),
            pltpu.VMEM((2, 2 * tq, tq), F32),
        ],
    )
    outs = pl.pallas_call(
        functools.partial(_attn_kernel, tq=tq, n_cast=len(chunked), n_full=n_full),
        grid_spec=grid_spec,
        out_shape=[jax.ShapeDtypeStruct((nb, s, D_ATTN), BF16)]
                  + [jax.ShapeDtypeStruct(c.shape, BF16) for c in chunked],
        compiler_params=_cparams(("parallel", "parallel", "arbitrary")),
        name="diff_attn",
    )(qids, slots, kblocks, lq1, lk1, lq2, lk2, g, proj3, proj3, proj3, km, vm, *chunked)
    return outs[0], [o.reshape(w.shape) for o, w in zip(outs[1:], to_bf16)]


def _merge_kernel(x_ref, rec_ref, att_ref, gr_ref, ga_ref, bgr_ref, bga_ref, wr_ref, wa_ref, wo_ref,
                  eg_ref, eb_ref, g1_ref, b1_ref, h_ref, hp_ref):
    r = jnp.dot(rec_ref[...], wr_ref[...], preferred_element_type=F32)
    a = jnp.dot(att_ref[...], wa_ref[...], preferred_element_type=F32)
    g_rec = jax.nn.sigmoid(gr_ref[...].astype(F32) + bgr_ref[...])
    g_att = jax.nn.sigmoid(ga_ref[...].astype(F32) + bga_ref[...])
    mix = jnp.dot((g_rec * r + g_att * a).astype(BF16), wo_ref[...], preferred_element_type=F32)
    h0 = _layer_norm_rows(x_ref[...], eg_ref[...], eb_ref[...])
    h1 = _layer_norm_rows(ALPHA * h0 + mix, g1_ref[...], b1_ref[...])
    h_ref[...] = h1
    hp_ref[...] = _pack_bf16_pairs(h1)


def _pack_bf16_pairs(v):
    half = v.shape[1] // 2
    lo = lax.bitcast_convert_type(v[:, :half].astype(BF16).astype(F32), U32) >> 16
    hi = lax.bitcast_convert_type(v[:, half:].astype(BF16).astype(F32), U32)
    return lo | hi


def _unpack_bf16_pairs(p):
    lo = lax.bitcast_convert_type(p << 16, F32)
    hi = lax.bitcast_convert_type(p & jnp.uint32(0xFFFF0000), F32)
    return lo, hi


def _merge(x2d, rec2d, att2d, proj2d, bgr, bga, wr, wa, wo, eg, eb, g1, b1):
    n, d = x2d.shape
    tm = min(n, 512)
    row = lambda col: pl.BlockSpec((tm, d), lambda i: (i, col))
    vec = pl.BlockSpec((1, d), lambda i: (0, 0))
    mat = pl.BlockSpec((d, d), lambda i: (0, 0))
    return pl.pallas_call(
        _merge_kernel,
        grid=(n // tm,),
        in_specs=[row(0), row(0), row(0), row(COL_GREC), row(COL_GATT), vec, vec, mat, mat, mat,
                  vec, vec, vec, vec],
        out_specs=[row(0), pl.BlockSpec((tm, d // 2), lambda i: (i, 0))],
        out_shape=[jax.ShapeDtypeStruct((n, d), F32), jax.ShapeDtypeStruct((n, d // 2), U32)],
        compiler_params=_cparams(("parallel",)),
        name="merge_ln1",
    )(x2d, rec2d, att2d, proj2d, proj2d, bgr, bga, wr, wa, wo, eg, eb, g1, b1)


def _router_kernel(h_ref, wt_ref, bias_ref, e_ref, w_ref, r_ref, c_ref, cnt_scr):
    tm = h_ref.shape[0]

    @pl.when(pl.program_id(0) == 0)
    def _():
        cnt_scr[...] = jnp.zeros(cnt_scr.shape, F32)

    logits = lax.dot_general(wt_ref[...], h_ref[...], (((1,), (1,)), ((), ())),
                             precision=lax.Precision.HIGHEST, preferred_element_type=F32)
    scores = jax.nn.sigmoid(logits)
    sel = scores + bias_ref[...]
    grp = sel.reshape(N_GROUPS, GROUP_SIZE, tm)
    gi = lax.broadcasted_iota(jnp.int32, grp.shape, 1)
    m1 = jnp.max(grp, axis=1, keepdims=True)
    first = jnp.min(jnp.where(grp == m1, gi, GROUP_SIZE), axis=1, keepdims=True)
    m2 = jnp.max(jnp.where(gi == first, -jnp.inf, grp), axis=1, keepdims=True)
    gscore = (m1 + m2).reshape(N_GROUPS, tm)

    gidx = lax.broadcasted_iota(jnp.int32, (N_GROUPS, tm), 0)
    keep = jnp.zeros((N_GROUPS, tm), jnp.bool_)
    cur = gscore
    for _ in range(TOPK_GROUPS):
        mx = jnp.max(cur, axis=0, keepdims=True)
        pick = gidx == jnp.min(jnp.where(cur == mx, gidx, N_GROUPS), axis=0, keepdims=True)
        keep = keep | pick
        cur = jnp.where(pick, -jnp.inf, cur)

    cur = jnp.where(keep[:, None, :], grp, -jnp.inf).reshape(N_EXPERTS, tm)
    eidx = lax.broadcasted_iota(jnp.int32, (N_EXPERTS, tm), 0)
    picked_e, picked_w, picks = [], [], []
    for _ in range(TOP_K):
        mx = jnp.max(cur, axis=0, keepdims=True)
        ei = jnp.min(jnp.where(cur == mx, eidx, N_EXPERTS), axis=0, keepdims=True)
        pick = eidx == ei
        picked_e.append(ei)
        picked_w.append(jnp.sum(jnp.where(pick, scores, 0.0), axis=0, keepdims=True))
        picks.append(pick)
        cur = jnp.where(pick, -jnp.inf, cur)
    w = jnp.concatenate(picked_w, axis=0)
    e_ref[...] = jnp.concatenate(picked_e, axis=0)
    w_ref[...] = w / jnp.sum(w, axis=0, keepdims=True) * ROUTED_SCALE

    member = functools.reduce(jnp.logical_or, picks)
    t_row = lax.broadcasted_iota(jnp.int32, (tm, tm), 0)
    t_col = lax.broadcasted_iota(jnp.int32, (tm, tm), 1)
    earlier = jnp.where(t_row < t_col, 1.0, 0.0).astype(BF16)
    member_f = jnp.where(member, 1.0, 0.0)
    before = cnt_scr[...] + jnp.dot(member_f.astype(BF16), earlier, preferred_element_type=F32)
    ranks = [jnp.sum(jnp.where(p, before, 0.0), axis=0, keepdims=True) for p in picks]
    r_ref[...] = jnp.concatenate(ranks, axis=0).astype(jnp.int32)
    total = cnt_scr[...] + jnp.sum(member_f, axis=1, keepdims=True)
    cnt_scr[...] = total
    c_ref[...] = total.astype(jnp.int32)


def _router(h1, router_wt, bias_col):
    n, d = h1.shape
    tm = min(n, 512)
    tok = pl.BlockSpec((TOP_K, tm), lambda i: (0, i))
    return pl.pallas_call(
        _router_kernel,
        grid=(n // tm,),
        in_specs=[
            pl.BlockSpec((tm, d), lambda i: (i, 0)),
            pl.BlockSpec((N_EXPERTS, d), lambda i: (0, 0)),
            pl.BlockSpec((N_EXPERTS, 1), lambda i: (0, 0)),
        ],
        out_specs=[tok, tok, tok, pl.BlockSpec((N_EXPERTS, 1), lambda i: (0, 0))],
        out_shape=[jax.ShapeDtypeStruct((TOP_K, n), jnp.int32), jax.ShapeDtypeStruct((TOP_K, n), F32),
                   jax.ShapeDtypeStruct((TOP_K, n), jnp.int32),
                   jax.ShapeDtypeStruct((N_EXPERTS, 1), jnp.int32)],
        scratch_shapes=[pltpu.VMEM((N_EXPERTS, 1), F32)],
        compiler_params=_cparams(("arbitrary",)),
        name="router",
    )(h1, router_wt, bias_col)


def _slots_kernel(e_ref, r_ref, ps_ref, o_ref):
    tm = e_ref.shape[1]
    eidx = lax.broadcasted_iota(jnp.int32, (N_EXPERTS, tm), 0)
    e = e_ref[...]
    ps = ps_ref[...]
    base = [jnp.sum(jnp.where(eidx == e[k:k + 1, :], ps, 0), axis=0, keepdims=True) for k in range(TOP_K)]
    o_ref[...] = jnp.concatenate(base, axis=0) + r_ref[...]


def _slots(eidx, rank, pstart_col):
    k, n = eidx.shape
    tm = min(n, 2048)
    tok = pl.BlockSpec((k, tm), lambda i: (0, i))
    return pl.pallas_call(
        _slots_kernel,
        grid=(n // tm,),
        in_specs=[tok, tok, pl.BlockSpec((N_EXPERTS, 1), lambda i: (0, 0))],
        out_specs=tok,
        out_shape=jax.ShapeDtypeStruct((k, n), jnp.int32),
        compiler_params=_cparams(("parallel",)),
        name="slots",
    )(eidx, rank, pstart_col)


SC_CORES = 2
SC_SUBCORES = 16
SC_WINDOW = 64


def _sc_mesh():
    return plsc.VectorSubcoreMesh(core_axis_name="core", subcore_axis_name="subcore")


def _sc_windows(slot_flat):
    m = slot_flat.shape[0]
    n_win = m // SC_WINDOW // (SC_CORES * SC_SUBCORES)
    assert n_win * SC_WINDOW * SC_CORES * SC_SUBCORES == m and n_win % 2 == 0
    return n_win, slot_flat.reshape(m // SC_WINDOW, SC_WINDOW)


def _sc_scratch(n_win, d, dtype):
    return [pltpu.VMEM((n_win, SC_WINDOW), jnp.int32), pltpu.VMEM((2, SC_WINDOW, d), dtype),
            pltpu.SemaphoreType.DMA, pltpu.SemaphoreType.DMA]


def _sc_scatter_rows(rows, slots, n_slots):
    n, d = rows.shape
    top_k = slots.shape[0]
    n_win = n // SC_WINDOW // (SC_CORES * SC_SUBCORES)
    assert n_win * SC_WINDOW * SC_CORES * SC_SUBCORES == n and n_win % 2 == 0
    slot_win = slots.reshape(top_k, n // SC_WINDOW, SC_WINDOW)

    @pl.kernel(out_type=jax.ShapeDtypeStruct((n_slots, d), rows.dtype), mesh=_sc_mesh(),
               scratch_types=[pltpu.VMEM((top_k, n_win, SC_WINDOW), jnp.int32),
                              pltpu.VMEM((2, SC_WINDOW, d), rows.dtype),
                              pltpu.SemaphoreType.DMA, pltpu.SemaphoreType.DMA])
    def scatter(x_hbm, i_hbm, o_hbm, idx_v, rows_v, sem0, sem1):
        wid = lax.axis_index("subcore") * SC_CORES + lax.axis_index("core")
        first = wid * n_win
        for k in range(top_k):
            pltpu.sync_copy(i_hbm.at[k, pl.ds(first, n_win)], idx_v.at[k])
        sems = (sem0, sem1)

        @pl.loop(0, n_win // 2)
        def _(pair):
            copies = []
            for b in range(2):
                w = pair * 2 + b
                pltpu.sync_copy(x_hbm.at[pl.ds((first + w) * SC_WINDOW, SC_WINDOW)], rows_v.at[b])
                for k in range(top_k):
                    cp = pltpu.make_async_copy(rows_v.at[b], o_hbm.at[idx_v.at[k, w]], sems[b])
                    cp.start()
                    copies.append(cp)
            for cp in copies:
                cp.wait()

    return scatter(rows, slot_win)


def _sc_gather_rows(table, slot_flat):
    d = table.shape[1]
    m = slot_flat.shape[0]
    n_win, slot_win = _sc_windows(slot_flat)

    @pl.kernel(out_type=jax.ShapeDtypeStruct((m, d), table.dtype), mesh=_sc_mesh(),
               scratch_types=_sc_scratch(n_win, d, table.dtype))
    def gather(t_hbm, i_hbm, o_hbm, idx_v, rows_v, sem0, sem1):
        wid = lax.axis_index("subcore") * SC_CORES + lax.axis_index("core")
        first = wid * n_win
        pltpu.sync_copy(i_hbm.at[pl.ds(first, n_win)], idx_v)
        sems = (sem0, sem1)

        def fetch(c, b):
            return pltpu.make_async_copy(t_hbm.at[idx_v.at[c]], rows_v.at[b], sems[b])

        def flush(c, b):
            pltpu.sync_copy(rows_v.at[b], o_hbm.at[pl.ds((first + c) * SC_WINDOW, SC_WINDOW)])

        fetch(0, 0).start()

        @pl.loop(0, n_win // 2)
        def _(pair):
            c = pair * 2
            fetch(c + 1, 1).start()
            fetch(c, 0).wait()
            flush(c, 0)

            @pl.when(c + 2 < n_win)
            def _():
                fetch(c + 2, 0).start()

            fetch(c + 1, 1).wait()
            flush(c + 1, 1)

    return gather(table, slot_win)


EXPERT_BLOCK = 512


def _experts_kernel(bexp_ref, bvalid_ref, nused_ref, x_ref, wg_ref, wu_ref, wd_ref, o_ref):
    i = pl.program_id(0)

    @pl.when(i < nused_ref[0])
    def _():
        lo, hi = _unpack_bf16_pairs(x_ref[...])
        x = jnp.concatenate([lo, hi], axis=1)
        rowid = lax.broadcasted_iota(jnp.int32, (EXPERT_BLOCK, 1), 0)
        x = jnp.where(rowid < bvalid_ref[i], x, 0.0).astype(BF16)
        g = jnp.dot(x, wg_ref[...], preferred_element_type=F32)
        u = jnp.dot(x, wu_ref[...], preferred_element_type=F32)
        hmid = (jax.nn.silu(g) * u).astype(BF16)
        y = jnp.dot(hmid, wd_ref[...], preferred_element_type=F32)
        o_ref[...] = _pack_bf16_pairs(y)


def _experts(block_exp, block_valid, n_used, xs, wg, wu, wd):
    n_slots, dh = xs.shape
    d = 2 * dh
    n_blocks = n_slots // EXPERT_BLOCK
    blk = lambda i, be, bv, nu: (jnp.minimum(i, nu[0] - 1), 0)
    wsel = lambda i, be, bv, nu: (be[jnp.minimum(i, nu[0] - 1)], 0, 0)
    grid_spec = pltpu.PrefetchScalarGridSpec(
        num_scalar_prefetch=3,
        grid=(n_blocks,),
        in_specs=[
            pl.BlockSpec((EXPERT_BLOCK, dh), blk),
            pl.BlockSpec((None, d, D_EXPERT), wsel),
            pl.BlockSpec((None, d, D_EXPERT), wsel),
            pl.BlockSpec((None, D_EXPERT, d), wsel),
        ],
        out_specs=pl.BlockSpec((EXPERT_BLOCK, dh), blk),
    )
    return pl.pallas_call(
        _experts_kernel,
        grid_spec=grid_spec,
        out_shape=jax.ShapeDtypeStruct((n_slots, dh), U32),
        compiler_params=_cparams(("arbitrary",)),
        name="experts",
    )(block_exp, block_valid, n_used, xs, wg, wu, wd)


def _final_kernel(h_ref, y_ref, w_ref, wgu_ref, wd_ref, g_ref, b_ref, o_ref):
    h1 = h_ref[...]
    gu = jnp.dot(h1.astype(BF16), wgu_ref[...], preferred_element_type=F32)
    hmid = (jax.nn.silu(gu[:, :D_SHARED]) * gu[:, D_SHARED:]).astype(BF16)
    shared = jnp.dot(hmid, wd_ref[...], preferred_element_type=F32)
    w = w_ref[...]
    r_lo = r_hi = None
    for k in range(TOP_K):
        lo, hi = _unpack_bf16_pairs(y_ref[k])
        wk = w[:, k:k + 1]
        r_lo = wk * lo if r_lo is None else r_lo + wk * lo
        r_hi = wk * hi if r_hi is None else r_hi + wk * hi
    ffn = jnp.concatenate([r_lo, r_hi], axis=1) + shared
    o_ref[...] = _layer_norm_rows(ALPHA * h1 + ffn, g_ref[...], b_ref[...])


def _final(h1, ytok, wtok, wgu, wd, g, b):
    n, d = h1.shape
    tm = min(n, 512)
    row = pl.BlockSpec((tm, d), lambda i: (i, 0))
    vec = pl.BlockSpec((1, d), lambda i: (0, 0))
    return pl.pallas_call(
        _final_kernel,
        grid=(n // tm,),
        in_specs=[row, pl.BlockSpec((TOP_K, tm, d // 2), lambda i: (0, i, 0)),
                  pl.BlockSpec((tm, TOP_K), lambda i: (i, 0)),
                  pl.BlockSpec((d, 2 * D_SHARED), lambda i: (0, 0)),
                  pl.BlockSpec((D_SHARED, d), lambda i: (0, 0)), vec, vec],
        out_specs=row,
        out_shape=jax.ShapeDtypeStruct((n, d), F32),
        compiler_params=_cparams(("parallel",)),
        name="shared_ln2",
    )(h1, ytok, wtok, wgu, wd, g, b)


def _segments(counts, n_assign):
    counts = counts.reshape(N_EXPERTS)
    padded = (counts + EXPERT_BLOCK - 1) // EXPERT_BLOCK * EXPERT_BLOCK
    pend = jnp.cumsum(padded)
    pstart = pend - padded
    n_blocks = n_assign // EXPERT_BLOCK + N_EXPERTS
    block_start = jnp.arange(n_blocks, dtype=jnp.int32) * EXPERT_BLOCK
    block_exp = jnp.minimum(jnp.searchsorted(pend, block_start, side='right'), N_EXPERTS - 1).astype(jnp.int32)
    block_valid = jnp.clip(counts[block_exp] - (block_start - pstart[block_exp]), 0, EXPERT_BLOCK).astype(jnp.int32)
    n_used = (pend[-1] // EXPERT_BLOCK).astype(jnp.int32).reshape(1)
    return pstart.astype(jnp.int32), block_exp, block_valid, n_used, n_blocks * EXPERT_BLOCK


def kernel(x, meta_tokens, emb_ln_g, emb_ln_b, w_in, b_gate, conv_w, conv_b, lru_w_a, lru_b_a, lru_w_i, lru_b_i, lru_lambda, lam_q1, lam_k1, lam_q2, lam_k2, subln_g, w_rec_proj, w_attn_proj, w_o, ln1_g, ln1_b, router_w, router_bias, w_e_gate, w_e_up, w_e_down, w_s_gate, w_s_up, w_s_down, ln2_g, ln2_b):
    nb, s, d = x.shape
    n = nb * s
    row = lambda v: v.reshape(1, -1).astype(F32)
    l = 0

    x2d = x.reshape(n, d)
    w_in_b = w_in[l].astype(BF16)
    eg, eb = row(emb_ln_g), row(emb_ln_b)
    proj = _ln_inproj(x2d, eg, eb, w_in_b)
    projm = _ln_inproj(meta_tokens.astype(F32), eg, eb, w_in_b)
    proj3 = proj.reshape(nb, s, IN_COLS)

    wai = jnp.concatenate([lru_w_a[l], lru_w_i[l]], axis=-1).astype(BF16)
    lru_args = (conv_w[l].astype(F32), row(conv_b[l]), wai, row(lru_b_a[l]), row(lru_b_i[l]),
                row(lru_lambda[l]))
    zeros8 = jnp.zeros((SUBLANES, D_RNN), F32)
    nct = D_RNN // LANES
    _, h_meta = _rglru(projm.reshape(1, N_META, IN_COLS), zeros8, jnp.zeros((nct, 1, LANES), F32),
                       *lru_args, first=True)
    ctx0 = projm[N_META - SUBLANES:, :D_RNN].astype(F32)
    rec, _ = _rglru(proj3, ctx0, jnp.broadcast_to(h_meta, (nct, nb, LANES)), *lru_args, first=False)

    pad = ((0, LANES - N_META), (0, 0))
    km = jnp.pad(projm[:, COL_K * D_ATTN:(COL_K + 1) * D_ATTN], pad)
    vm = jnp.pad(projm[:, COL_V * D_ATTN:(COL_V + 1) * D_ATTN], pad)
    att, (wg_b, wu_b, wd_b) = _diff_attn(
        proj3, km, vm, row(lam_q1[l]), row(lam_k1[l]), row(lam_q2[l]), row(lam_k2[l]), row(subln_g[l]),
        to_bf16=(w_e_gate[l], w_e_up[l], w_e_down[l]))

    bg = b_gate[l].astype(F32)
    h1, h1p = _merge(x2d, rec.reshape(n, D_RNN), att.reshape(n, D_ATTN), proj,
                     bg[:D_MODEL].reshape(1, -1), bg[D_MODEL:].reshape(1, -1),
                     w_rec_proj[l].astype(BF16), w_attn_proj[l].astype(BF16), w_o[l].astype(BF16),
                     eg, eb, row(ln1_g[l]), row(ln1_b[l]))

    eidx, wts, rank, counts = _router(h1, router_w[l].T.astype(F32),
                                      router_bias[l].reshape(-1, 1).astype(F32))
    pstart, block_exp, block_valid, n_used, n_slots = _segments(counts, TOP_K * n)
    slots = _slots(eidx, rank, pstart.reshape(-1, 1))

    xs = _sc_scatter_rows(h1p, slots, n_slots)
    ys = _experts(block_exp, block_valid, n_used, xs, wg_b, wu_b, wd_b)
    ytok = _sc_gather_rows(ys, slots.reshape(TOP_K * n)).reshape(TOP_K, n, d // 2)

    wgu = jnp.concatenate([w_s_gate[l], w_s_up[l]], axis=-1).astype(BF16)
    out = _final(h1, ytok, wts.T, wgu, w_s_down[l].astype(BF16), row(ln2_g[l]), row(ln2_b[l]))
    return out.reshape(nb, s, d)
```

```python
import functools
import math

import jax
import jax.numpy as jnp
from jax import lax
from jax.experimental import pallas as pl
from jax.experimental.pallas import tpu as pltpu
from jax.experimental.pallas import tpu_sc as plsc

F32 = jnp.float32
BF16 = jnp.bfloat16
U32 = jnp.uint32

N_META = 16
D_MODEL = 1024
D_RNN = 1024
N_RNN_BLOCKS = 8
RNN_BLOCK = D_RNN // N_RNN_BLOCKS
CONV_WIDTH = 4
LRU_C = 8.0
N_HEADS = 8
HEAD_DIM = 64
V_DIM = 2 * HEAD_DIM
D_ATTN = N_HEADS * V_DIM
N_EXPERTS = 256
TOP_K = 8
N_GROUPS = 8
GROUP_SIZE = N_EXPERTS // N_GROUPS
TOPK_GROUPS = 4
D_EXPERT = 256
D_SHARED = 256
ROUTED_SCALE = 2.5
LN_EPS = 1e-5
DEPTH = 1
IN_COLS = 2 * D_RNN + 3 * D_ATTN + 2 * D_MODEL
ALPHA = (2.0 * DEPTH) ** 0.25
LAMBDA_INIT = 0.8 - 0.6 * math.exp(-0.3 * 0)

LANES = 128
SUBLANES = 8
VMEM_LIMIT = 56 * 1024 * 1024
NEG_BIG = -1e30
LOG2_E = math.log2(math.e)

COL_XR, COL_YR, COL_Q, COL_K, COL_V, COL_GREC, COL_GATT = range(7)


def _cparams(sem):
    return pltpu.CompilerParams(dimension_semantics=sem, vmem_limit_bytes=VMEM_LIMIT)


def _layer_norm_rows(x, g, b):
    mu = jnp.mean(x, axis=-1, keepdims=True)
    xc = x - mu
    var = jnp.mean(xc * xc, axis=-1, keepdims=True)
    return xc * lax.rsqrt(var + LN_EPS) * g + b


def _ln_inproj_kernel(x_ref, g_ref, b_ref, w_ref, o_ref, xn_ref):
    @pl.when(pl.program_id(1) == 0)
    def _():
        xn_ref[...] = _layer_norm_rows(x_ref[...], g_ref[...], b_ref[...]).astype(BF16)

    o_ref[...] = jnp.dot(xn_ref[...], w_ref[...], preferred_element_type=F32).astype(o_ref.dtype)


def _ln_inproj(x2d, g, b, w_bf16):
    n, d = x2d.shape
    cols = w_bf16.shape[1]
    tm = min(n, 2048)
    tn = 1024
    return pl.pallas_call(
        _ln_inproj_kernel,
        grid=(n // tm, cols // tn),
        in_specs=[
            pl.BlockSpec((tm, d), lambda i, j: (i, 0)),
            pl.BlockSpec((1, d), lambda i, j: (0, 0)),
            pl.BlockSpec((1, d), lambda i, j: (0, 0)),
            pl.BlockSpec((d, tn), lambda i, j: (0, j)),
        ],
        out_specs=pl.BlockSpec((tm, tn), lambda i, j: (i, j)),
        out_shape=jax.ShapeDtypeStruct((n, cols), BF16),
        scratch_shapes=[pltpu.VMEM((tm, d), BF16)],
        compiler_params=_cparams(("parallel", "arbitrary")),
        name="ln_inproj",
    )(x2d, g, b, w_bf16)


def _sigmoid_tanh(v):
    return 0.5 * jnp.tanh(0.5 * v) + 0.5


def _rglru_kernel(x_ref, y_ref, ctx0_ref, h0_ref, cw_ref, cb_ref, wai_ref, ba_ref, bi_ref,
                  lam_ref, o_ref, hl_ref, ctx_scr, a_scr, b_scr, hs_scr, h_scr, *, nb, tb, first):
    step = pl.program_id(0)
    nct = D_RNN // LANES

    @pl.when(step == 0)
    def _():
        for bb in range(nb):
            ctx_scr[bb] = ctx0_ref[...]
        h_scr[...] = h0_ref[...]

    lam = lam_ref[...]
    sp = jnp.maximum(-lam, 0.0) + jnp.log1p(jnp.exp(-jnp.abs(lam)))
    cw = cw_ref[...]
    cb = cb_ref[...]
    ba = ba_ref[...]
    bi = bi_ref[...]
    row = lax.broadcasted_iota(jnp.int32, (tb, 1), 0)

    def slab(bb, carry):
        xb = x_ref[bb].astype(F32)
        xcat = jnp.concatenate([ctx_scr[bb], xb], axis=0)
        u = cb + cw[CONV_WIDTH - 1:CONV_WIDTH, :] * xb
        for s in range(1, CONV_WIDTH):
            shifted = pltpu.roll(xcat, s, 0)[SUBLANES:, :]
            u = u + cw[CONV_WIDTH - 1 - s:CONV_WIDTH - s, :] * shifted
        ctx_scr[bb] = xb[tb - SUBLANES:, :]
        ub = u.astype(BF16)
        for n in range(N_RNN_BLOCKS):
            cs = slice(n * RNN_BLOCK, (n + 1) * RNN_BLOCK)
            pre = jnp.dot(ub[:, cs], wai_ref[n], preferred_element_type=F32)
            r = _sigmoid_tanh(pre[:, :RNN_BLOCK] + ba[:, cs])
            gi = _sigmoid_tanh(pre[:, RNN_BLOCK:] + bi[:, cs])
            log_a = -LRU_C * r * sp[:, cs]
            a = jnp.exp(log_a)
            mult = jnp.sqrt(1.0 - a * a)
            if first:
                mult = jnp.where((row == 0) & (step == 0), 1.0, mult)
            a_scr[n, pl.ds(bb, tb, stride=nb), :] = a
            b_scr[n, pl.ds(bb, tb, stride=nb), :] = mult * (gi * u[:, cs])
        return carry

    lax.fori_loop(0, nb, slab, 0)

    def tstep(t, h):
        base = pl.multiple_of(t * nb, nb)
        h = a_scr[:, pl.ds(base, nb), :] * h + b_scr[:, pl.ds(base, nb), :]
        hs_scr[:, pl.ds(base, nb), :] = h
        return h

    h_fin = lax.fori_loop(0, tb, tstep, h_scr[...], unroll=8)
    h_scr[...] = h_fin
    hl_ref[...] = h_fin

    def gate(bb, carry):
        hb = jnp.concatenate([hs_scr[n, pl.ds(bb, tb, stride=nb), :] for n in range(nct)], axis=1)
        o_ref[bb] = (hb * jax.nn.gelu(y_ref[bb].astype(F32))).astype(o_ref.dtype)
        return carry

    lax.fori_loop(0, nb, gate, 0)


def _rglru(proj3, ctx0, h0, cw, cb, wai, ba, bi, lam, *, first):
    nb, s, _ = proj3.shape
    tb = min(s, 128)
    c = D_RNN
    nct = c // LANES
    kern = functools.partial(_rglru_kernel, nb=nb, tb=tb, first=first)
    full = lambda shape: pl.BlockSpec(shape, lambda t: (0,) * len(shape))
    return pl.pallas_call(
        kern,
        grid=(s // tb,),
        in_specs=[
            pl.BlockSpec((nb, tb, c), lambda t: (0, t, COL_XR)),
            pl.BlockSpec((nb, tb, c), lambda t: (0, t, COL_YR)),
            full((SUBLANES, c)), full((nct, nb, LANES)), full((CONV_WIDTH, c)), full((1, c)),
            full((N_RNN_BLOCKS, RNN_BLOCK, 2 * RNN_BLOCK)), full((1, c)), full((1, c)), full((1, c)),
        ],
        out_specs=[pl.BlockSpec((nb, tb, c), lambda t: (0, t, 0)), full((nct, nb, LANES))],
        out_shape=[jax.ShapeDtypeStruct((nb, s, c), BF16), jax.ShapeDtypeStruct((nct, nb, LANES), F32)],
        scratch_shapes=[
            pltpu.VMEM((nb, SUBLANES, c), F32),
            pltpu.VMEM((nct, nb * tb, LANES), F32),
            pltpu.VMEM((nct, nb * tb, LANES), F32),
            pltpu.VMEM((nct, nb * tb, LANES), F32),
            pltpu.VMEM((nct, nb, LANES), F32),
        ],
        compiler_params=_cparams(("arbitrary",)),
        name="rglru_first" if first else "rglru",
    )(proj3, proj3, ctx0, h0, cw, cb, wai, ba, bi, lam)


ATTN_Q_PER_STEP = 4


def _attn_schedule(nq):
    assert nq % ATTN_Q_PER_STEP == 0
    qids, slots, kblocks = [], [], []
    for j in range(nq // ATTN_Q_PER_STEP):
        ids = [2 * j, nq - 1 - 2 * j, 2 * j + 1, nq - 2 - 2 * j]
        items = [(sl, kb) for sl, q in enumerate(ids) for kb in range(q)] + list(enumerate(ids))
        qids += ids
        slots += [it[0] for it in items]
        kblocks += [it[1] for it in items]
    as_i32 = lambda v: jnp.asarray(v, jnp.int32)
    return as_i32(qids), as_i32(slots), as_i32(kblocks), 2 * (nq - 1)


def _attn_kernel(qid_ref, slot_ref, kb_ref, lq1_ref, lk1_ref, lq2_ref, lk2_ref, g_ref, q_ref, k_ref, v_ref,
                 km_ref, vm_ref, *rest, tq, cast_groups, n_full):
    n_in, n_out = sum(cast_groups), len(cast_groups)
    cast_in, o_ref, cast_out = rest[:n_in], rest[n_in], rest[n_in + 1:n_in + 1 + n_out]
    q2_scr, m_scr, acc_scr, s_scr = rest[n_in + 1 + n_out:]
    first = 0
    for dst, n_src in zip(cast_out, cast_groups):
        parts = [src[...].astype(BF16) for src in cast_in[first:first + n_src]]
        dst[...] = parts[0] if n_src == 1 else jnp.concatenate(parts, axis=-1)
        first += n_src

    step = pl.program_id(2)
    n_items = n_full + ATTN_Q_PER_STEP
    lane = lax.broadcasted_iota(jnp.int32, (1, V_DIM), 1)

    def qrows(sl):
        return pl.ds(pl.multiple_of(qid_ref[step * ATTN_Q_PER_STEP + sl] * tq, tq), tq)

    for sl in range(ATTN_Q_PER_STEP):
        q = (q_ref[qrows(sl), :].astype(F32) * (HEAD_DIM ** -0.5 * LOG2_E)).astype(BF16)
        zero = jnp.zeros_like(q)
        q2_scr[sl] = jnp.concatenate([jnp.where(lane < HEAD_DIM, q, zero),
                                      jnp.where(lane >= HEAD_DIM, q, zero)], axis=0)
    m_scr[...] = jnp.full(m_scr.shape, NEG_BIG, F32)
    acc_scr[...] = jnp.zeros(acc_scr.shape, F32)

    def scores(sl, kblk):
        return lax.dot_general(q2_scr[sl], kblk, (((1,), (1,)), ((), ())), preferred_element_type=F32)

    def consume(sl, s, vblk, mask):
        width = s.shape[1]
        vext = jnp.concatenate([vblk, jnp.ones_like(vblk)], axis=1)
        if mask is not None:
            s = jnp.where(mask, s, NEG_BIG)
        m_prev = m_scr[sl]
        m_next = jnp.maximum(m_prev, jnp.max(s, axis=1, keepdims=True))
        p = jnp.exp2(s - jnp.tile(m_next, (1, width // LANES)))
        corr = jnp.exp2(m_prev - m_next)
        acc_scr[sl] = (jnp.tile(corr, (1, 2)) * acc_scr[sl]
                       + jnp.dot(p.astype(BF16), vext, preferred_element_type=F32))
        m_scr[sl] = m_next

    def kv_rows(kb):
        return pl.ds(pl.multiple_of(kb * tq, tq), tq)

    mcol = lax.broadcasted_iota(jnp.int32, (2 * tq, LANES), 1)
    for sl in range(ATTN_Q_PER_STEP):
        consume(sl, scores(sl, km_ref[...]), vm_ref[...], mcol < N_META)

    r_i = lax.broadcasted_iota(jnp.int32, (2 * tq, tq), 0)
    r_i = jnp.where(r_i >= tq, r_i - tq, r_i)
    c_i = lax.broadcasted_iota(jnp.int32, (2 * tq, tq), 1)
    causal = c_i <= r_i

    def item(t):
        return slot_ref[step * n_items + t], kb_ref[step * n_items + t]

    sl0, kb0 = item(0)
    s_scr[0] = scores(sl0, k_ref[kv_rows(kb0), :])
    for t in range(n_items):
        sl, kb = item(t)
        if t + 1 < n_items:
            sl_next, kb_next = item(t + 1)
            s_scr[(t + 1) % 2] = scores(sl_next, k_ref[kv_rows(kb_next), :])
        consume(sl, s_scr[t % 2], v_ref[kv_rows(kb), :], causal if t >= n_full else None)

    lam = (jnp.exp(jnp.sum(lq1_ref[...] * lk1_ref[...], axis=1, keepdims=True))
           - jnp.exp(jnp.sum(lq2_ref[...] * lk2_ref[...], axis=1, keepdims=True)) + LAMBDA_INIT)
    for sl in range(ATTN_Q_PER_STEP):
        a1, a2 = acc_scr[sl, :tq, :], acc_scr[sl, tq:, :]
        o = a1[:, :V_DIM] / a1[:, V_DIM:] - lam * (a2[:, :V_DIM] / a2[:, V_DIM:])
        o = o * lax.rsqrt(jnp.mean(o * o, axis=-1, keepdims=True) + LN_EPS) * g_ref[...]
        o_ref[qrows(sl), :] = (o * (1.0 - LAMBDA_INIT)).astype(o_ref.dtype)


def _diff_attn(proj3, km, vm, lq1, lk1, lq2, lk2, g, to_bf16):
    nb, s, _ = proj3.shape
    tq = min(s // ATTN_Q_PER_STEP, 512)
    nq = s // tq
    qids, slots, kblocks, n_full = _attn_schedule(nq)
    n_steps = nq // ATTN_Q_PER_STEP
    steps = nb * N_HEADS * n_steps
    hb = D_ATTN // V_DIM
    small = lambda shape: pl.BlockSpec(shape, lambda b, h, j, *_: (0,) * len(shape))
    seq = lambda col: pl.BlockSpec((None, s, V_DIM), lambda b, h, j, *_: (b, 0, col * hb + h))
    meta = pl.BlockSpec((LANES, V_DIM), lambda b, h, j, *_: (0, h))
    chunk = lambda w: w.reshape(steps, w.size // w.shape[-1] // steps, w.shape[-1])
    chunk_spec = lambda shape: pl.BlockSpec((None,) + shape[1:],
                                            lambda b, h, j, *_: ((b * N_HEADS + h) * n_steps + j, 0, 0))
    chunked = [chunk(w) for group in to_bf16 for w in group]
    joined = [chunk(group[0]).shape[:2] + (sum(w.shape[-1] for w in group),) for group in to_bf16]
    grid_spec = pltpu.PrefetchScalarGridSpec(
        num_scalar_prefetch=3,
        grid=(nb, N_HEADS, n_steps),
        in_specs=[small((1, HEAD_DIM)), small((1, HEAD_DIM)), small((1, HEAD_DIM)), small((1, HEAD_DIM)),
                  small((1, V_DIM)), seq(COL_Q), seq(COL_K), seq(COL_V), meta, meta]
                 + [chunk_spec(c.shape) for c in chunked],
        out_specs=[pl.BlockSpec((None, s, V_DIM), lambda b, h, j, *_: (b, 0, h))]
                  + [chunk_spec(shape) for shape in joined],
        scratch_shapes=[
            pltpu.VMEM((ATTN_Q_PER_STEP, 2 * tq, V_DIM), BF16),
            pltpu.VMEM((ATTN_Q_PER_STEP, 2 * tq, LANES), F32),
            pltpu.VMEM((ATTN_Q_PER_STEP, 2 * tq, 2 * V_DIM), F32),
            pltpu.VMEM((2, 2 * tq, tq), F32),
        ],
    )
    outs = pl.pallas_call(
        functools.partial(_attn_kernel, tq=tq, cast_groups=tuple(len(group) for group in to_bf16),
                          n_full=n_full),
        grid_spec=grid_spec,
        out_shape=[jax.ShapeDtypeStruct((nb, s, D_ATTN), BF16)]
                  + [jax.ShapeDtypeStruct(shape, BF16) for shape in joined],
        compiler_params=_cparams(("parallel", "parallel", "arbitrary")),
        name="diff_attn",
    )(qids, slots, kblocks, lq1, lk1, lq2, lk2, g, proj3, proj3, proj3, km, vm, *chunked)
    return outs[0], [o.reshape(group[0].shape[:-1] + (o.shape[-1],)) for o, group in zip(outs[1:], to_bf16)]


def _merge_kernel(x_ref, rec_ref, att_ref, gr_ref, ga_ref, bgr_ref, bga_ref, wr_ref, wa_ref, wo_ref,
                  eg_ref, eb_ref, g1_ref, b1_ref, h_ref, hp_ref):
    r = jnp.dot(rec_ref[...], wr_ref[...], preferred_element_type=F32)
    a = jnp.dot(att_ref[...], wa_ref[...], preferred_element_type=F32)
    g_rec = jax.nn.sigmoid(gr_ref[...].astype(F32) + bgr_ref[...])
    g_att = jax.nn.sigmoid(ga_ref[...].astype(F32) + bga_ref[...])
    mix = jnp.dot((g_rec * r + g_att * a).astype(BF16), wo_ref[...], preferred_element_type=F32)
    h0 = _layer_norm_rows(x_ref[...], eg_ref[...], eb_ref[...])
    h1 = _layer_norm_rows(ALPHA * h0 + mix, g1_ref[...], b1_ref[...])
    h_ref[...] = h1
    hp_ref[...] = _pack_bf16_pairs(h1)


def _pack_bf16_pairs(v):
    half = v.shape[1] // 2
    lo = lax.bitcast_convert_type(v[:, :half].astype(BF16).astype(F32), U32) >> 16
    hi = lax.bitcast_convert_type(v[:, half:].astype(BF16).astype(F32), U32)
    return lo | hi


def _unpack_bf16_pairs(p):
    lo = lax.bitcast_convert_type(p << 16, F32)
    hi = lax.bitcast_convert_type(p & jnp.uint32(0xFFFF0000), F32)
    return lo, hi


def _merge(x2d, rec2d, att2d, proj2d, bgr, bga, wr, wa, wo, eg, eb, g1, b1):
    n, d = x2d.shape
    tm = min(n, 512)
    row = lambda col: pl.BlockSpec((tm, d), lambda i: (i, col))
    vec = pl.BlockSpec((1, d), lambda i: (0, 0))
    mat = pl.BlockSpec((d, d), lambda i: (0, 0))
    return pl.pallas_call(
        _merge_kernel,
        grid=(n // tm,),
        in_specs=[row(0), row(0), row(0), row(COL_GREC), row(COL_GATT), vec, vec, mat, mat, mat,
                  vec, vec, vec, vec],
        out_specs=[row(0), pl.BlockSpec((tm, d // 2), lambda i: (i, 0))],
        out_shape=[jax.ShapeDtypeStruct((n, d), F32), jax.ShapeDtypeStruct((n, d // 2), U32)],
        compiler_params=_cparams(("parallel",)),
        name="merge_ln1",
    )(x2d, rec2d, att2d, proj2d, proj2d, bgr, bga, wr, wa, wo, eg, eb, g1, b1)


def _router_kernel(h_ref, wt_ref, bias_ref, e_ref, w_ref, r_ref, c_ref, cnt_scr):
    tm = h_ref.shape[0]

    @pl.when(pl.program_id(0) == 0)
    def _():
        cnt_scr[...] = jnp.zeros(cnt_scr.shape, F32)

    logits = lax.dot_general(wt_ref[...], h_ref[...], (((1,), (1,)), ((), ())),
                             precision=lax.Precision.HIGHEST, preferred_element_type=F32)
    scores = jax.nn.sigmoid(logits)
    sel = scores + bias_ref[...]
    grp = sel.reshape(N_GROUPS, GROUP_SIZE, tm)
    gi = lax.broadcasted_iota(jnp.int32, grp.shape, 1)
    m1 = jnp.max(grp, axis=1, keepdims=True)
    first = jnp.min(jnp.where(grp == m1, gi, GROUP_SIZE), axis=1, keepdims=True)
    m2 = jnp.max(jnp.where(gi == first, -jnp.inf, grp), axis=1, keepdims=True)
    gscore = (m1 + m2).reshape(N_GROUPS, tm)

    gidx = lax.broadcasted_iota(jnp.int32, (N_GROUPS, tm), 0)
    keep = jnp.zeros((N_GROUPS, tm), jnp.bool_)
    cur = gscore
    for _ in range(TOPK_GROUPS):
        mx = jnp.max(cur, axis=0, keepdims=True)
        pick = gidx == jnp.min(jnp.where(cur == mx, gidx, N_GROUPS), axis=0, keepdims=True)
        keep = keep | pick
        cur = jnp.where(pick, -jnp.inf, cur)

    cur = jnp.where(keep[:, None, :], grp, -jnp.inf).reshape(N_EXPERTS, tm)
    eidx = lax.broadcasted_iota(jnp.int32, (N_EXPERTS, tm), 0)
    picked_e, picked_w, picks = [], [], []
    for _ in range(TOP_K):
        mx = jnp.max(cur, axis=0, keepdims=True)
        ei = jnp.min(jnp.where(cur == mx, eidx, N_EXPERTS), axis=0, keepdims=True)
        pick = eidx == ei
        picked_e.append(ei)
        picked_w.append(jnp.sum(jnp.where(pick, scores, 0.0), axis=0, keepdims=True))
        picks.append(pick)
        cur = jnp.where(pick, -jnp.inf, cur)
    w = jnp.concatenate(picked_w, axis=0)
    e_ref[...] = jnp.concatenate(picked_e, axis=0)
    w_ref[...] = w / jnp.sum(w, axis=0, keepdims=True) * ROUTED_SCALE

    member = functools.reduce(jnp.logical_or, picks)
    t_row = lax.broadcasted_iota(jnp.int32, (tm, tm), 0)
    t_col = lax.broadcasted_iota(jnp.int32, (tm, tm), 1)
    earlier = jnp.where(t_row < t_col, 1.0, 0.0).astype(BF16)
    member_f = jnp.where(member, 1.0, 0.0)
    before = cnt_scr[...] + jnp.dot(member_f.astype(BF16), earlier, preferred_element_type=F32)
    ranks = [jnp.sum(jnp.where(p, before, 0.0), axis=0, keepdims=True) for p in picks]
    r_ref[...] = jnp.concatenate(ranks, axis=0).astype(jnp.int32)
    total = cnt_scr[...] + jnp.sum(member_f, axis=1, keepdims=True)
    cnt_scr[...] = total
    c_ref[...] = total.astype(jnp.int32)


def _router(h1, router_wt, bias_col):
    n, d = h1.shape
    tm = min(n, 512)
    tok = pl.BlockSpec((TOP_K, tm), lambda i: (0, i))
    return pl.pallas_call(
        _router_kernel,
        grid=(n // tm,),
        in_specs=[
            pl.BlockSpec((tm, d), lambda i: (i, 0)),
            pl.BlockSpec((N_EXPERTS, d), lambda i: (0, 0)),
            pl.BlockSpec((N_EXPERTS, 1), lambda i: (0, 0)),
        ],
        out_specs=[tok, tok, tok, pl.BlockSpec((N_EXPERTS, 1), lambda i: (0, 0))],
        out_shape=[jax.ShapeDtypeStruct((TOP_K, n), jnp.int32), jax.ShapeDtypeStruct((TOP_K, n), F32),
                   jax.ShapeDtypeStruct((TOP_K, n), jnp.int32),
                   jax.ShapeDtypeStruct((N_EXPERTS, 1), jnp.int32)],
        scratch_shapes=[pltpu.VMEM((N_EXPERTS, 1), F32)],
        compiler_params=_cparams(("arbitrary",)),
        name="router",
    )(h1, router_wt, bias_col)


def _slots_kernel(e_ref, r_ref, ps_ref, o_ref):
    tm = e_ref.shape[1]
    eidx = lax.broadcasted_iota(jnp.int32, (N_EXPERTS, tm), 0)
    e = e_ref[...]
    ps = ps_ref[...]
    base = [jnp.sum(jnp.where(eidx == e[k:k + 1, :], ps, 0), axis=0, keepdims=True) for k in range(TOP_K)]
    o_ref[...] = jnp.concatenate(base, axis=0) + r_ref[...]


def _slots(eidx, rank, pstart_col):
    k, n = eidx.shape
    tm = min(n, 2048)
    tok = pl.BlockSpec((k, tm), lambda i: (0, i))
    return pl.pallas_call(
        _slots_kernel,
        grid=(n // tm,),
        in_specs=[tok, tok, pl.BlockSpec((N_EXPERTS, 1), lambda i: (0, 0))],
        out_specs=tok,
        out_shape=jax.ShapeDtypeStruct((k, n), jnp.int32),
        compiler_params=_cparams(("parallel",)),
        name="slots",
    )(eidx, rank, pstart_col)


SC_CORES = 2
SC_SUBCORES = 16
SC_WINDOW = 64


def _sc_mesh():
    return plsc.VectorSubcoreMesh(core_axis_name="core", subcore_axis_name="subcore")


def _sc_windows(slot_flat):
    m = slot_flat.shape[0]
    n_win = m // SC_WINDOW // (SC_CORES * SC_SUBCORES)
    assert n_win * SC_WINDOW * SC_CORES * SC_SUBCORES == m and n_win % 2 == 0
    return n_win, slot_flat.reshape(m // SC_WINDOW, SC_WINDOW)


def _sc_scratch(n_win, d, dtype):
    return [pltpu.VMEM((n_win, SC_WINDOW), jnp.int32), pltpu.VMEM((2, SC_WINDOW, d), dtype),
            pltpu.SemaphoreType.DMA, pltpu.SemaphoreType.DMA]


def _sc_scatter_rows(rows, slots, n_slots):
    n, d = rows.shape
    top_k = slots.shape[0]
    n_win = n // SC_WINDOW // (SC_CORES * SC_SUBCORES)
    assert n_win * SC_WINDOW * SC_CORES * SC_SUBCORES == n and n_win % 2 == 0
    slot_win = slots.reshape(top_k, n // SC_WINDOW, SC_WINDOW)

    @pl.kernel(out_type=jax.ShapeDtypeStruct((n_slots, d), rows.dtype), mesh=_sc_mesh(),
               scratch_types=[pltpu.VMEM((top_k, n_win, SC_WINDOW), jnp.int32),
                              pltpu.VMEM((2, SC_WINDOW, d), rows.dtype),
                              pltpu.SemaphoreType.DMA, pltpu.SemaphoreType.DMA])
    def scatter(x_hbm, i_hbm, o_hbm, idx_v, rows_v, sem0, sem1):
        wid = lax.axis_index("subcore") * SC_CORES + lax.axis_index("core")
        first = wid * n_win
        for k in range(top_k):
            pltpu.sync_copy(i_hbm.at[k, pl.ds(first, n_win)], idx_v.at[k])
        sems = (sem0, sem1)

        @pl.loop(0, n_win // 2)
        def _(pair):
            copies = []
            for b in range(2):
                w = pair * 2 + b
                pltpu.sync_copy(x_hbm.at[pl.ds((first + w) * SC_WINDOW, SC_WINDOW)], rows_v.at[b])
                for k in range(top_k):
                    cp = pltpu.make_async_copy(rows_v.at[b], o_hbm.at[idx_v.at[k, w]], sems[b])
                    cp.start()
                    copies.append(cp)
            for cp in copies:
                cp.wait()

    return scatter(rows, slot_win)


def _sc_gather_rows(table, slot_flat):
    d = table.shape[1]
    m = slot_flat.shape[0]
    n_win, slot_win = _sc_windows(slot_flat)

    @pl.kernel(out_type=jax.ShapeDtypeStruct((m, d), table.dtype), mesh=_sc_mesh(),
               scratch_types=_sc_scratch(n_win, d, table.dtype))
    def gather(t_hbm, i_hbm, o_hbm, idx_v, rows_v, sem0, sem1):
        wid = lax.axis_index("subcore") * SC_CORES + lax.axis_index("core")
        first = wid * n_win
        pltpu.sync_copy(i_hbm.at[pl.ds(first, n_win)], idx_v)
        sems = (sem0, sem1)

        def fetch(c, b):
            return pltpu.make_async_copy(t_hbm.at[idx_v.at[c]], rows_v.at[b], sems[b])

        def flush(c, b):
            pltpu.sync_copy(rows_v.at[b], o_hbm.at[pl.ds((first + c) * SC_WINDOW, SC_WINDOW)])

        fetch(0, 0).start()

        @pl.loop(0, n_win // 2)
        def _(pair):
            c = pair * 2
            fetch(c + 1, 1).start()
            fetch(c, 0).wait()
            flush(c, 0)

            @pl.when(c + 2 < n_win)
            def _():
                fetch(c + 2, 0).start()

            fetch(c + 1, 1).wait()
            flush(c + 1, 1)

    return gather(table, slot_win)


EXPERT_BLOCK = 256


def _experts_kernel(start_ref, rows_ref, xs_hbm, wgu_ref, wd_ref, ys_hbm, xbuf, ybuf, sem_in, sem_out):
    e = pl.program_id(0)
    base = start_ref[e]
    rows = rows_ref[e]
    n_chunks = lax.shift_right_logical(rows + (EXPERT_BLOCK - 1), EXPERT_BLOCK.bit_length() - 1)

    def fetch(seg_base, c, b):
        src = xs_hbm.at[pl.ds(pl.multiple_of(seg_base + c * EXPERT_BLOCK, EXPERT_BLOCK), EXPERT_BLOCK)]
        return pltpu.make_async_copy(src, xbuf.at[b], sem_in.at[b])

    def flush(c, b):
        dst = ys_hbm.at[pl.ds(pl.multiple_of(base + c * EXPERT_BLOCK, EXPERT_BLOCK), EXPERT_BLOCK)]
        return pltpu.make_async_copy(ybuf.at[b], dst, sem_out.at[b])

    def compute(c, b):
        lo, hi = _unpack_bf16_pairs(xbuf[b])
        x = jnp.concatenate([lo, hi], axis=1)
        rowid = lax.broadcasted_iota(jnp.int32, (EXPERT_BLOCK, 1), 0)
        x = jnp.where(rowid < rows - c * EXPERT_BLOCK, x, 0.0).astype(BF16)
        gu = jnp.dot(x, wgu_ref[...], preferred_element_type=F32)
        hmid = (jax.nn.silu(gu[:, :D_EXPERT]) * gu[:, D_EXPERT:]).astype(BF16)
        ybuf[b] = _pack_bf16_pairs(jnp.dot(hmid, wd_ref[...], preferred_element_type=F32))

    @pl.when((e == 0) & (n_chunks > 0))
    def _():
        fetch(base, 0, 0).start()

    def chunk(c, b, older_flush_pending):
        fetch(base, c, b).wait()

        @pl.when(c + 1 < n_chunks)
        def _():
            fetch(base, c + 1, 1 - b).start()

        @pl.when(older_flush_pending)
        def _():
            flush(c, b).wait()

        compute(c, b)
        flush(c, b).start()

    def pair(p, carry):
        chunk(2 * p, 0, p > 0)
        chunk(2 * p + 1, 1, p > 0)
        return carry

    lax.fori_loop(0, n_chunks // 2, pair, 0)

    @pl.when(n_chunks % 2 == 1)
    def _():
        chunk(n_chunks - 1, 0, n_chunks > 1)

    nxt = jnp.minimum(e + 1, pl.num_programs(0) - 1)

    @pl.when((e + 1 < pl.num_programs(0)) & (rows_ref[nxt] > 0))
    def _():
        fetch(start_ref[nxt], 0, 0).start()

    @pl.when(n_chunks >= 1)
    def _():
        flush(0, 0).wait()

    @pl.when(n_chunks >= 2)
    def _():
        flush(0, 1).wait()


def _experts(seg_start, seg_rows, xs, wgu, wd):
    n_slots, dh = xs.shape
    d = 2 * dh
    grid_spec = pltpu.PrefetchScalarGridSpec(
        num_scalar_prefetch=2,
        grid=(N_EXPERTS,),
        in_specs=[
            pl.BlockSpec(memory_space=pl.ANY),
            pl.BlockSpec((None, d, 2 * D_EXPERT), lambda e, st, rw: (e, 0, 0)),
            pl.BlockSpec((None, D_EXPERT, d), lambda e, st, rw: (e, 0, 0)),
        ],
        out_specs=pl.BlockSpec(memory_space=pl.ANY),
        scratch_shapes=[
            pltpu.VMEM((2, EXPERT_BLOCK, dh), U32),
            pltpu.VMEM((2, EXPERT_BLOCK, dh), U32),
            pltpu.SemaphoreType.DMA((2,)),
            pltpu.SemaphoreType.DMA((2,)),
        ],
    )
    return pl.pallas_call(
        _experts_kernel,
        grid_spec=grid_spec,
        out_shape=jax.ShapeDtypeStruct((n_slots, dh), U32),
        compiler_params=_cparams(("arbitrary",)),
        name="experts",
    )(seg_start, seg_rows, xs, wgu, wd)


def _final_kernel(h_ref, y_ref, w_ref, wgu_ref, wd_ref, g_ref, b_ref, o_ref):
    h1 = h_ref[...]
    gu = jnp.dot(h1.astype(BF16), wgu_ref[...], preferred_element_type=F32)
    hmid = (jax.nn.silu(gu[:, :D_SHARED]) * gu[:, D_SHARED:]).astype(BF16)
    shared = jnp.dot(hmid, wd_ref[...], preferred_element_type=F32)
    w = w_ref[...]
    r_lo = r_hi = None
    for k in range(TOP_K):
        lo, hi = _unpack_bf16_pairs(y_ref[k])
        wk = w[:, k:k + 1]
        r_lo = wk * lo if r_lo is None else r_lo + wk * lo
        r_hi = wk * hi if r_hi is None else r_hi + wk * hi
    ffn = jnp.concatenate([r_lo, r_hi], axis=1) + shared
    o_ref[...] = _layer_norm_rows(ALPHA * h1 + ffn, g_ref[...], b_ref[...])


def _final(h1, ytok, wtok, wgu, wd, g, b):
    n, d = h1.shape
    tm = min(n, 512)
    row = pl.BlockSpec((tm, d), lambda i: (i, 0))
    vec = pl.BlockSpec((1, d), lambda i: (0, 0))
    return pl.pallas_call(
        _final_kernel,
        grid=(n // tm,),
        in_specs=[row, pl.BlockSpec((TOP_K, tm, d // 2), lambda i: (0, i, 0)),
                  pl.BlockSpec((tm, TOP_K), lambda i: (i, 0)),
                  pl.BlockSpec((d, 2 * D_SHARED), lambda i: (0, 0)),
                  pl.BlockSpec((D_SHARED, d), lambda i: (0, 0)), vec, vec],
        out_specs=row,
        out_shape=jax.ShapeDtypeStruct((n, d), F32),
        compiler_params=_cparams(("parallel",)),
        name="shared_ln2",
    )(h1, ytok, wtok, wgu, wd, g, b)


def _segments(counts, n_assign):
    counts = counts.reshape(N_EXPERTS)
    padded = (counts + EXPERT_BLOCK - 1) // EXPERT_BLOCK * EXPERT_BLOCK
    seg_start = (jnp.cumsum(padded) - padded).astype(jnp.int32)
    return seg_start, counts, n_assign + N_EXPERTS * EXPERT_BLOCK


def kernel(x, meta_tokens, emb_ln_g, emb_ln_b, w_in, b_gate, conv_w, conv_b, lru_w_a, lru_b_a, lru_w_i, lru_b_i, lru_lambda, lam_q1, lam_k1, lam_q2, lam_k2, subln_g, w_rec_proj, w_attn_proj, w_o, ln1_g, ln1_b, router_w, router_bias, w_e_gate, w_e_up, w_e_down, w_s_gate, w_s_up, w_s_down, ln2_g, ln2_b):
    nb, s, d = x.shape
    n = nb * s
    row = lambda v: v.reshape(1, -1).astype(F32)
    l = 0

    x2d = x.reshape(n, d)
    w_in_b = w_in[l].astype(BF16)
    eg, eb = row(emb_ln_g), row(emb_ln_b)
    proj = _ln_inproj(x2d, eg, eb, w_in_b)
    projm = _ln_inproj(meta_tokens.astype(F32), eg, eb, w_in_b)
    proj3 = proj.reshape(nb, s, IN_COLS)

    wai = jnp.concatenate([lru_w_a[l], lru_w_i[l]], axis=-1).astype(BF16)
    lru_args = (conv_w[l].astype(F32), row(conv_b[l]), wai, row(lru_b_a[l]), row(lru_b_i[l]),
                row(lru_lambda[l]))
    zeros8 = jnp.zeros((SUBLANES, D_RNN), F32)
    nct = D_RNN // LANES
    _, h_meta = _rglru(projm.reshape(1, N_META, IN_COLS), zeros8, jnp.zeros((nct, 1, LANES), F32),
                       *lru_args, first=True)
    ctx0 = projm[N_META - SUBLANES:, :D_RNN].astype(F32)
    rec, _ = _rglru(proj3, ctx0, jnp.broadcast_to(h_meta, (nct, nb, LANES)), *lru_args, first=False)

    pad = ((0, LANES - N_META), (0, 0))
    km = jnp.pad(projm[:, COL_K * D_ATTN:(COL_K + 1) * D_ATTN], pad)
    vm = jnp.pad(projm[:, COL_V * D_ATTN:(COL_V + 1) * D_ATTN], pad)
    att, (wgu_b, wd_b) = _diff_attn(
        proj3, km, vm, row(lam_q1[l]), row(lam_k1[l]), row(lam_q2[l]), row(lam_k2[l]), row(subln_g[l]),
        to_bf16=((w_e_gate[l], w_e_up[l]), (w_e_down[l],)))

    bg = b_gate[l].astype(F32)
    h1, h1p = _merge(x2d, rec.reshape(n, D_RNN), att.reshape(n, D_ATTN), proj,
                     bg[:D_MODEL].reshape(1, -1), bg[D_MODEL:].reshape(1, -1),
                     w_rec_proj[l].astype(BF16), w_attn_proj[l].astype(BF16), w_o[l].astype(BF16),
                     eg, eb, row(ln1_g[l]), row(ln1_b[l]))

    eidx, wts, rank, counts = _router(h1, router_w[l].T.astype(F32),
                                      router_bias[l].reshape(-1, 1).astype(F32))
    seg_start, seg_rows, n_slots = _segments(counts, TOP_K * n)
    slots = _slots(eidx, rank, seg_start.reshape(-1, 1))

    xs = _sc_scatter_rows(h1p, slots, n_slots)
    ys = _experts(seg_start, seg_rows, xs, wgu_b, wd_b)
    ytok = _sc_gather_rows(ys, slots.reshape(TOP_K * n)).reshape(TOP_K, n, d // 2)

    wgu = jnp.concatenate([w_s_gate[l], w_s_up[l]], axis=-1).astype(BF16)
    out = _final(h1, ytok, wts.T, wgu, w_s_down[l].astype(BF16), row(ln2_g[l]), row(ln2_b[l]))
    return out.reshape(nb, s, d)
```

```python
import functools
import math

import jax
import jax.numpy as jnp
from jax import lax
from jax.experimental import pallas as pl
from jax.experimental.pallas import tpu as pltpu
from jax.experimental.pallas import tpu_sc as plsc

F32 = jnp.float32
BF16 = jnp.bfloat16
U32 = jnp.uint32

N_META = 16
D_MODEL = 1024
D_RNN = 1024
N_RNN_BLOCKS = 8
RNN_BLOCK = D_RNN // N_RNN_BLOCKS
CONV_WIDTH = 4
LRU_C = 8.0
N_HEADS = 8
HEAD_DIM = 64
V_DIM = 2 * HEAD_DIM
D_ATTN = N_HEADS * V_DIM
N_EXPERTS = 256
TOP_K = 8
N_GROUPS = 8
GROUP_SIZE = N_EXPERTS // N_GROUPS
TOPK_GROUPS = 4
D_EXPERT = 256
D_SHARED = 256
ROUTED_SCALE = 2.5
LN_EPS = 1e-5
DEPTH = 1
IN_COLS = 2 * D_RNN + 3 * D_ATTN + 2 * D_MODEL
ALPHA = (2.0 * DEPTH) ** 0.25
LAMBDA_INIT = 0.8 - 0.6 * math.exp(-0.3 * 0)

LANES = 128
SUBLANES = 8
VMEM_LIMIT = 56 * 1024 * 1024
NEG_BIG = -1e30
LOG2_E = math.log2(math.e)

COL_XR, COL_YR, COL_Q, COL_K, COL_V, COL_GREC, COL_GATT = range(7)


def _cparams(sem):
    return pltpu.CompilerParams(dimension_semantics=sem, vmem_limit_bytes=VMEM_LIMIT)


def _layer_norm_rows(x, g, b):
    mu = jnp.mean(x, axis=-1, keepdims=True)
    xc = x - mu
    var = jnp.mean(xc * xc, axis=-1, keepdims=True)
    return xc * lax.rsqrt(var + LN_EPS) * g + b


def _ln_inproj_kernel(x_ref, g_ref, b_ref, w_ref, o_ref, xn_ref):
    @pl.when(pl.program_id(1) == 0)
    def _():
        xn_ref[...] = _layer_norm_rows(x_ref[...], g_ref[...], b_ref[...]).astype(BF16)

    o_ref[...] = jnp.dot(xn_ref[...], w_ref[...], preferred_element_type=F32).astype(o_ref.dtype)


def _ln_inproj(x2d, g, b, w_bf16):
    n, d = x2d.shape
    cols = w_bf16.shape[1]
    tm = min(n, 2048)
    tn = 1024
    return pl.pallas_call(
        _ln_inproj_kernel,
        grid=(n // tm, cols // tn),
        in_specs=[
            pl.BlockSpec((tm, d), lambda i, j: (i, 0)),
            pl.BlockSpec((1, d), lambda i, j: (0, 0)),
            pl.BlockSpec((1, d), lambda i, j: (0, 0)),
            pl.BlockSpec((d, tn), lambda i, j: (0, j)),
        ],
        out_specs=pl.BlockSpec((tm, tn), lambda i, j: (i, j)),
        out_shape=jax.ShapeDtypeStruct((n, cols), BF16),
        scratch_shapes=[pltpu.VMEM((tm, d), BF16)],
        compiler_params=_cparams(("parallel", "arbitrary")),
        name="ln_inproj",
    )(x2d, g, b, w_bf16)


def _sigmoid_tanh(v):
    return 0.5 * jnp.tanh(0.5 * v) + 0.5


def _rglru_kernel(x_ref, y_ref, ctx0_ref, h0_ref, cw_ref, cb_ref, wai_ref, ba_ref, bi_ref,
                  lam_ref, o_ref, hl_ref, ctx_scr, a_scr, b_scr, hs_scr, h_scr, *, nb, tb, first):
    step = pl.program_id(0)
    nct = D_RNN // LANES

    @pl.when(step == 0)
    def _():
        for bb in range(nb):
            ctx_scr[bb] = ctx0_ref[...]
        h_scr[...] = h0_ref[...]

    lam = lam_ref[...]
    sp = jnp.maximum(-lam, 0.0) + jnp.log1p(jnp.exp(-jnp.abs(lam)))
    cw = cw_ref[...]
    cb = cb_ref[...]
    ba = ba_ref[...]
    bi = bi_ref[...]
    row = lax.broadcasted_iota(jnp.int32, (tb, 1), 0)

    def slab(bb, carry):
        xb = x_ref[bb].astype(F32)
        xcat = jnp.concatenate([ctx_scr[bb], xb], axis=0)
        u = cb + cw[CONV_WIDTH - 1:CONV_WIDTH, :] * xb
        for s in range(1, CONV_WIDTH):
            shifted = pltpu.roll(xcat, s, 0)[SUBLANES:, :]
            u = u + cw[CONV_WIDTH - 1 - s:CONV_WIDTH - s, :] * shifted
        ctx_scr[bb] = xb[tb - SUBLANES:, :]
        ub = u.astype(BF16)
        for n in range(N_RNN_BLOCKS):
            cs = slice(n * RNN_BLOCK, (n + 1) * RNN_BLOCK)
            pre = jnp.dot(ub[:, cs], wai_ref[n], preferred_element_type=F32)
            r = _sigmoid_tanh(pre[:, :RNN_BLOCK] + ba[:, cs])
            gi = _sigmoid_tanh(pre[:, RNN_BLOCK:] + bi[:, cs])
            log_a = -LRU_C * r * sp[:, cs]
            a = jnp.exp(log_a)
            mult = jnp.sqrt(1.0 - a * a)
            if first:
                mult = jnp.where((row == 0) & (step == 0), 1.0, mult)
            a_scr[n, pl.ds(bb, tb, stride=nb), :] = a
            b_scr[n, pl.ds(bb, tb, stride=nb), :] = mult * (gi * u[:, cs])
        return carry

    lax.fori_loop(0, nb, slab, 0)

    def tstep(t, h):
        base = pl.multiple_of(t * nb, nb)
        h = a_scr[:, pl.ds(base, nb), :] * h + b_scr[:, pl.ds(base, nb), :]
        hs_scr[:, pl.ds(base, nb), :] = h
        return h

    h_fin = lax.fori_loop(0, tb, tstep, h_scr[...], unroll=8)
    h_scr[...] = h_fin
    hl_ref[...] = h_fin

    def gate(bb, carry):
        hb = jnp.concatenate([hs_scr[n, pl.ds(bb, tb, stride=nb), :] for n in range(nct)], axis=1)
        o_ref[bb] = (hb * jax.nn.gelu(y_ref[bb].astype(F32))).astype(o_ref.dtype)
        return carry

    lax.fori_loop(0, nb, gate, 0)


def _rglru(proj3, ctx0, h0, cw, cb, wai, ba, bi, lam, *, first):
    nb, s, _ = proj3.shape
    tb = min(s, 128)
    c = D_RNN
    nct = c // LANES
    kern = functools.partial(_rglru_kernel, nb=nb, tb=tb, first=first)
    full = lambda shape: pl.BlockSpec(shape, lambda t: (0,) * len(shape))
    return pl.pallas_call(
        kern,
        grid=(s // tb,),
        in_specs=[
            pl.BlockSpec((nb, tb, c), lambda t: (0, t, COL_XR)),
            pl.BlockSpec((nb, tb, c), lambda t: (0, t, COL_YR)),
            full((SUBLANES, c)), full((nct, nb, LANES)), full((CONV_WIDTH, c)), full((1, c)),
            full((N_RNN_BLOCKS, RNN_BLOCK, 2 * RNN_BLOCK)), full((1, c)), full((1, c)), full((1, c)),
        ],
        out_specs=[pl.BlockSpec((nb, tb, c), lambda t: (0, t, 0)), full((nct, nb, LANES))],
        out_shape=[jax.ShapeDtypeStruct((nb, s, c), BF16), jax.ShapeDtypeStruct((nct, nb, LANES), F32)],
        scratch_shapes=[
            pltpu.VMEM((nb, SUBLANES, c), F32),
            pltpu.VMEM((nct, nb * tb, LANES), F32),
            pltpu.VMEM((nct, nb * tb, LANES), F32),
            pltpu.VMEM((nct, nb * tb, LANES), F32),
            pltpu.VMEM((nct, nb, LANES), F32),
        ],
        compiler_params=_cparams(("arbitrary",)),
        name="rglru_first" if first else "rglru",
    )(proj3, proj3, ctx0, h0, cw, cb, wai, ba, bi, lam)


ATTN_Q_PER_STEP = 4


def _attn_schedule(nq):
    assert nq % ATTN_Q_PER_STEP == 0
    qids, slots, kblocks = [], [], []
    for j in range(nq // ATTN_Q_PER_STEP):
        ids = [2 * j, nq - 1 - 2 * j, 2 * j + 1, nq - 2 - 2 * j]
        items = [(sl, kb) for sl, q in enumerate(ids) for kb in range(q)] + list(enumerate(ids))
        qids += ids
        slots += [it[0] for it in items]
        kblocks += [it[1] for it in items]
    as_i32 = lambda v: jnp.asarray(v, jnp.int32)
    return as_i32(qids), as_i32(slots), as_i32(kblocks), 2 * (nq - 1)


def _attn_kernel(qid_ref, slot_ref, kb_ref, lq1_ref, lk1_ref, lq2_ref, lk2_ref, g_ref, q_ref, k_ref, v_ref,
                 km_ref, vm_ref, *rest, tq, cast_groups, n_full):
    n_in, n_out = sum(cast_groups), len(cast_groups)
    cast_in, o_ref, cast_out = rest[:n_in], rest[n_in], rest[n_in + 1:n_in + 1 + n_out]
    q2_scr, m_scr, acc_scr, s_scr = rest[n_in + 1 + n_out:]
    first = 0
    for dst, n_src in zip(cast_out, cast_groups):
        parts = [src[...].astype(BF16) for src in cast_in[first:first + n_src]]
        dst[...] = parts[0] if n_src == 1 else jnp.concatenate(parts, axis=-1)
        first += n_src

    step = pl.program_id(2)
    n_items = n_full + ATTN_Q_PER_STEP
    lane = lax.broadcasted_iota(jnp.int32, (1, V_DIM), 1)

    def qrows(sl):
        return pl.ds(pl.multiple_of(qid_ref[step * ATTN_Q_PER_STEP + sl] * tq, tq), tq)

    for sl in range(ATTN_Q_PER_STEP):
        q = (q_ref[qrows(sl), :].astype(F32) * (HEAD_DIM ** -0.5 * LOG2_E)).astype(BF16)
        zero = jnp.zeros_like(q)
        q2_scr[sl] = jnp.concatenate([jnp.where(lane < HEAD_DIM, q, zero),
                                      jnp.where(lane >= HEAD_DIM, q, zero)], axis=0)
    m_scr[...] = jnp.full(m_scr.shape, NEG_BIG, F32)
    acc_scr[...] = jnp.zeros(acc_scr.shape, F32)

    def scores(sl, kblk):
        return lax.dot_general(q2_scr[sl], kblk, (((1,), (1,)), ((), ())), preferred_element_type=F32)

    def consume(sl, s, vblk, mask):
        width = s.shape[1]
        vext = jnp.concatenate([vblk, jnp.ones_like(vblk)], axis=1)
        if mask is not None:
            s = jnp.where(mask, s, NEG_BIG)
        m_prev = m_scr[sl]
        m_next = jnp.maximum(m_prev, jnp.max(s, axis=1, keepdims=True))
        p = jnp.exp2(s - jnp.tile(m_next, (1, width // LANES)))
        corr = jnp.exp2(m_prev - m_next)
        acc_scr[sl] = (jnp.tile(corr, (1, 2)) * acc_scr[sl]
                       + jnp.dot(p.astype(BF16), vext, preferred_element_type=F32))
        m_scr[sl] = m_next

    def kv_rows(kb):
        return pl.ds(pl.multiple_of(kb * tq, tq), tq)

    mcol = lax.broadcasted_iota(jnp.int32, (2 * tq, LANES), 1)
    for sl in range(ATTN_Q_PER_STEP):
        consume(sl, scores(sl, km_ref[...]), vm_ref[...], mcol < N_META)

    r_i = lax.broadcasted_iota(jnp.int32, (2 * tq, tq), 0)
    r_i = jnp.where(r_i >= tq, r_i - tq, r_i)
    c_i = lax.broadcasted_iota(jnp.int32, (2 * tq, tq), 1)
    causal = c_i <= r_i

    def item(t):
        return slot_ref[step * n_items + t], kb_ref[step * n_items + t]

    sl0, kb0 = item(0)
    s_scr[0] = scores(sl0, k_ref[kv_rows(kb0), :])
    for t in range(n_items):
        sl, kb = item(t)
        if t + 1 < n_items:
            sl_next, kb_next = item(t + 1)
            s_scr[(t + 1) % 2] = scores(sl_next, k_ref[kv_rows(kb_next), :])
        consume(sl, s_scr[t % 2], v_ref[kv_rows(kb), :], causal if t >= n_full else None)

    lam = (jnp.exp(jnp.sum(lq1_ref[...] * lk1_ref[...], axis=1, keepdims=True))
           - jnp.exp(jnp.sum(lq2_ref[...] * lk2_ref[...], axis=1, keepdims=True)) + LAMBDA_INIT)
    for sl in range(ATTN_Q_PER_STEP):
        a1, a2 = acc_scr[sl, :tq, :], acc_scr[sl, tq:, :]
        o = a1[:, :V_DIM] / a1[:, V_DIM:] - lam * (a2[:, :V_DIM] / a2[:, V_DIM:])
        o = o * lax.rsqrt(jnp.mean(o * o, axis=-1, keepdims=True) + LN_EPS) * g_ref[...]
        o_ref[qrows(sl), :] = (o * (1.0 - LAMBDA_INIT)).astype(o_ref.dtype)


def _diff_attn(proj3, km, vm, lq1, lk1, lq2, lk2, g, to_bf16):
    nb, s, _ = proj3.shape
    tq = min(s // ATTN_Q_PER_STEP, 512)
    nq = s // tq
    qids, slots, kblocks, n_full = _attn_schedule(nq)
    n_steps = nq // ATTN_Q_PER_STEP
    steps = nb * N_HEADS * n_steps
    hb = D_ATTN // V_DIM
    small = lambda shape: pl.BlockSpec(shape, lambda b, h, j, *_: (0,) * len(shape))
    seq = lambda col: pl.BlockSpec((None, s, V_DIM), lambda b, h, j, *_: (b, 0, col * hb + h))
    meta = pl.BlockSpec((LANES, V_DIM), lambda b, h, j, *_: (0, h))
    chunk = lambda w: w.reshape(steps, w.size // w.shape[-1] // steps, w.shape[-1])
    chunk_spec = lambda shape: pl.BlockSpec((None,) + shape[1:],
                                            lambda b, h, j, *_: ((b * N_HEADS + h) * n_steps + j, 0, 0))
    chunked = [chunk(w) for group in to_bf16 for w in group]
    joined = [chunk(group[0]).shape[:2] + (sum(w.shape[-1] for w in group),) for group in to_bf16]
    grid_spec = pltpu.PrefetchScalarGridSpec(
        num_scalar_prefetch=3,
        grid=(nb, N_HEADS, n_steps),
        in_specs=[small((1, HEAD_DIM)), small((1, HEAD_DIM)), small((1, HEAD_DIM)), small((1, HEAD_DIM)),
                  small((1, V_DIM)), seq(COL_Q), seq(COL_K), seq(COL_V), meta, meta]
                 + [chunk_spec(c.shape) for c in chunked],
        out_specs=[pl.BlockSpec((None, s, V_DIM), lambda b, h, j, *_: (b, 0, h))]
                  + [chunk_spec(shape) for shape in joined],
        scratch_shapes=[
            pltpu.VMEM((ATTN_Q_PER_STEP, 2 * tq, V_DIM), BF16),
            pltpu.VMEM((ATTN_Q_PER_STEP, 2 * tq, LANES), F32),
            pltpu.VMEM((ATTN_Q_PER_STEP, 2 * tq, 2 * V_DIM), F32),
            pltpu.VMEM((2, 2 * tq, tq), F32),
        ],
    )
    outs = pl.pallas_call(
        functools.partial(_attn_kernel, tq=tq, cast_groups=tuple(len(group) for group in to_bf16),
                          n_full=n_full),
        grid_spec=grid_spec,
        out_shape=[jax.ShapeDtypeStruct((nb, s, D_ATTN), BF16)]
                  + [jax.ShapeDtypeStruct(shape, BF16) for shape in joined],
        compiler_params=_cparams(("parallel", "parallel", "arbitrary")),
        name="diff_attn",
    )(qids, slots, kblocks, lq1, lk1, lq2, lk2, g, proj3, proj3, proj3, km, vm, *chunked)
    return outs[0], [o.reshape(group[0].shape[:-1] + (o.shape[-1],)) for o, group in zip(outs[1:], to_bf16)]


def _merge_kernel(x_ref, rec_ref, att_ref, gr_ref, ga_ref, bgr_ref, bga_ref, wr_ref, wa_ref, wo_ref,
                  eg_ref, eb_ref, g1_ref, b1_ref, h_ref, hp_ref):
    r = jnp.dot(rec_ref[...], wr_ref[...], preferred_element_type=F32)
    a = jnp.dot(att_ref[...], wa_ref[...], preferred_element_type=F32)
    g_rec = jax.nn.sigmoid(gr_ref[...].astype(F32) + bgr_ref[...])
    g_att = jax.nn.sigmoid(ga_ref[...].astype(F32) + bga_ref[...])
    mix = jnp.dot((g_rec * r + g_att * a).astype(BF16), wo_ref[...], preferred_element_type=F32)
    h0 = _layer_norm_rows(x_ref[...], eg_ref[...], eb_ref[...])
    h1 = _layer_norm_rows(ALPHA * h0 + mix, g1_ref[...], b1_ref[...])
    h_ref[...] = h1
    hp_ref[...] = _pack_bf16_pairs(h1)


def _pack_bf16_pairs(v):
    half = v.shape[1] // 2
    lo = lax.bitcast_convert_type(v[:, :half].astype(BF16).astype(F32), U32) >> 16
    hi = lax.bitcast_convert_type(v[:, half:].astype(BF16).astype(F32), U32)
    return lo | hi


def _unpack_bf16_pairs(p):
    lo = lax.bitcast_convert_type(p << 16, F32)
    hi = lax.bitcast_convert_type(p & jnp.uint32(0xFFFF0000), F32)
    return lo, hi


def _merge(x2d, rec2d, att2d, proj2d, bgr, bga, wr, wa, wo, eg, eb, g1, b1):
    n, d = x2d.shape
    tm = min(n, 512)
    row = lambda col: pl.BlockSpec((tm, d), lambda i: (i, col))
    vec = pl.BlockSpec((1, d), lambda i: (0, 0))
    mat = pl.BlockSpec((d, d), lambda i: (0, 0))
    return pl.pallas_call(
        _merge_kernel,
        grid=(n // tm,),
        in_specs=[row(0), row(0), row(0), row(COL_GREC), row(COL_GATT), vec, vec, mat, mat, mat,
                  vec, vec, vec, vec],
        out_specs=[row(0), pl.BlockSpec((tm, d // 2), lambda i: (i, 0))],
        out_shape=[jax.ShapeDtypeStruct((n, d), F32), jax.ShapeDtypeStruct((n, d // 2), U32)],
        compiler_params=_cparams(("parallel",)),
        name="merge_ln1",
    )(x2d, rec2d, att2d, proj2d, proj2d, bgr, bga, wr, wa, wo, eg, eb, g1, b1)


def _router_kernel(h_ref, wt_ref, bias_ref, e_ref, w_ref, r_ref, c_ref, cnt_scr):
    tm = h_ref.shape[0]

    @pl.when(pl.program_id(0) == 0)
    def _():
        cnt_scr[...] = jnp.zeros(cnt_scr.shape, F32)

    logits = lax.dot_general(wt_ref[...], h_ref[...], (((1,), (1,)), ((), ())),
                             precision=lax.Precision.HIGHEST, preferred_element_type=F32)
    scores = jax.nn.sigmoid(logits)
    sel = scores + bias_ref[...]
    grp = sel.reshape(N_GROUPS, GROUP_SIZE, tm)
    gi = lax.broadcasted_iota(jnp.int32, grp.shape, 1)
    m1 = jnp.max(grp, axis=1, keepdims=True)
    first = jnp.min(jnp.where(grp == m1, gi, GROUP_SIZE), axis=1, keepdims=True)
    m2 = jnp.max(jnp.where(gi == first, -jnp.inf, grp), axis=1, keepdims=True)
    gscore = (m1 + m2).reshape(N_GROUPS, tm)

    gidx = lax.broadcasted_iota(jnp.int32, (N_GROUPS, tm), 0)
    keep = jnp.zeros((N_GROUPS, tm), jnp.bool_)
    cur = gscore
    for _ in range(TOPK_GROUPS):
        mx = jnp.max(cur, axis=0, keepdims=True)
        pick = gidx == jnp.min(jnp.where(cur == mx, gidx, N_GROUPS), axis=0, keepdims=True)
        keep = keep | pick
        cur = jnp.where(pick, -jnp.inf, cur)

    cur = jnp.where(keep[:, None, :], grp, -jnp.inf).reshape(N_EXPERTS, tm)
    eidx = lax.broadcasted_iota(jnp.int32, (N_EXPERTS, tm), 0)
    picked_e, picked_w, picks = [], [], []
    for _ in range(TOP_K):
        mx = jnp.max(cur, axis=0, keepdims=True)
        ei = jnp.min(jnp.where(cur == mx, eidx, N_EXPERTS), axis=0, keepdims=True)
        pick = eidx == ei
        picked_e.append(ei)
        picked_w.append(jnp.sum(jnp.where(pick, scores, 0.0), axis=0, keepdims=True))
        picks.append(pick)
        cur = jnp.where(pick, -jnp.inf, cur)
    w = jnp.concatenate(picked_w, axis=0)
    e_ref[...] = jnp.concatenate(picked_e, axis=0)
    w_ref[...] = w / jnp.sum(w, axis=0, keepdims=True) * ROUTED_SCALE

    member = functools.reduce(jnp.logical_or, picks)
    t_row = lax.broadcasted_iota(jnp.int32, (tm, tm), 0)
    t_col = lax.broadcasted_iota(jnp.int32, (tm, tm), 1)
    earlier = jnp.where(t_row < t_col, 1.0, 0.0).astype(BF16)
    member_f = jnp.where(member, 1.0, 0.0)
    before = cnt_scr[...] + jnp.dot(member_f.astype(BF16), earlier, preferred_element_type=F32)
    ranks = [jnp.sum(jnp.where(p, before, 0.0), axis=0, keepdims=True) for p in picks]
    r_ref[...] = jnp.concatenate(ranks, axis=0).astype(jnp.int32)
    total = cnt_scr[...] + jnp.sum(member_f, axis=1, keepdims=True)
    cnt_scr[...] = total
    c_ref[...] = total.astype(jnp.int32)


def _router(h1, router_wt, bias_col):
    n, d = h1.shape
    tm = min(n, 512)
    tok = pl.BlockSpec((TOP_K, tm), lambda i: (0, i))
    return pl.pallas_call(
        _router_kernel,
        grid=(n // tm,),
        in_specs=[
            pl.BlockSpec((tm, d), lambda i: (i, 0)),
            pl.BlockSpec((N_EXPERTS, d), lambda i: (0, 0)),
            pl.BlockSpec((N_EXPERTS, 1), lambda i: (0, 0)),
        ],
        out_specs=[tok, tok, tok, pl.BlockSpec((N_EXPERTS, 1), lambda i: (0, 0))],
        out_shape=[jax.ShapeDtypeStruct((TOP_K, n), jnp.int32), jax.ShapeDtypeStruct((TOP_K, n), F32),
                   jax.ShapeDtypeStruct((TOP_K, n), jnp.int32),
                   jax.ShapeDtypeStruct((N_EXPERTS, 1), jnp.int32)],
        scratch_shapes=[pltpu.VMEM((N_EXPERTS, 1), F32)],
        compiler_params=_cparams(("arbitrary",)),
        name="router",
    )(h1, router_wt, bias_col)


def _slots_kernel(e_ref, r_ref, ps_ref, o_ref):
    tm = e_ref.shape[1]
    eidx = lax.broadcasted_iota(jnp.int32, (N_EXPERTS, tm), 0)
    e = e_ref[...]
    ps = ps_ref[...]
    base = [jnp.sum(jnp.where(eidx == e[k:k + 1, :], ps, 0), axis=0, keepdims=True) for k in range(TOP_K)]
    o_ref[...] = jnp.concatenate(base, axis=0) + r_ref[...]


def _slots(eidx, rank, pstart_col):
    k, n = eidx.shape
    tm = min(n, 2048)
    tok = pl.BlockSpec((k, tm), lambda i: (0, i))
    return pl.pallas_call(
        _slots_kernel,
        grid=(n // tm,),
        in_specs=[tok, tok, pl.BlockSpec((N_EXPERTS, 1), lambda i: (0, 0))],
        out_specs=tok,
        out_shape=jax.ShapeDtypeStruct((k, n), jnp.int32),
        compiler_params=_cparams(("parallel",)),
        name="slots",
    )(eidx, rank, pstart_col)


SC_CORES = 2
SC_SUBCORES = 16
SC_WINDOW = 64


def _sc_mesh():
    return plsc.VectorSubcoreMesh(core_axis_name="core", subcore_axis_name="subcore")


def _sc_windows(slot_flat):
    m = slot_flat.shape[0]
    n_win = m // SC_WINDOW // (SC_CORES * SC_SUBCORES)
    assert n_win * SC_WINDOW * SC_CORES * SC_SUBCORES == m and n_win % 2 == 0
    return n_win, slot_flat.reshape(m // SC_WINDOW, SC_WINDOW)


def _sc_scratch(n_win, d, dtype):
    return [pltpu.VMEM((n_win, SC_WINDOW), jnp.int32), pltpu.VMEM((2, SC_WINDOW, d), dtype),
            pltpu.SemaphoreType.DMA, pltpu.SemaphoreType.DMA]


def _sc_scatter_rows(rows, slots, n_slots):
    n, d = rows.shape
    top_k = slots.shape[0]
    n_win = n // SC_WINDOW // (SC_CORES * SC_SUBCORES)
    assert n_win * SC_WINDOW * SC_CORES * SC_SUBCORES == n and n_win % 2 == 0
    slot_win = slots.reshape(top_k, n // SC_WINDOW, SC_WINDOW)

    @pl.kernel(out_type=jax.ShapeDtypeStruct((n_slots, d), rows.dtype), mesh=_sc_mesh(),
               scratch_types=[pltpu.VMEM((top_k, n_win, SC_WINDOW), jnp.int32),
                              pltpu.VMEM((2, SC_WINDOW, d), rows.dtype),
                              pltpu.SemaphoreType.DMA, pltpu.SemaphoreType.DMA])
    def scatter(x_hbm, i_hbm, o_hbm, idx_v, rows_v, sem0, sem1):
        wid = lax.axis_index("subcore") * SC_CORES + lax.axis_index("core")
        first = wid * n_win
        for k in range(top_k):
            pltpu.sync_copy(i_hbm.at[k, pl.ds(first, n_win)], idx_v.at[k])
        sems = (sem0, sem1)

        @pl.loop(0, n_win // 2)
        def _(pair):
            copies = []
            for b in range(2):
                w = pair * 2 + b
                pltpu.sync_copy(x_hbm.at[pl.ds((first + w) * SC_WINDOW, SC_WINDOW)], rows_v.at[b])
                for k in range(top_k):
                    cp = pltpu.make_async_copy(rows_v.at[b], o_hbm.at[idx_v.at[k, w]], sems[b])
                    cp.start()
                    copies.append(cp)
            for cp in copies:
                cp.wait()

    return scatter(rows, slot_win)


def _sc_gather_rows(table, slot_flat):
    d = table.shape[1]
    m = slot_flat.shape[0]
    n_win, slot_win = _sc_windows(slot_flat)

    @pl.kernel(out_type=jax.ShapeDtypeStruct((m, d), table.dtype), mesh=_sc_mesh(),
               scratch_types=_sc_scratch(n_win, d, table.dtype))
    def gather(t_hbm, i_hbm, o_hbm, idx_v, rows_v, sem0, sem1):
        wid = lax.axis_index("subcore") * SC_CORES + lax.axis_index("core")
        first = wid * n_win
        pltpu.sync_copy(i_hbm.at[pl.ds(first, n_win)], idx_v)
        sems = (sem0, sem1)

        def fetch(c, b):
            return pltpu.make_async_copy(t_hbm.at[idx_v.at[c]], rows_v.at[b], sems[b])

        def flush(c, b):
            pltpu.sync_copy(rows_v.at[b], o_hbm.at[pl.ds((first + c) * SC_WINDOW, SC_WINDOW)])

        fetch(0, 0).start()

        @pl.loop(0, n_win // 2)
        def _(pair):
            c = pair * 2
            fetch(c + 1, 1).start()
            fetch(c, 0).wait()
            flush(c, 0)

            @pl.when(c + 2 < n_win)
            def _():
                fetch(c + 2, 0).start()

            fetch(c + 1, 1).wait()
            flush(c + 1, 1)

    return gather(table, slot_win)


EXPERT_BLOCK = 512
BLOCKS_PER_STEP = 2


def _experts_kernel(bexp_ref, bvalid_ref, nused_ref, x_ref, *rest):
    o_ref = rest[2 * BLOCKS_PER_STEP]
    for half in range(BLOCKS_PER_STEP):
        blk = pl.program_id(0) * BLOCKS_PER_STEP + half
        wgu_ref, wd_ref = rest[2 * half], rest[2 * half + 1]
        rows = pl.ds(half * EXPERT_BLOCK, EXPERT_BLOCK)

        @pl.when(blk < nused_ref[0])
        def _():
            lo, hi = _unpack_bf16_pairs(x_ref[rows, :])
            x = jnp.concatenate([lo, hi], axis=1)
            rowid = lax.broadcasted_iota(jnp.int32, (EXPERT_BLOCK, 1), 0)
            x = jnp.where(rowid < bvalid_ref[blk], x, 0.0).astype(BF16)
            gu = jnp.dot(x, wgu_ref[...], preferred_element_type=F32)
            hmid = (jax.nn.silu(gu[:, :D_EXPERT]) * gu[:, D_EXPERT:]).astype(BF16)
            o_ref[rows, :] = _pack_bf16_pairs(jnp.dot(hmid, wd_ref[...], preferred_element_type=F32))


def _experts(block_exp, block_valid, n_used, xs, wgu, wd):
    n_slots, dh = xs.shape
    d = 2 * dh
    step_rows = BLOCKS_PER_STEP * EXPERT_BLOCK
    n_steps = n_slots // step_rows
    last_step = lambda nu: (nu[0] - 1) // BLOCKS_PER_STEP
    rows = lambda i, be, bv, nu: (jnp.minimum(i, last_step(nu)), 0)

    def weights(half):
        return lambda i, be, bv, nu: (be[jnp.minimum(i, last_step(nu)) * BLOCKS_PER_STEP + half], 0, 0)

    w_specs = []
    for half in range(BLOCKS_PER_STEP):
        w_specs += [pl.BlockSpec((None, d, 2 * D_EXPERT), weights(half)),
                    pl.BlockSpec((None, D_EXPERT, d), weights(half))]
    grid_spec = pltpu.PrefetchScalarGridSpec(
        num_scalar_prefetch=3,
        grid=(n_steps,),
        in_specs=[pl.BlockSpec((step_rows, dh), rows)] + w_specs,
        out_specs=pl.BlockSpec((step_rows, dh), rows),
    )
    return pl.pallas_call(
        _experts_kernel,
        grid_spec=grid_spec,
        out_shape=jax.ShapeDtypeStruct((n_slots, dh), U32),
        compiler_params=_cparams(("arbitrary",)),
        name="experts",
    )(block_exp, block_valid, n_used, xs, *([wgu, wd] * BLOCKS_PER_STEP))


def _final_kernel(h_ref, y_ref, w_ref, wgu_ref, wd_ref, g_ref, b_ref, o_ref):
    h1 = h_ref[...]
    gu = jnp.dot(h1.astype(BF16), wgu_ref[...], preferred_element_type=F32)
    hmid = (jax.nn.silu(gu[:, :D_SHARED]) * gu[:, D_SHARED:]).astype(BF16)
    shared = jnp.dot(hmid, wd_ref[...], preferred_element_type=F32)
    w = w_ref[...]
    r_lo = r_hi = None
    for k in range(TOP_K):
        lo, hi = _unpack_bf16_pairs(y_ref[k])
        wk = w[:, k:k + 1]
        r_lo = wk * lo if r_lo is None else r_lo + wk * lo
        r_hi = wk * hi if r_hi is None else r_hi + wk * hi
    ffn = jnp.concatenate([r_lo, r_hi], axis=1) + shared
    o_ref[...] = _layer_norm_rows(ALPHA * h1 + ffn, g_ref[...], b_ref[...])


def _final(h1, ytok, wtok, wgu, wd, g, b):
    n, d = h1.shape
    tm = min(n, 512)
    row = pl.BlockSpec((tm, d), lambda i: (i, 0))
    vec = pl.BlockSpec((1, d), lambda i: (0, 0))
    return pl.pallas_call(
        _final_kernel,
        grid=(n // tm,),
        in_specs=[row, pl.BlockSpec((TOP_K, tm, d // 2), lambda i: (0, i, 0)),
                  pl.BlockSpec((tm, TOP_K), lambda i: (i, 0)),
                  pl.BlockSpec((d, 2 * D_SHARED), lambda i: (0, 0)),
                  pl.BlockSpec((D_SHARED, d), lambda i: (0, 0)), vec, vec],
        out_specs=row,
        out_shape=jax.ShapeDtypeStruct((n, d), F32),
        compiler_params=_cparams(("parallel",)),
        name="shared_ln2",
    )(h1, ytok, wtok, wgu, wd, g, b)


def _segments(counts, n_assign):
    counts = counts.reshape(N_EXPERTS)
    padded = (counts + EXPERT_BLOCK - 1) // EXPERT_BLOCK * EXPERT_BLOCK
    seg_end = jnp.cumsum(padded)
    seg_start = seg_end - padded
    n_blocks = n_assign // EXPERT_BLOCK + N_EXPERTS
    block_start = jnp.arange(n_blocks, dtype=jnp.int32) * EXPERT_BLOCK
    block_exp = jnp.minimum(jnp.sum(seg_end[None, :] <= block_start[:, None], axis=1), N_EXPERTS - 1)
    mine = block_exp[:, None] == jnp.arange(N_EXPERTS)[None, :]
    rows_left = jnp.sum(jnp.where(mine, (counts + seg_start)[None, :], 0), axis=1) - block_start
    block_valid = jnp.clip(rows_left, 0, EXPERT_BLOCK)
    n_used = (seg_end[-1] // EXPERT_BLOCK).reshape(1)
    i32 = lambda v: v.astype(jnp.int32)
    return i32(seg_start), i32(block_exp), i32(block_valid), i32(n_used), n_blocks * EXPERT_BLOCK


def kernel(x, meta_tokens, emb_ln_g, emb_ln_b, w_in, b_gate, conv_w, conv_b, lru_w_a, lru_b_a, lru_w_i, lru_b_i, lru_lambda, lam_q1, lam_k1, lam_q2, lam_k2, subln_g, w_rec_proj, w_attn_proj, w_o, ln1_g, ln1_b, router_w, router_bias, w_e_gate, w_e_up, w_e_down, w_s_gate, w_s_up, w_s_down, ln2_g, ln2_b):
    nb, s, d = x.shape
    n = nb * s
    row = lambda v: v.reshape(1, -1).astype(F32)
    l = 0

    x2d = x.reshape(n, d)
    w_in_b = w_in[l].astype(BF16)
    eg, eb = row(emb_ln_g), row(emb_ln_b)
    proj = _ln_inproj(x2d, eg, eb, w_in_b)
    projm = _ln_inproj(meta_tokens.astype(F32), eg, eb, w_in_b)
    proj3 = proj.reshape(nb, s, IN_COLS)

    wai = jnp.concatenate([lru_w_a[l], lru_w_i[l]], axis=-1).astype(BF16)
    lru_args = (conv_w[l].astype(F32), row(conv_b[l]), wai, row(lru_b_a[l]), row(lru_b_i[l]),
                row(lru_lambda[l]))
    zeros8 = jnp.zeros((SUBLANES, D_RNN), F32)
    nct = D_RNN // LANES
    _, h_meta = _rglru(projm.reshape(1, N_META, IN_COLS), zeros8, jnp.zeros((nct, 1, LANES), F32),
                       *lru_args, first=True)
    ctx0 = projm[N_META - SUBLANES:, :D_RNN].astype(F32)
    rec, _ = _rglru(proj3, ctx0, jnp.broadcast_to(h_meta, (nct, nb, LANES)), *lru_args, first=False)

    pad = ((0, LANES - N_META), (0, 0))
    km = jnp.pad(projm[:, COL_K * D_ATTN:(COL_K + 1) * D_ATTN], pad)
    vm = jnp.pad(projm[:, COL_V * D_ATTN:(COL_V + 1) * D_ATTN], pad)
    att, (wgu_b, wd_b) = _diff_attn(
        proj3, km, vm, row(lam_q1[l]), row(lam_k1[l]), row(lam_q2[l]), row(lam_k2[l]), row(subln_g[l]),
        to_bf16=((w_e_gate[l], w_e_up[l]), (w_e_down[l],)))

    bg = b_gate[l].astype(F32)
    h1, h1p = _merge(x2d, rec.reshape(n, D_RNN), att.reshape(n, D_ATTN), proj,
                     bg[:D_MODEL].reshape(1, -1), bg[D_MODEL:].reshape(1, -1),
                     w_rec_proj[l].astype(BF16), w_attn_proj[l].astype(BF16), w_o[l].astype(BF16),
                     eg, eb, row(ln1_g[l]), row(ln1_b[l]))

    eidx, wts, rank, counts = _router(h1, router_w[l].T.astype(F32),
                                      router_bias[l].reshape(-1, 1).astype(F32))
    seg_start, block_exp, block_valid, n_used, n_slots = _segments(counts, TOP_K * n)
    slots = _slots(eidx, rank, seg_start.reshape(-1, 1))

    xs = _sc_scatter_rows(h1p, slots, n_slots)
    ys = _experts(block_exp, block_valid, n_used, xs, wgu_b, wd_b)
    ytok = _sc_gather_rows(ys, slots.reshape(TOP_K * n)).reshape(TOP_K, n, d // 2)

    wgu = jnp.concatenate([w_s_gate[l], w_s_up[l]], axis=-1).astype(BF16)
    out = _final(h1, ytok, wts.T, wgu, w_s_down[l].astype(BF16), row(ln2_g[l]), row(ln2_b[l]))
    return out.reshape(nb, s, d)
```

```python
import functools
import math

import jax
import jax.numpy as jnp
from jax import lax
from jax.experimental import pallas as pl
from jax.experimental.pallas import tpu as pltpu
from jax.experimental.pallas import tpu_sc as plsc

F32 = jnp.float32
BF16 = jnp.bfloat16
U32 = jnp.uint32

N_META = 16
D_MODEL = 1024
D_RNN = 1024
N_RNN_BLOCKS = 8
RNN_BLOCK = D_RNN // N_RNN_BLOCKS
CONV_WIDTH = 4
LRU_C = 8.0
N_HEADS = 8
HEAD_DIM = 64
V_DIM = 2 * HEAD_DIM
D_ATTN = N_HEADS * V_DIM
N_EXPERTS = 256
TOP_K = 8
N_GROUPS = 8
GROUP_SIZE = N_EXPERTS // N_GROUPS
TOPK_GROUPS = 4
D_EXPERT = 256
D_SHARED = 256
ROUTED_SCALE = 2.5
LN_EPS = 1e-5
DEPTH = 1
IN_COLS = 2 * D_RNN + 3 * D_ATTN + 2 * D_MODEL
ALPHA = (2.0 * DEPTH) ** 0.25
LAMBDA_INIT = 0.8 - 0.6 * math.exp(-0.3 * 0)

LANES = 128
SUBLANES = 8
VMEM_LIMIT = 56 * 1024 * 1024
NEG_BIG = -1e30
LOG2_E = math.log2(math.e)

COL_XR, COL_YR, COL_Q, COL_K, COL_V, COL_GREC, COL_GATT = range(7)


def _cparams(sem):
    return pltpu.CompilerParams(dimension_semantics=sem, vmem_limit_bytes=VMEM_LIMIT)


def _layer_norm_rows(x, g, b):
    mu = jnp.mean(x, axis=-1, keepdims=True)
    xc = x - mu
    var = jnp.mean(xc * xc, axis=-1, keepdims=True)
    return xc * lax.rsqrt(var + LN_EPS) * g + b


def _ln_inproj_kernel(x_ref, g_ref, b_ref, w_ref, o_ref, xn_ref):
    @pl.when(pl.program_id(1) == 0)
    def _():
        xn_ref[...] = _layer_norm_rows(x_ref[...], g_ref[...], b_ref[...]).astype(BF16)

    o_ref[...] = jnp.dot(xn_ref[...], w_ref[...], preferred_element_type=F32).astype(o_ref.dtype)


def _ln_inproj(x2d, g, b, w_bf16):
    n, d = x2d.shape
    cols = w_bf16.shape[1]
    tm = min(n, 2048)
    tn = 1024
    return pl.pallas_call(
        _ln_inproj_kernel,
        grid=(n // tm, cols // tn),
        in_specs=[
            pl.BlockSpec((tm, d), lambda i, j: (i, 0)),
            pl.BlockSpec((1, d), lambda i, j: (0, 0)),
            pl.BlockSpec((1, d), lambda i, j: (0, 0)),
            pl.BlockSpec((d, tn), lambda i, j: (0, j)),
        ],
        out_specs=pl.BlockSpec((tm, tn), lambda i, j: (i, j)),
        out_shape=jax.ShapeDtypeStruct((n, cols), BF16),
        scratch_shapes=[pltpu.VMEM((tm, d), BF16)],
        compiler_params=_cparams(("parallel", "arbitrary")),
        name="ln_inproj",
    )(x2d, g, b, w_bf16)


def _sigmoid_tanh(v):
    return 0.5 * jnp.tanh(0.5 * v) + 0.5


def _rglru_kernel(x_ref, y_ref, ctx0_ref, h0_ref, cw_ref, cb_ref, wai_ref, ba_ref, bi_ref,
                  lam_ref, o_ref, hl_ref, ctx_scr, a_scr, b_scr, hs_scr, h_scr, *, nb, tb, first):
    step = pl.program_id(0)
    nct = D_RNN // LANES

    @pl.when(step == 0)
    def _():
        for bb in range(nb):
            ctx_scr[bb] = ctx0_ref[...]
        h_scr[...] = h0_ref[...]

    lam = lam_ref[...]
    sp = jnp.maximum(-lam, 0.0) + jnp.log1p(jnp.exp(-jnp.abs(lam)))
    cw = cw_ref[...]
    cb = cb_ref[...]
    ba = ba_ref[...]
    bi = bi_ref[...]
    row = lax.broadcasted_iota(jnp.int32, (tb, 1), 0)

    def slab(bb, carry):
        xb = x_ref[bb].astype(F32)
        xcat = jnp.concatenate([ctx_scr[bb], xb], axis=0)
        u = cb + cw[CONV_WIDTH - 1:CONV_WIDTH, :] * xb
        for s in range(1, CONV_WIDTH):
            shifted = pltpu.roll(xcat, s, 0)[SUBLANES:, :]
            u = u + cw[CONV_WIDTH - 1 - s:CONV_WIDTH - s, :] * shifted
        ctx_scr[bb] = xb[tb - SUBLANES:, :]
        ub = u.astype(BF16)
        for n in range(N_RNN_BLOCKS):
            cs = slice(n * RNN_BLOCK, (n + 1) * RNN_BLOCK)
            pre = jnp.dot(ub[:, cs], wai_ref[n], preferred_element_type=F32)
            r = _sigmoid_tanh(pre[:, :RNN_BLOCK] + ba[:, cs])
            gi = _sigmoid_tanh(pre[:, RNN_BLOCK:] + bi[:, cs])
            log_a = -LRU_C * r * sp[:, cs]
            a = jnp.exp(log_a)
            mult = jnp.sqrt(1.0 - a * a)
            if first:
                mult = jnp.where((row == 0) & (step == 0), 1.0, mult)
            a_scr[n, pl.ds(bb, tb, stride=nb), :] = a
            b_scr[n, pl.ds(bb, tb, stride=nb), :] = mult * (gi * u[:, cs])
        return carry

    lax.fori_loop(0, nb, slab, 0)

    def tstep(t, h):
        base = pl.multiple_of(t * nb, nb)
        h = a_scr[:, pl.ds(base, nb), :] * h + b_scr[:, pl.ds(base, nb), :]
        hs_scr[:, pl.ds(base, nb), :] = h
        return h

    h_fin = lax.fori_loop(0, tb, tstep, h_scr[...], unroll=8)
    h_scr[...] = h_fin
    hl_ref[...] = h_fin

    def gate(bb, carry):
        hb = jnp.concatenate([hs_scr[n, pl.ds(bb, tb, stride=nb), :] for n in range(nct)], axis=1)
        o_ref[bb] = (hb * jax.nn.gelu(y_ref[bb].astype(F32))).astype(o_ref.dtype)
        return carry

    lax.fori_loop(0, nb, gate, 0)


def _rglru(proj3, ctx0, h0, cw, cb, wai, ba, bi, lam, *, first):
    nb, s, _ = proj3.shape
    tb = min(s, 128)
    c = D_RNN
    nct = c // LANES
    kern = functools.partial(_rglru_kernel, nb=nb, tb=tb, first=first)
    full = lambda shape: pl.BlockSpec(shape, lambda t: (0,) * len(shape))
    return pl.pallas_call(
        kern,
        grid=(s // tb,),
        in_specs=[
            pl.BlockSpec((nb, tb, c), lambda t: (0, t, COL_XR)),
            pl.BlockSpec((nb, tb, c), lambda t: (0, t, COL_YR)),
            full((SUBLANES, c)), full((nct, nb, LANES)), full((CONV_WIDTH, c)), full((1, c)),
            full((N_RNN_BLOCKS, RNN_BLOCK, 2 * RNN_BLOCK)), full((1, c)), full((1, c)), full((1, c)),
        ],
        out_specs=[pl.BlockSpec((nb, tb, c), lambda t: (0, t, 0)), full((nct, nb, LANES))],
        out_shape=[jax.ShapeDtypeStruct((nb, s, c), BF16), jax.ShapeDtypeStruct((nct, nb, LANES), F32)],
        scratch_shapes=[
            pltpu.VMEM((nb, SUBLANES, c), F32),
            pltpu.VMEM((nct, nb * tb, LANES), F32),
            pltpu.VMEM((nct, nb * tb, LANES), F32),
            pltpu.VMEM((nct, nb * tb, LANES), F32),
            pltpu.VMEM((nct, nb, LANES), F32),
        ],
        compiler_params=_cparams(("arbitrary",)),
        name="rglru_first" if first else "rglru",
    )(proj3, proj3, ctx0, h0, cw, cb, wai, ba, bi, lam)


ATTN_Q_PER_STEP = 4


def _attn_schedule(nq):
    assert nq % ATTN_Q_PER_STEP == 0
    qids, slots, kblocks = [], [], []
    for j in range(nq // ATTN_Q_PER_STEP):
        ids = [2 * j, nq - 1 - 2 * j, 2 * j + 1, nq - 2 - 2 * j]
        items = [(sl, kb) for sl, q in enumerate(ids) for kb in range(q)] + list(enumerate(ids))
        qids += ids
        slots += [it[0] for it in items]
        kblocks += [it[1] for it in items]
    as_i32 = lambda v: jnp.asarray(v, jnp.int32)
    return as_i32(qids), as_i32(slots), as_i32(kblocks), 2 * (nq - 1)


def _attn_kernel(qid_ref, slot_ref, kb_ref, lq1_ref, lk1_ref, lq2_ref, lk2_ref, g_ref, q_ref, k_ref, v_ref,
                 km_ref, vm_ref, *rest, tq, cast_groups, n_full):
    n_in, n_out = sum(cast_groups), len(cast_groups)
    cast_in, o_ref, cast_out = rest[:n_in], rest[n_in], rest[n_in + 1:n_in + 1 + n_out]
    q2_scr, m_scr, acc_scr, s_scr = rest[n_in + 1 + n_out:]
    first = 0
    for dst, n_src in zip(cast_out, cast_groups):
        parts = [src[...].astype(BF16) for src in cast_in[first:first + n_src]]
        dst[...] = parts[0] if n_src == 1 else jnp.concatenate(parts, axis=-1)
        first += n_src

    step = pl.program_id(2)
    n_items = n_full + ATTN_Q_PER_STEP
    hq = tq // 2
    lane = lax.broadcasted_iota(jnp.int32, (1, V_DIM), 1)

    def qrows(sl):
        return pl.multiple_of(qid_ref[step * ATTN_Q_PER_STEP + sl] * tq, tq)

    for sl in range(ATTN_Q_PER_STEP):
        q = (q_ref[pl.ds(qrows(sl), tq), :].astype(F32) * (HEAD_DIM ** -0.5 * LOG2_E)).astype(BF16)
        zero = jnp.zeros_like(q)
        q_m1, q_m2 = jnp.where(lane < HEAD_DIM, q, zero), jnp.where(lane >= HEAD_DIM, q, zero)
        q2_scr[sl] = jnp.concatenate([q_m1[:hq], q_m2[:hq], q_m1[hq:], q_m2[hq:]], axis=0)
    m_scr[...] = jnp.full(m_scr.shape, NEG_BIG, F32)
    acc_scr[...] = jnp.zeros(acc_scr.shape, F32)

    def scores(sl, r0, kblk):
        return lax.dot_general(q2_scr[sl, r0:, :], kblk, (((1,), (1,)), ((), ())),
                               preferred_element_type=F32)

    def consume(sl, r0, s, vblk, mask):
        width = s.shape[1]
        vext = jnp.concatenate([vblk, jnp.ones_like(vblk)], axis=1)
        if mask is not None:
            s = jnp.where(mask, s, NEG_BIG)
        m_prev = m_scr[sl, r0:, :]
        m_next = jnp.maximum(m_prev, jnp.max(s, axis=1, keepdims=True))
        p = jnp.exp2(s - jnp.tile(m_next, (1, width // LANES)))
        corr = jnp.exp2(m_prev - m_next)
        acc_scr[sl, r0:, :] = (jnp.tile(corr, (1, 2)) * acc_scr[sl, r0:, :]
                               + jnp.dot(p.astype(BF16), vext, preferred_element_type=F32))
        m_scr[sl, r0:, :] = m_next

    mcol = lax.broadcasted_iota(jnp.int32, (2 * tq, LANES), 1)
    r_i = lax.broadcasted_iota(jnp.int32, (2 * tq, hq), 0)
    c_i = lax.broadcasted_iota(jnp.int32, (2 * tq, hq), 1)
    within = c_i <= (r_i & (hq - 1))
    first_half_mask = (r_i >= tq) | within
    second_half_mask = within[tq:, :]

    work = [(sl, 0, None, LANES, mcol < N_META) for sl in range(ATTN_Q_PER_STEP)]
    for t in range(n_items):
        sl, kb = slot_ref[step * n_items + t], kb_ref[step * n_items + t]
        if t < n_full:
            work.append((sl, 0, pl.ds(pl.multiple_of(kb * tq, tq), tq), tq, None))
        else:
            work.append((sl, 0, pl.ds(pl.multiple_of(kb * tq, hq), hq), hq, first_half_mask))
            work.append((sl, tq, pl.ds(pl.multiple_of(kb * tq + hq, hq), hq), hq, second_half_mask))

    def keys(rows):
        return km_ref[...] if rows is None else k_ref[rows, :]

    def values(rows):
        return vm_ref[...] if rows is None else v_ref[rows, :]

    def put_scores(t):
        sl, r0, rows, width, _ = work[t]
        s_scr[t % 2, r0:, :width] = scores(sl, r0, keys(rows))

    put_scores(0)
    for t, (sl, r0, rows, width, mask) in enumerate(work):
        if t + 1 < len(work):
            put_scores(t + 1)
        consume(sl, r0, s_scr[t % 2, r0:, :width], values(rows), mask)

    lam = (jnp.exp(jnp.sum(lq1_ref[...] * lk1_ref[...], axis=1, keepdims=True))
           - jnp.exp(jnp.sum(lq2_ref[...] * lk2_ref[...], axis=1, keepdims=True)) + LAMBDA_INIT)
    for sl in range(ATTN_Q_PER_STEP):
        for half in range(2):
            a1 = acc_scr[sl, pl.ds((2 * half) * hq, hq), :]
            a2 = acc_scr[sl, pl.ds((2 * half + 1) * hq, hq), :]
            o = a1[:, :V_DIM] / a1[:, V_DIM:] - lam * (a2[:, :V_DIM] / a2[:, V_DIM:])
            o = o * lax.rsqrt(jnp.mean(o * o, axis=-1, keepdims=True) + LN_EPS) * g_ref[...]
            o_ref[pl.ds(qrows(sl) + half * hq, hq), :] = (o * (1.0 - LAMBDA_INIT)).astype(o_ref.dtype)


def _diff_attn(proj3, km, vm, lq1, lk1, lq2, lk2, g, to_bf16):
    nb, s, _ = proj3.shape
    tq = min(s // ATTN_Q_PER_STEP, 512)
    nq = s // tq
    qids, slots, kblocks, n_full = _attn_schedule(nq)
    n_steps = nq // ATTN_Q_PER_STEP
    steps = nb * N_HEADS * n_steps
    hb = D_ATTN // V_DIM
    small = lambda shape: pl.BlockSpec(shape, lambda b, h, j, *_: (0,) * len(shape))
    seq = lambda col: pl.BlockSpec((None, s, V_DIM), lambda b, h, j, *_: (b, 0, col * hb + h))
    meta = pl.BlockSpec((LANES, V_DIM), lambda b, h, j, *_: (0, h))
    chunk = lambda w: w.reshape(steps, w.size // w.shape[-1] // steps, w.shape[-1])
    chunk_spec = lambda shape: pl.BlockSpec((None,) + shape[1:],
                                            lambda b, h, j, *_: ((b * N_HEADS + h) * n_steps + j, 0, 0))
    chunked = [chunk(w) for group in to_bf16 for w in group]
    joined = [chunk(group[0]).shape[:2] + (sum(w.shape[-1] for w in group),) for group in to_bf16]
    grid_spec = pltpu.PrefetchScalarGridSpec(
        num_scalar_prefetch=3,
        grid=(nb, N_HEADS, n_steps),
        in_specs=[small((1, HEAD_DIM)), small((1, HEAD_DIM)), small((1, HEAD_DIM)), small((1, HEAD_DIM)),
                  small((1, V_DIM)), seq(COL_Q), seq(COL_K), seq(COL_V), meta, meta]
                 + [chunk_spec(c.shape) for c in chunked],
        out_specs=[pl.BlockSpec((None, s, V_DIM), lambda b, h, j, *_: (b, 0, h))]
                  + [chunk_spec(shape) for shape in joined],
        scratch_shapes=[
            pltpu.VMEM((ATTN_Q_PER_STEP, 2 * tq, V_DIM), BF16),
            pltpu.VMEM((ATTN_Q_PER_STEP, 2 * tq, LANES), F32),
            pltpu.VMEM((ATTN_Q_PER_STEP, 2 * tq, 2 * V_DIM), F32),
            pltpu.VMEM((2, 2 * tq, tq), F32),
        ],
    )
    outs = pl.pallas_call(
        functools.partial(_attn_kernel, tq=tq, cast_groups=tuple(len(group) for group in to_bf16),
                          n_full=n_full),
        grid_spec=grid_spec,
        out_shape=[jax.ShapeDtypeStruct((nb, s, D_ATTN), BF16)]
                  + [jax.ShapeDtypeStruct(shape, BF16) for shape in joined],
        compiler_params=_cparams(("parallel", "parallel", "arbitrary")),
        name="diff_attn",
    )(qids, slots, kblocks, lq1, lk1, lq2, lk2, g, proj3, proj3, proj3, km, vm, *chunked)
    return outs[0], [o.reshape(group[0].shape[:-1] + (o.shape[-1],)) for o, group in zip(outs[1:], to_bf16)]


def _merge_kernel(x_ref, rec_ref, att_ref, gr_ref, ga_ref, bgr_ref, bga_ref, wr_ref, wa_ref, wo_ref,
                  eg_ref, eb_ref, g1_ref, b1_ref, h_ref, hp_ref):
    r = jnp.dot(rec_ref[...], wr_ref[...], preferred_element_type=F32)
    a = jnp.dot(att_ref[...], wa_ref[...], preferred_element_type=F32)
    g_rec = jax.nn.sigmoid(gr_ref[...].astype(F32) + bgr_ref[...])
    g_att = jax.nn.sigmoid(ga_ref[...].astype(F32) + bga_ref[...])
    mix = jnp.dot((g_rec * r + g_att * a).astype(BF16), wo_ref[...], preferred_element_type=F32)
    h0 = _layer_norm_rows(x_ref[...], eg_ref[...], eb_ref[...])
    h1 = _layer_norm_rows(ALPHA * h0 + mix, g1_ref[...], b1_ref[...])
    h_ref[...] = h1
    hp_ref[...] = _pack_bf16_pairs(h1)


def _pack_bf16_pairs(v):
    half = v.shape[1] // 2
    lo = lax.bitcast_convert_type(v[:, :half].astype(BF16).astype(F32), U32) >> 16
    hi = lax.bitcast_convert_type(v[:, half:].astype(BF16).astype(F32), U32) & jnp.uint32(0xFFFF0000)
    return lo | hi


def _unpack_bf16_pairs(p):
    lo = lax.bitcast_convert_type(p << 16, F32)
    hi = lax.bitcast_convert_type(p & jnp.uint32(0xFFFF0000), F32)
    return lo, hi


def _merge(x2d, rec2d, att2d, proj2d, bgr, bga, wr, wa, wo, eg, eb, g1, b1):
    n, d = x2d.shape
    tm = min(n, 512)
    row = lambda col: pl.BlockSpec((tm, d), lambda i: (i, col))
    vec = pl.BlockSpec((1, d), lambda i: (0, 0))
    mat = pl.BlockSpec((d, d), lambda i: (0, 0))
    return pl.pallas_call(
        _merge_kernel,
        grid=(n // tm,),
        in_specs=[row(0), row(0), row(0), row(COL_GREC), row(COL_GATT), vec, vec, mat, mat, mat,
                  vec, vec, vec, vec],
        out_specs=[row(0), pl.BlockSpec((tm, d // 2), lambda i: (i, 0))],
        out_shape=[jax.ShapeDtypeStruct((n, d), F32), jax.ShapeDtypeStruct((n, d // 2), U32)],
        compiler_params=_cparams(("parallel",)),
        name="merge_ln1",
    )(x2d, rec2d, att2d, proj2d, proj2d, bgr, bga, wr, wa, wo, eg, eb, g1, b1)


def _split_bf16(v):
    hi = lax.bitcast_convert_type(lax.bitcast_convert_type(v, U32) & jnp.uint32(0xFFFF0000), F32)
    return hi.astype(BF16), (v - hi).astype(BF16)


def _router_kernel(h_ref, wt_hi_ref, wt_lo_ref, bias_ref, e_ref, w_ref, r_ref, c_ref, cnt_scr):
    tm = h_ref.shape[0]

    @pl.when(pl.program_id(0) == 0)
    def _():
        cnt_scr[...] = jnp.zeros(cnt_scr.shape, F32)

    h_hi, h_lo = _split_bf16(h_ref[...])
    nt = lambda a, b: lax.dot_general(a, b, (((1,), (1,)), ((), ())), preferred_element_type=F32)
    logits = nt(wt_hi_ref[...], h_hi) + nt(wt_hi_ref[...], h_lo) + nt(wt_lo_ref[...], h_hi)
    scores = jax.nn.sigmoid(logits)
    sel = scores + bias_ref[...]
    grp = sel.reshape(N_GROUPS, GROUP_SIZE, tm)
    gi = lax.broadcasted_iota(jnp.int32, grp.shape, 1)
    m1 = jnp.max(grp, axis=1, keepdims=True)
    first = jnp.min(jnp.where(grp == m1, gi, GROUP_SIZE), axis=1, keepdims=True)
    m2 = jnp.max(jnp.where(gi == first, -jnp.inf, grp), axis=1, keepdims=True)
    gscore = (m1 + m2).reshape(N_GROUPS, tm)

    gidx = lax.broadcasted_iota(jnp.int32, (N_GROUPS, tm), 0)
    keep = jnp.zeros((N_GROUPS, tm), jnp.bool_)
    cur = gscore
    for _ in range(TOPK_GROUPS):
        mx = jnp.max(cur, axis=0, keepdims=True)
        pick = gidx == jnp.min(jnp.where(cur == mx, gidx, N_GROUPS), axis=0, keepdims=True)
        keep = keep | pick
        cur = jnp.where(pick, -jnp.inf, cur)

    cur = jnp.where(keep[:, None, :], grp, -jnp.inf).reshape(N_EXPERTS, tm)
    eidx = lax.broadcasted_iota(jnp.int32, (N_EXPERTS, tm), 0)
    picked_e, picked_w, picks = [], [], []
    for _ in range(TOP_K):
        mx = jnp.max(cur, axis=0, keepdims=True)
        ei = jnp.min(jnp.where(cur == mx, eidx, N_EXPERTS), axis=0, keepdims=True)
        pick = eidx == ei
        picked_e.append(ei)
        picked_w.append(jnp.sum(jnp.where(pick, scores, 0.0), axis=0, keepdims=True))
        picks.append(pick)
        cur = jnp.where(pick, -jnp.inf, cur)
    w = jnp.concatenate(picked_w, axis=0)
    e_ref[...] = jnp.concatenate(picked_e, axis=0)
    w_ref[...] = w / jnp.sum(w, axis=0, keepdims=True) * ROUTED_SCALE

    member = functools.reduce(jnp.logical_or, picks)
    t_row = lax.broadcasted_iota(jnp.int32, (tm, tm), 0)
    t_col = lax.broadcasted_iota(jnp.int32, (tm, tm), 1)
    earlier = jnp.where(t_row < t_col, 1.0, 0.0).astype(BF16)
    member_f = jnp.where(member, 1.0, 0.0)
    before = cnt_scr[...] + jnp.dot(member_f.astype(BF16), earlier, preferred_element_type=F32)
    ranks = [jnp.sum(jnp.where(p, before, 0.0), axis=0, keepdims=True) for p in picks]
    r_ref[...] = jnp.concatenate(ranks, axis=0).astype(jnp.int32)
    total = cnt_scr[...] + jnp.sum(member_f, axis=1, keepdims=True)
    cnt_scr[...] = total
    c_ref[...] = total.astype(jnp.int32)


def _router(h1, router_wt, bias_col):
    n, d = h1.shape
    tm = min(n, 512)
    tok = pl.BlockSpec((TOP_K, tm), lambda i: (0, i))
    wt_hi, wt_lo = _split_bf16(router_wt)
    return pl.pallas_call(
        _router_kernel,
        grid=(n // tm,),
        in_specs=[
            pl.BlockSpec((tm, d), lambda i: (i, 0)),
            pl.BlockSpec((N_EXPERTS, d), lambda i: (0, 0)),
            pl.BlockSpec((N_EXPERTS, d), lambda i: (0, 0)),
            pl.BlockSpec((N_EXPERTS, 1), lambda i: (0, 0)),
        ],
        out_specs=[tok, tok, tok, pl.BlockSpec((N_EXPERTS, 1), lambda i: (0, 0))],
        out_shape=[jax.ShapeDtypeStruct((TOP_K, n), jnp.int32), jax.ShapeDtypeStruct((TOP_K, n), F32),
                   jax.ShapeDtypeStruct((TOP_K, n), jnp.int32),
                   jax.ShapeDtypeStruct((N_EXPERTS, 1), jnp.int32)],
        scratch_shapes=[pltpu.VMEM((N_EXPERTS, 1), F32)],
        compiler_params=_cparams(("arbitrary",)),
        name="router",
    )(h1, wt_hi, wt_lo, bias_col)


def _slots_kernel(e_ref, r_ref, ps_ref, o_ref):
    tm = e_ref.shape[1]
    eidx = lax.broadcasted_iota(jnp.int32, (N_EXPERTS, tm), 0)
    e = e_ref[...]
    ps = ps_ref[...]
    base = [jnp.sum(jnp.where(eidx == e[k:k + 1, :], ps, 0), axis=0, keepdims=True) for k in range(TOP_K)]
    o_ref[...] = jnp.concatenate(base, axis=0) + r_ref[...]


def _slots(eidx, rank, pstart_col):
    k, n = eidx.shape
    tm = min(n, 2048)
    tok = pl.BlockSpec((k, tm), lambda i: (0, i))
    return pl.pallas_call(
        _slots_kernel,
        grid=(n // tm,),
        in_specs=[tok, tok, pl.BlockSpec((N_EXPERTS, 1), lambda i: (0, 0))],
        out_specs=tok,
        out_shape=jax.ShapeDtypeStruct((k, n), jnp.int32),
        compiler_params=_cparams(("parallel",)),
        name="slots",
    )(eidx, rank, pstart_col)


SC_CORES = 2
SC_SUBCORES = 16
SC_WINDOW = 64


def _sc_mesh():
    return plsc.VectorSubcoreMesh(core_axis_name="core", subcore_axis_name="subcore")


def _sc_windows(slot_flat):
    m = slot_flat.shape[0]
    n_win = m // SC_WINDOW // (SC_CORES * SC_SUBCORES)
    assert n_win * SC_WINDOW * SC_CORES * SC_SUBCORES == m and n_win % 2 == 0
    return n_win, slot_flat.reshape(m // SC_WINDOW, SC_WINDOW)


def _sc_scratch(n_win, d, dtype):
    return [pltpu.VMEM((n_win, SC_WINDOW), jnp.int32), pltpu.VMEM((2, SC_WINDOW, d), dtype),
            pltpu.SemaphoreType.DMA, pltpu.SemaphoreType.DMA]


def _sc_scatter_rows(rows, slots, n_slots):
    n, d = rows.shape
    top_k = slots.shape[0]
    n_win = n // SC_WINDOW // (SC_CORES * SC_SUBCORES)
    assert n_win * SC_WINDOW * SC_CORES * SC_SUBCORES == n and n_win % 2 == 0
    slot_win = slots.reshape(top_k, n // SC_WINDOW, SC_WINDOW)

    @pl.kernel(out_type=jax.ShapeDtypeStruct((n_slots, d), rows.dtype), mesh=_sc_mesh(),
               scratch_types=[pltpu.VMEM((top_k, n_win, SC_WINDOW), jnp.int32),
                              pltpu.VMEM((2, SC_WINDOW, d), rows.dtype),
                              pltpu.SemaphoreType.DMA, pltpu.SemaphoreType.DMA])
    def scatter(x_hbm, i_hbm, o_hbm, idx_v, rows_v, sem0, sem1):
        wid = lax.axis_index("subcore") * SC_CORES + lax.axis_index("core")
        first = wid * n_win
        for k in range(top_k):
            pltpu.sync_copy(i_hbm.at[k, pl.ds(first, n_win)], idx_v.at[k])
        sems = (sem0, sem1)

        @pl.loop(0, n_win // 2)
        def _(pair):
            copies = []
            for b in range(2):
                w = pair * 2 + b
                pltpu.sync_copy(x_hbm.at[pl.ds((first + w) * SC_WINDOW, SC_WINDOW)], rows_v.at[b])
                for k in range(top_k):
                    cp = pltpu.make_async_copy(rows_v.at[b], o_hbm.at[idx_v.at[k, w]], sems[b])
                    cp.start()
                    copies.append(cp)
            for cp in copies:
                cp.wait()

    return scatter(rows, slot_win)


def _sc_gather_rows(table, slot_flat):
    d = table.shape[1]
    m = slot_flat.shape[0]
    n_win, slot_win = _sc_windows(slot_flat)

    @pl.kernel(out_type=jax.ShapeDtypeStruct((m, d), table.dtype), mesh=_sc_mesh(),
               scratch_types=_sc_scratch(n_win, d, table.dtype))
    def gather(t_hbm, i_hbm, o_hbm, idx_v, rows_v, sem0, sem1):
        wid = lax.axis_index("subcore") * SC_CORES + lax.axis_index("core")
        first = wid * n_win
        pltpu.sync_copy(i_hbm.at[pl.ds(first, n_win)], idx_v)
        sems = (sem0, sem1)

        def fetch(c, b):
            return pltpu.make_async_copy(t_hbm.at[idx_v.at[c]], rows_v.at[b], sems[b])

        def flush(c, b):
            pltpu.sync_copy(rows_v.at[b], o_hbm.at[pl.ds((first + c) * SC_WINDOW, SC_WINDOW)])

        fetch(0, 0).start()

        @pl.loop(0, n_win // 2)
        def _(pair):
            c = pair * 2
            fetch(c + 1, 1).start()
            fetch(c, 0).wait()
            flush(c, 0)

            @pl.when(c + 2 < n_win)
            def _():
                fetch(c + 2, 0).start()

            fetch(c + 1, 1).wait()
            flush(c + 1, 1)

    return gather(table, slot_win)


EXPERT_BLOCK = 512
BLOCKS_PER_STEP = 2


def _experts_kernel(bexp_ref, bvalid_ref, nused_ref, x_ref, *rest):
    o_ref = rest[2 * BLOCKS_PER_STEP]
    for half in range(BLOCKS_PER_STEP):
        blk = pl.program_id(0) * BLOCKS_PER_STEP + half
        wgu_ref, wd_ref = rest[2 * half], rest[2 * half + 1]
        rows = pl.ds(half * EXPERT_BLOCK, EXPERT_BLOCK)

        @pl.when(blk < nused_ref[0])
        def _():
            lo, hi = _unpack_bf16_pairs(x_ref[rows, :])
            x = jnp.concatenate([lo, hi], axis=1)
            rowid = lax.broadcasted_iota(jnp.int32, (EXPERT_BLOCK, 1), 0)
            x = jnp.where(rowid < bvalid_ref[blk], x, 0.0).astype(BF16)
            gu = jnp.dot(x, wgu_ref[...], preferred_element_type=F32)
            hmid = (jax.nn.silu(gu[:, :D_EXPERT]) * gu[:, D_EXPERT:]).astype(BF16)
            o_ref[rows, :] = _pack_bf16_pairs(jnp.dot(hmid, wd_ref[...], preferred_element_type=F32))


def _experts(block_exp, block_valid, n_used, xs, wgu, wd):
    n_slots, dh = xs.shape
    d = 2 * dh
    step_rows = BLOCKS_PER_STEP * EXPERT_BLOCK
    n_steps = n_slots // step_rows
    last_step = lambda nu: (nu[0] - 1) // BLOCKS_PER_STEP
    rows = lambda i, be, bv, nu: (jnp.minimum(i, last_step(nu)), 0)

    def weights(half):
        return lambda i, be, bv, nu: (be[jnp.minimum(i, last_step(nu)) * BLOCKS_PER_STEP + half], 0, 0)

    w_specs = []
    for half in range(BLOCKS_PER_STEP):
        w_specs += [pl.BlockSpec((None, d, 2 * D_EXPERT), weights(half)),
                    pl.BlockSpec((None, D_EXPERT, d), weights(half))]
    grid_spec = pltpu.PrefetchScalarGridSpec(
        num_scalar_prefetch=3,
        grid=(n_steps,),
        in_specs=[pl.BlockSpec((step_rows, dh), rows)] + w_specs,
        out_specs=pl.BlockSpec((step_rows, dh), rows),
    )
    return pl.pallas_call(
        _experts_kernel,
        grid_spec=grid_spec,
        out_shape=jax.ShapeDtypeStruct((n_slots, dh), U32),
        compiler_params=_cparams(("arbitrary",)),
        name="experts",
    )(block_exp, block_valid, n_used, xs, *([wgu, wd] * BLOCKS_PER_STEP))


def _final_kernel(h_ref, y_ref, w_ref, wgu_ref, wd_ref, g_ref, b_ref, o_ref):
    h1 = h_ref[...]
    gu = jnp.dot(h1.astype(BF16), wgu_ref[...], preferred_element_type=F32)
    hmid = (jax.nn.silu(gu[:, :D_SHARED]) * gu[:, D_SHARED:]).astype(BF16)
    shared = jnp.dot(hmid, wd_ref[...], preferred_element_type=F32)
    w = w_ref[...]
    r_lo = r_hi = None
    for k in range(TOP_K):
        lo, hi = _unpack_bf16_pairs(y_ref[k])
        wk = w[:, k:k + 1]
        r_lo = wk * lo if r_lo is None else r_lo + wk * lo
        r_hi = wk * hi if r_hi is None else r_hi + wk * hi
    ffn = jnp.concatenate([r_lo, r_hi], axis=1) + shared
    o_ref[...] = _layer_norm_rows(ALPHA * h1 + ffn, g_ref[...], b_ref[...])


def _final(h1, ytok, wtok, wgu, wd, g, b):
    n, d = h1.shape
    tm = min(n, 512)
    row = pl.BlockSpec((tm, d), lambda i: (i, 0))
    vec = pl.BlockSpec((1, d), lambda i: (0, 0))
    return pl.pallas_call(
        _final_kernel,
        grid=(n // tm,),
        in_specs=[row, pl.BlockSpec((TOP_K, tm, d // 2), lambda i: (0, i, 0)),
                  pl.BlockSpec((tm, TOP_K), lambda i: (i, 0)),
                  pl.BlockSpec((d, 2 * D_SHARED), lambda i: (0, 0)),
                  pl.BlockSpec((D_SHARED, d), lambda i: (0, 0)), vec, vec],
        out_specs=row,
        out_shape=jax.ShapeDtypeStruct((n, d), F32),
        compiler_params=_cparams(("parallel",)),
        name="shared_ln2",
    )(h1, ytok, wtok, wgu, wd, g, b)


def _segments(counts, n_assign):
    counts = counts.reshape(N_EXPERTS)
    padded = (counts + EXPERT_BLOCK - 1) // EXPERT_BLOCK * EXPERT_BLOCK
    seg_end = jnp.cumsum(padded)
    seg_start = seg_end - padded
    n_blocks = n_assign // EXPERT_BLOCK + N_EXPERTS
    block_start = jnp.arange(n_blocks, dtype=jnp.int32) * EXPERT_BLOCK
    block_exp = jnp.minimum(jnp.sum(seg_end[None, :] <= block_start[:, None], axis=1), N_EXPERTS - 1)
    mine = block_exp[:, None] == jnp.arange(N_EXPERTS)[None, :]
    rows_left = jnp.sum(jnp.where(mine, (counts + seg_start)[None, :], 0), axis=1) - block_start
    block_valid = jnp.clip(rows_left, 0, EXPERT_BLOCK)
    n_used = (seg_end[-1] // EXPERT_BLOCK).reshape(1)
    i32 = lambda v: v.astype(jnp.int32)
    return i32(seg_start), i32(block_exp), i32(block_valid), i32(n_used), n_blocks * EXPERT_BLOCK


def kernel(x, meta_tokens, emb_ln_g, emb_ln_b, w_in, b_gate, conv_w, conv_b, lru_w_a, lru_b_a, lru_w_i, lru_b_i, lru_lambda, lam_q1, lam_k1, lam_q2, lam_k2, subln_g, w_rec_proj, w_attn_proj, w_o, ln1_g, ln1_b, router_w, router_bias, w_e_gate, w_e_up, w_e_down, w_s_gate, w_s_up, w_s_down, ln2_g, ln2_b):
    nb, s, d = x.shape
    n = nb * s
    row = lambda v: v.reshape(1, -1).astype(F32)
    l = 0

    x2d = x.reshape(n, d)
    w_in_b = w_in[l].astype(BF16)
    eg, eb = row(emb_ln_g), row(emb_ln_b)
    proj = _ln_inproj(x2d, eg, eb, w_in_b)
    projm = _ln_inproj(meta_tokens.astype(F32), eg, eb, w_in_b)
    proj3 = proj.reshape(nb, s, IN_COLS)

    wai = jnp.concatenate([lru_w_a[l], lru_w_i[l]], axis=-1).astype(BF16)
    lru_args = (conv_w[l].astype(F32), row(conv_b[l]), wai, row(lru_b_a[l]), row(lru_b_i[l]),
                row(lru_lambda[l]))
    zeros8 = jnp.zeros((SUBLANES, D_RNN), F32)
    nct = D_RNN // LANES
    _, h_meta = _rglru(projm.reshape(1, N_META, IN_COLS), zeros8, jnp.zeros((nct, 1, LANES), F32),
                       *lru_args, first=True)
    ctx0 = projm[N_META - SUBLANES:, :D_RNN].astype(F32)
    rec, _ = _rglru(proj3, ctx0, jnp.broadcast_to(h_meta, (nct, nb, LANES)), *lru_args, first=False)

    pad = ((0, LANES - N_META), (0, 0))
    km = jnp.pad(projm[:, COL_K * D_ATTN:(COL_K + 1) * D_ATTN], pad)
    vm = jnp.pad(projm[:, COL_V * D_ATTN:(COL_V + 1) * D_ATTN], pad)
    att, (wgu_b, wd_b) = _diff_attn(
        proj3, km, vm, row(lam_q1[l]), row(lam_k1[l]), row(lam_q2[l]), row(lam_k2[l]), row(subln_g[l]),
        to_bf16=((w_e_gate[l], w_e_up[l]), (w_e_down[l],)))

    bg = b_gate[l].astype(F32)
    h1, h1p = _merge(x2d, rec.reshape(n, D_RNN), att.reshape(n, D_ATTN), proj,
                     bg[:D_MODEL].reshape(1, -1), bg[D_MODEL:].reshape(1, -1),
                     w_rec_proj[l].astype(BF16), w_attn_proj[l].astype(BF16), w_o[l].astype(BF16),
                     eg, eb, row(ln1_g[l]), row(ln1_b[l]))

    eidx, wts, rank, counts = _router(h1, router_w[l].T.astype(F32),
                                      router_bias[l].reshape(-1, 1).astype(F32))
    seg_start, block_exp, block_valid, n_used, n_slots = _segments(counts, TOP_K * n)
    slots = _slots(eidx, rank, seg_start.reshape(-1, 1))

    xs = _sc_scatter_rows(h1p, slots, n_slots)
    ys = _experts(block_exp, block_valid, n_used, xs, wgu_b, wd_b)
    ytok = _sc_gather_rows(ys, slots.reshape(TOP_K * n)).reshape(TOP_K, n, d // 2)

    wgu = jnp.concatenate([w_s_gate[l], w_s_up[l]], axis=-1).astype(BF16)
    out = _final(h1, ytok, wts.T, wgu, w_s_down[l].astype(BF16), row(ln2_g[l]), row(ln2_b[l]))
    return out.reshape(nb, s, d)
```

```python
import functools
import math

import jax
import jax.numpy as jnp
from jax import lax
from jax.experimental import pallas as pl
from jax.experimental.pallas import tpu as pltpu
from jax.experimental.pallas import tpu_sc as plsc

F32 = jnp.float32
BF16 = jnp.bfloat16
U32 = jnp.uint32

N_META = 16
D_MODEL = 1024
D_RNN = 1024
N_RNN_BLOCKS = 8
RNN_BLOCK = D_RNN // N_RNN_BLOCKS
CONV_WIDTH = 4
LRU_C = 8.0
N_HEADS = 8
HEAD_DIM = 64
V_DIM = 2 * HEAD_DIM
D_ATTN = N_HEADS * V_DIM
N_EXPERTS = 256
TOP_K = 8
N_GROUPS = 8
GROUP_SIZE = N_EXPERTS // N_GROUPS
TOPK_GROUPS = 4
D_EXPERT = 256
D_SHARED = 256
ROUTED_SCALE = 2.5
LN_EPS = 1e-5
DEPTH = 1
IN_COLS = 2 * D_RNN + 3 * D_ATTN + 2 * D_MODEL
ALPHA = (2.0 * DEPTH) ** 0.25
LAMBDA_INIT = 0.8 - 0.6 * math.exp(-0.3 * 0)

LANES = 128
SUBLANES = 8
VMEM_LIMIT = 56 * 1024 * 1024
NEG_BIG = -1e30
LOG2_E = math.log2(math.e)

IN_PROJ_TILE = (2048, 1024)
LRU_TIME_BLOCK = 128
ATTN_Q_BLOCK = 512
ROW_TILE = 512
SLOT_TILE = 2048

COL_XR, COL_YR, COL_Q, COL_K, COL_V, COL_GREC, COL_GATT = range(7)


def _cparams(sem):
    return pltpu.CompilerParams(dimension_semantics=sem, vmem_limit_bytes=VMEM_LIMIT)


def _layer_norm_rows(x, g, b):
    mu = jnp.mean(x, axis=-1, keepdims=True)
    xc = x - mu
    var = jnp.mean(xc * xc, axis=-1, keepdims=True)
    return xc * lax.rsqrt(var + LN_EPS) * g + b


def _ln_inproj_kernel(x_ref, g_ref, b_ref, w_ref, o_ref, xn_ref):
    @pl.when(pl.program_id(1) == 0)
    def _():
        xn_ref[...] = _layer_norm_rows(x_ref[...], g_ref[...], b_ref[...]).astype(BF16)

    o_ref[...] = jnp.dot(xn_ref[...], w_ref[...], preferred_element_type=F32).astype(o_ref.dtype)


def _ln_inproj(x2d, g, b, w_bf16):
    n, d = x2d.shape
    cols = w_bf16.shape[1]
    tm = min(n, IN_PROJ_TILE[0])
    tn = IN_PROJ_TILE[1]
    return pl.pallas_call(
        _ln_inproj_kernel,
        grid=(n // tm, cols // tn),
        in_specs=[
            pl.BlockSpec((tm, d), lambda i, j: (i, 0)),
            pl.BlockSpec((1, d), lambda i, j: (0, 0)),
            pl.BlockSpec((1, d), lambda i, j: (0, 0)),
            pl.BlockSpec((d, tn), lambda i, j: (0, j)),
        ],
        out_specs=pl.BlockSpec((tm, tn), lambda i, j: (i, j)),
        out_shape=jax.ShapeDtypeStruct((n, cols), BF16),
        scratch_shapes=[pltpu.VMEM((tm, d), BF16)],
        compiler_params=_cparams(("parallel", "arbitrary")),
        name="ln_inproj",
    )(x2d, g, b, w_bf16)


def _sigmoid_tanh(v):
    return 0.5 * jnp.tanh(0.5 * v) + 0.5


def _rglru_kernel(x_ref, y_ref, ctx0_ref, h0_ref, cw_ref, cb_ref, wai_ref, ba_ref, bi_ref,
                  lam_ref, o_ref, hl_ref, ctx_scr, a_scr, b_scr, hs_scr, h_scr, *, nb, tb, first):
    step = pl.program_id(0)
    nct = D_RNN // LANES

    @pl.when(step == 0)
    def _():
        for bb in range(nb):
            ctx_scr[bb] = ctx0_ref[...]
        h_scr[...] = h0_ref[...]

    lam = lam_ref[...]
    sp = jnp.maximum(-lam, 0.0) + jnp.log1p(jnp.exp(-jnp.abs(lam)))
    cw = cw_ref[...]
    cb = cb_ref[...]
    ba = ba_ref[...]
    bi = bi_ref[...]
    row = lax.broadcasted_iota(jnp.int32, (tb, 1), 0)

    def slab(bb, carry):
        xb = x_ref[bb].astype(F32)
        xcat = jnp.concatenate([ctx_scr[bb], xb], axis=0)
        u = cb + cw[CONV_WIDTH - 1:CONV_WIDTH, :] * xb
        for s in range(1, CONV_WIDTH):
            shifted = pltpu.roll(xcat, s, 0)[SUBLANES:, :]
            u = u + cw[CONV_WIDTH - 1 - s:CONV_WIDTH - s, :] * shifted
        ctx_scr[bb] = xb[tb - SUBLANES:, :]
        ub = u.astype(BF16)
        for n in range(N_RNN_BLOCKS):
            cs = slice(n * RNN_BLOCK, (n + 1) * RNN_BLOCK)
            pre = jnp.dot(ub[:, cs], wai_ref[n], preferred_element_type=F32)
            r = _sigmoid_tanh(pre[:, :RNN_BLOCK] + ba[:, cs])
            gi = _sigmoid_tanh(pre[:, RNN_BLOCK:] + bi[:, cs])
            log_a = -LRU_C * r * sp[:, cs]
            a = jnp.exp(log_a)
            mult = jnp.sqrt(1.0 - a * a)
            if first:
                mult = jnp.where((row == 0) & (step == 0), 1.0, mult)
            a_scr[n, pl.ds(bb, tb, stride=nb), :] = a
            b_scr[n, pl.ds(bb, tb, stride=nb), :] = mult * (gi * u[:, cs])
        return carry

    lax.fori_loop(0, nb, slab, 0)

    def tstep(t, h):
        base = pl.multiple_of(t * nb, nb)
        h = a_scr[:, pl.ds(base, nb), :] * h + b_scr[:, pl.ds(base, nb), :]
        hs_scr[:, pl.ds(base, nb), :] = h
        return h

    h_fin = lax.fori_loop(0, tb, tstep, h_scr[...], unroll=8)
    h_scr[...] = h_fin
    hl_ref[...] = h_fin

    def gate(bb, carry):
        hb = jnp.concatenate([hs_scr[n, pl.ds(bb, tb, stride=nb), :] for n in range(nct)], axis=1)
        o_ref[bb] = (hb * jax.nn.gelu(y_ref[bb].astype(F32))).astype(o_ref.dtype)
        return carry

    lax.fori_loop(0, nb, gate, 0)


def _rglru(proj3, ctx0, h0, cw, cb, wai, ba, bi, lam, *, first):
    nb, s, _ = proj3.shape
    tb = min(s, LRU_TIME_BLOCK)
    c = D_RNN
    nct = c // LANES
    kern = functools.partial(_rglru_kernel, nb=nb, tb=tb, first=first)
    full = lambda shape: pl.BlockSpec(shape, lambda t: (0,) * len(shape))
    return pl.pallas_call(
        kern,
        grid=(s // tb,),
        in_specs=[
            pl.BlockSpec((nb, tb, c), lambda t: (0, t, COL_XR)),
            pl.BlockSpec((nb, tb, c), lambda t: (0, t, COL_YR)),
            full((SUBLANES, c)), full((nct, nb, LANES)), full((CONV_WIDTH, c)), full((1, c)),
            full((N_RNN_BLOCKS, RNN_BLOCK, 2 * RNN_BLOCK)), full((1, c)), full((1, c)), full((1, c)),
        ],
        out_specs=[pl.BlockSpec((nb, tb, c), lambda t: (0, t, 0)), full((nct, nb, LANES))],
        out_shape=[jax.ShapeDtypeStruct((nb, s, c), BF16), jax.ShapeDtypeStruct((nct, nb, LANES), F32)],
        scratch_shapes=[
            pltpu.VMEM((nb, SUBLANES, c), F32),
            pltpu.VMEM((nct, nb * tb, LANES), F32),
            pltpu.VMEM((nct, nb * tb, LANES), F32),
            pltpu.VMEM((nct, nb * tb, LANES), F32),
            pltpu.VMEM((nct, nb, LANES), F32),
        ],
        compiler_params=_cparams(("arbitrary",)),
        name="rglru_first" if first else "rglru",
    )(proj3, proj3, ctx0, h0, cw, cb, wai, ba, bi, lam)


ATTN_Q_PER_STEP = 4


def _attn_schedule(nq):
    assert nq % ATTN_Q_PER_STEP == 0
    qids, slots, kblocks = [], [], []
    for j in range(nq // ATTN_Q_PER_STEP):
        ids = [2 * j, nq - 1 - 2 * j, 2 * j + 1, nq - 2 - 2 * j]
        items = [(sl, kb) for sl, q in enumerate(ids) for kb in range(q)] + list(enumerate(ids))
        qids += ids
        slots += [it[0] for it in items]
        kblocks += [it[1] for it in items]
    as_i32 = lambda v: jnp.asarray(v, jnp.int32)
    return as_i32(qids), as_i32(slots), as_i32(kblocks), 2 * (nq - 1)


def _attn_kernel(qid_ref, slot_ref, kb_ref, lq1_ref, lk1_ref, lq2_ref, lk2_ref, g_ref, q_ref, k_ref, v_ref,
                 km_ref, vm_ref, *rest, tq, cast_groups, n_full):
    n_in, n_out = sum(cast_groups), len(cast_groups)
    cast_in, o_ref, cast_out = rest[:n_in], rest[n_in], rest[n_in + 1:n_in + 1 + n_out]
    q2_scr, m_scr, acc_scr, s_scr = rest[n_in + 1 + n_out:]
    first = 0
    for dst, n_src in zip(cast_out, cast_groups):
        parts = [src[...].astype(BF16) for src in cast_in[first:first + n_src]]
        dst[...] = parts[0] if n_src == 1 else jnp.concatenate(parts, axis=-1)
        first += n_src

    step = pl.program_id(2)
    n_items = n_full + ATTN_Q_PER_STEP
    hq = tq // 2
    lane = lax.broadcasted_iota(jnp.int32, (1, V_DIM), 1)

    def qrows(sl):
        return pl.multiple_of(qid_ref[step * ATTN_Q_PER_STEP + sl] * tq, tq)

    for sl in range(ATTN_Q_PER_STEP):
        q = (q_ref[pl.ds(qrows(sl), tq), :].astype(F32) * (HEAD_DIM ** -0.5 * LOG2_E)).astype(BF16)
        zero = jnp.zeros_like(q)
        q_m1, q_m2 = jnp.where(lane < HEAD_DIM, q, zero), jnp.where(lane >= HEAD_DIM, q, zero)
        q2_scr[sl] = jnp.concatenate([q_m1[:hq], q_m2[:hq], q_m1[hq:], q_m2[hq:]], axis=0)
    m_scr[...] = jnp.full(m_scr.shape, NEG_BIG, F32)
    acc_scr[...] = jnp.zeros(acc_scr.shape, F32)

    def scores(sl, r0, kblk):
        return lax.dot_general(q2_scr[sl, r0:, :], kblk, (((1,), (1,)), ((), ())),
                               preferred_element_type=F32)

    def consume(sl, r0, s, vblk, mask):
        width = s.shape[1]
        vext = jnp.concatenate([vblk, jnp.ones_like(vblk)], axis=1)
        if mask is not None:
            s = jnp.where(mask, s, NEG_BIG)
        m_prev = m_scr[sl, r0:, :]
        m_next = jnp.maximum(m_prev, jnp.max(s, axis=1, keepdims=True))
        p = jnp.exp2(s - jnp.tile(m_next, (1, width // LANES)))
        corr = jnp.exp2(m_prev - m_next)
        acc_scr[sl, r0:, :] = (jnp.tile(corr, (1, 2)) * acc_scr[sl, r0:, :]
                               + jnp.dot(p.astype(BF16), vext, preferred_element_type=F32))
        m_scr[sl, r0:, :] = m_next

    mcol = lax.broadcasted_iota(jnp.int32, (2 * tq, LANES), 1)
    r_i = lax.broadcasted_iota(jnp.int32, (2 * tq, hq), 0)
    c_i = lax.broadcasted_iota(jnp.int32, (2 * tq, hq), 1)
    within = c_i <= (r_i & (hq - 1))
    first_half_mask = (r_i >= tq) | within
    second_half_mask = within[tq:, :]

    work = [(sl, 0, None, LANES, mcol < N_META) for sl in range(ATTN_Q_PER_STEP)]
    for t in range(n_items):
        sl, kb = slot_ref[step * n_items + t], kb_ref[step * n_items + t]
        if t < n_full:
            work.append((sl, 0, pl.ds(pl.multiple_of(kb * tq, tq), tq), tq, None))
        else:
            work.append((sl, 0, pl.ds(pl.multiple_of(kb * tq, hq), hq), hq, first_half_mask))
            work.append((sl, tq, pl.ds(pl.multiple_of(kb * tq + hq, hq), hq), hq, second_half_mask))

    def keys(rows):
        return km_ref[...] if rows is None else k_ref[rows, :]

    def values(rows):
        return vm_ref[...] if rows is None else v_ref[rows, :]

    def put_scores(t):
        sl, r0, rows, width, _ = work[t]
        s_scr[t % 2, r0:, :width] = scores(sl, r0, keys(rows))

    put_scores(0)
    for t, (sl, r0, rows, width, mask) in enumerate(work):
        if t + 1 < len(work):
            put_scores(t + 1)
        consume(sl, r0, s_scr[t % 2, r0:, :width], values(rows), mask)

    lam = (jnp.exp(jnp.sum(lq1_ref[...] * lk1_ref[...], axis=1, keepdims=True))
           - jnp.exp(jnp.sum(lq2_ref[...] * lk2_ref[...], axis=1, keepdims=True)) + LAMBDA_INIT)
    for sl in range(ATTN_Q_PER_STEP):
        for half in range(2):
            a1 = acc_scr[sl, pl.ds((2 * half) * hq, hq), :]
            a2 = acc_scr[sl, pl.ds((2 * half + 1) * hq, hq), :]
            o = a1[:, :V_DIM] / a1[:, V_DIM:] - lam * (a2[:, :V_DIM] / a2[:, V_DIM:])
            o = o * lax.rsqrt(jnp.mean(o * o, axis=-1, keepdims=True) + LN_EPS) * g_ref[...]
            o_ref[pl.ds(qrows(sl) + half * hq, hq), :] = (o * (1.0 - LAMBDA_INIT)).astype(o_ref.dtype)


def _diff_attn(proj3, km, vm, lq1, lk1, lq2, lk2, g, to_bf16):
    nb, s, _ = proj3.shape
    tq = min(s // ATTN_Q_PER_STEP, ATTN_Q_BLOCK)
    nq = s // tq
    qids, slots, kblocks, n_full = _attn_schedule(nq)
    n_steps = nq // ATTN_Q_PER_STEP
    steps = nb * N_HEADS * n_steps
    hb = D_ATTN // V_DIM
    small = lambda shape: pl.BlockSpec(shape, lambda b, h, j, *_: (0,) * len(shape))
    seq = lambda col: pl.BlockSpec((None, s, V_DIM), lambda b, h, j, *_: (b, 0, col * hb + h))
    meta = pl.BlockSpec((LANES, V_DIM), lambda b, h, j, *_: (0, h))
    chunk = lambda w: w.reshape(steps, w.size // w.shape[-1] // steps, w.shape[-1])
    chunk_spec = lambda shape: pl.BlockSpec((None,) + shape[1:],
                                            lambda b, h, j, *_: ((b * N_HEADS + h) * n_steps + j, 0, 0))
    chunked = [chunk(w) for group in to_bf16 for w in group]
    joined = [chunk(group[0]).shape[:2] + (sum(w.shape[-1] for w in group),) for group in to_bf16]
    grid_spec = pltpu.PrefetchScalarGridSpec(
        num_scalar_prefetch=3,
        grid=(nb, N_HEADS, n_steps),
        in_specs=[small((1, HEAD_DIM)), small((1, HEAD_DIM)), small((1, HEAD_DIM)), small((1, HEAD_DIM)),
                  small((1, V_DIM)), seq(COL_Q), seq(COL_K), seq(COL_V), meta, meta]
                 + [chunk_spec(c.shape) for c in chunked],
        out_specs=[pl.BlockSpec((None, s, V_DIM), lambda b, h, j, *_: (b, 0, h))]
                  + [chunk_spec(shape) for shape in joined],
        scratch_shapes=[
            pltpu.VMEM((ATTN_Q_PER_STEP, 2 * tq, V_DIM), BF16),
            pltpu.VMEM((ATTN_Q_PER_STEP, 2 * tq, LANES), F32),
            pltpu.VMEM((ATTN_Q_PER_STEP, 2 * tq, 2 * V_DIM), F32),
            pltpu.VMEM((2, 2 * tq, tq), F32),
        ],
    )
    outs = pl.pallas_call(
        functools.partial(_attn_kernel, tq=tq, cast_groups=tuple(len(group) for group in to_bf16),
                          n_full=n_full),
        grid_spec=grid_spec,
        out_shape=[jax.ShapeDtypeStruct((nb, s, D_ATTN), BF16)]
                  + [jax.ShapeDtypeStruct(shape, BF16) for shape in joined],
        compiler_params=_cparams(("parallel", "parallel", "arbitrary")),
        name="diff_attn",
    )(qids, slots, kblocks, lq1, lk1, lq2, lk2, g, proj3, proj3, proj3, km, vm, *chunked)
    return outs[0], [o.reshape(group[0].shape[:-1] + (o.shape[-1],)) for o, group in zip(outs[1:], to_bf16)]


def _merge_kernel(x_ref, rec_ref, att_ref, gr_ref, ga_ref, bgr_ref, bga_ref, wr_ref, wa_ref, wo_ref,
                  eg_ref, eb_ref, g1_ref, b1_ref, h_ref, hp_ref):
    r = jnp.dot(rec_ref[...], wr_ref[...], preferred_element_type=F32)
    a = jnp.dot(att_ref[...], wa_ref[...], preferred_element_type=F32)
    g_rec = jax.nn.sigmoid(gr_ref[...].astype(F32) + bgr_ref[...])
    g_att = jax.nn.sigmoid(ga_ref[...].astype(F32) + bga_ref[...])
    mix = jnp.dot((g_rec * r + g_att * a).astype(BF16), wo_ref[...], preferred_element_type=F32)
    h0 = _layer_norm_rows(x_ref[...], eg_ref[...], eb_ref[...])
    h1 = _layer_norm_rows(ALPHA * h0 + mix, g1_ref[...], b1_ref[...])
    h_ref[...] = h1
    hp_ref[...] = _pack_bf16_pairs(h1)


def _pack_bf16_pairs(v):
    half = v.shape[1] // 2
    lo = lax.bitcast_convert_type(v[:, :half].astype(BF16).astype(F32), U32) >> 16
    hi = lax.bitcast_convert_type(v[:, half:].astype(BF16).astype(F32), U32) & jnp.uint32(0xFFFF0000)
    return lo | hi


def _unpack_bf16_pairs(p):
    lo = lax.bitcast_convert_type(p << 16, F32)
    hi = lax.bitcast_convert_type(p & jnp.uint32(0xFFFF0000), F32)
    return lo, hi


def _merge(x2d, rec2d, att2d, proj2d, bgr, bga, wr, wa, wo, eg, eb, g1, b1):
    n, d = x2d.shape
    tm = min(n, ROW_TILE)
    row = lambda col: pl.BlockSpec((tm, d), lambda i: (i, col))
    vec = pl.BlockSpec((1, d), lambda i: (0, 0))
    mat = pl.BlockSpec((d, d), lambda i: (0, 0))
    return pl.pallas_call(
        _merge_kernel,
        grid=(n // tm,),
        in_specs=[row(0), row(0), row(0), row(COL_GREC), row(COL_GATT), vec, vec, mat, mat, mat,
                  vec, vec, vec, vec],
        out_specs=[row(0), pl.BlockSpec((tm, d // 2), lambda i: (i, 0))],
        out_shape=[jax.ShapeDtypeStruct((n, d), F32), jax.ShapeDtypeStruct((n, d // 2), U32)],
        compiler_params=_cparams(("parallel",)),
        name="merge_ln1",
    )(x2d, rec2d, att2d, proj2d, proj2d, bgr, bga, wr, wa, wo, eg, eb, g1, b1)


def _split_bf16(v):
    hi = lax.bitcast_convert_type(lax.bitcast_convert_type(v, U32) & jnp.uint32(0xFFFF0000), F32)
    return hi.astype(BF16), (v - hi).astype(BF16)


def _router_kernel(h_ref, wt_hi_ref, wt_lo_ref, bias_ref, e_ref, w_ref, r_ref, c_ref, cnt_scr):
    tm = h_ref.shape[0]

    @pl.when(pl.program_id(0) == 0)
    def _():
        cnt_scr[...] = jnp.zeros(cnt_scr.shape, F32)

    h_hi, h_lo = _split_bf16(h_ref[...])
    nt = lambda a, b: lax.dot_general(a, b, (((1,), (1,)), ((), ())), preferred_element_type=F32)
    logits = nt(wt_hi_ref[...], h_hi) + nt(wt_hi_ref[...], h_lo) + nt(wt_lo_ref[...], h_hi)
    scores = jax.nn.sigmoid(logits)
    sel = scores + bias_ref[...]
    grp = sel.reshape(N_GROUPS, GROUP_SIZE, tm)
    gi = lax.broadcasted_iota(jnp.int32, grp.shape, 1)
    m1 = jnp.max(grp, axis=1, keepdims=True)
    first = jnp.min(jnp.where(grp == m1, gi, GROUP_SIZE), axis=1, keepdims=True)
    m2 = jnp.max(jnp.where(gi == first, -jnp.inf, grp), axis=1, keepdims=True)
    gscore = (m1 + m2).reshape(N_GROUPS, tm)

    gidx = lax.broadcasted_iota(jnp.int32, (N_GROUPS, tm), 0)
    keep = jnp.zeros((N_GROUPS, tm), jnp.bool_)
    cur = gscore
    for _ in range(TOPK_GROUPS):
        mx = jnp.max(cur, axis=0, keepdims=True)
        pick = gidx == jnp.min(jnp.where(cur == mx, gidx, N_GROUPS), axis=0, keepdims=True)
        keep = keep | pick
        cur = jnp.where(pick, -jnp.inf, cur)

    cur = jnp.where(keep[:, None, :], grp, -jnp.inf).reshape(N_EXPERTS, tm)
    eidx = lax.broadcasted_iota(jnp.int32, (N_EXPERTS, tm), 0)
    picked_e, picked_w, picks = [], [], []
    for _ in range(TOP_K):
        mx = jnp.max(cur, axis=0, keepdims=True)
        ei = jnp.min(jnp.where(cur == mx, eidx, N_EXPERTS), axis=0, keepdims=True)
        pick = eidx == ei
        picked_e.append(ei)
        picked_w.append(jnp.sum(jnp.where(pick, scores, 0.0), axis=0, keepdims=True))
        picks.append(pick)
        cur = jnp.where(pick, -jnp.inf, cur)
    w = jnp.concatenate(picked_w, axis=0)
    e_ref[...] = jnp.concatenate(picked_e, axis=0)
    w_ref[...] = w / jnp.sum(w, axis=0, keepdims=True) * ROUTED_SCALE

    member = functools.reduce(jnp.logical_or, picks)
    t_row = lax.broadcasted_iota(jnp.int32, (tm, tm), 0)
    t_col = lax.broadcasted_iota(jnp.int32, (tm, tm), 1)
    earlier = jnp.where(t_row < t_col, 1.0, 0.0).astype(BF16)
    member_f = jnp.where(member, 1.0, 0.0)
    before = cnt_scr[...] + jnp.dot(member_f.astype(BF16), earlier, preferred_element_type=F32)
    ranks = [jnp.sum(jnp.where(p, before, 0.0), axis=0, keepdims=True) for p in picks]
    r_ref[...] = jnp.concatenate(ranks, axis=0).astype(jnp.int32)
    total = cnt_scr[...] + jnp.sum(member_f, axis=1, keepdims=True)
    cnt_scr[...] = total
    c_ref[...] = total.astype(jnp.int32)


def _router(h1, router_wt, bias_col):
    n, d = h1.shape
    tm = min(n, ROW_TILE)
    tok = pl.BlockSpec((TOP_K, tm), lambda i: (0, i))
    wt_hi, wt_lo = _split_bf16(router_wt)
    return pl.pallas_call(
        _router_kernel,
        grid=(n // tm,),
        in_specs=[
            pl.BlockSpec((tm, d), lambda i: (i, 0)),
            pl.BlockSpec((N_EXPERTS, d), lambda i: (0, 0)),
            pl.BlockSpec((N_EXPERTS, d), lambda i: (0, 0)),
            pl.BlockSpec((N_EXPERTS, 1), lambda i: (0, 0)),
        ],
        out_specs=[tok, tok, tok, pl.BlockSpec((N_EXPERTS, 1), lambda i: (0, 0))],
        out_shape=[jax.ShapeDtypeStruct((TOP_K, n), jnp.int32), jax.ShapeDtypeStruct((TOP_K, n), F32),
                   jax.ShapeDtypeStruct((TOP_K, n), jnp.int32),
                   jax.ShapeDtypeStruct((N_EXPERTS, 1), jnp.int32)],
        scratch_shapes=[pltpu.VMEM((N_EXPERTS, 1), F32)],
        compiler_params=_cparams(("arbitrary",)),
        name="router",
    )(h1, wt_hi, wt_lo, bias_col)


def _slots_kernel(e_ref, r_ref, ps_ref, o_ref):
    tm = e_ref.shape[1]
    eidx = lax.broadcasted_iota(jnp.int32, (N_EXPERTS, tm), 0)
    e = e_ref[...]
    ps = ps_ref[...]
    base = [jnp.sum(jnp.where(eidx == e[k:k + 1, :], ps, 0), axis=0, keepdims=True) for k in range(TOP_K)]
    o_ref[...] = jnp.concatenate(base, axis=0) + r_ref[...]


def _slots(eidx, rank, pstart_col):
    k, n = eidx.shape
    tm = min(n, SLOT_TILE)
    tok = pl.BlockSpec((k, tm), lambda i: (0, i))
    return pl.pallas_call(
        _slots_kernel,
        grid=(n // tm,),
        in_specs=[tok, tok, pl.BlockSpec((N_EXPERTS, 1), lambda i: (0, 0))],
        out_specs=tok,
        out_shape=jax.ShapeDtypeStruct((k, n), jnp.int32),
        compiler_params=_cparams(("parallel",)),
        name="slots",
    )(eidx, rank, pstart_col)


SC_CORES = 2
SC_SUBCORES = 16
SC_WINDOW = 64


def _sc_mesh():
    return plsc.VectorSubcoreMesh(core_axis_name="core", subcore_axis_name="subcore")


def _sc_windows(slot_flat):
    m = slot_flat.shape[0]
    n_win = m // SC_WINDOW // (SC_CORES * SC_SUBCORES)
    assert n_win * SC_WINDOW * SC_CORES * SC_SUBCORES == m and n_win % 2 == 0
    return n_win, slot_flat.reshape(m // SC_WINDOW, SC_WINDOW)


def _sc_scratch(n_win, d, dtype):
    return [pltpu.VMEM((n_win, SC_WINDOW), jnp.int32), pltpu.VMEM((2, SC_WINDOW, d), dtype),
            pltpu.SemaphoreType.DMA, pltpu.SemaphoreType.DMA]


def _sc_scatter_rows(rows, slots, n_slots):
    n, d = rows.shape
    top_k = slots.shape[0]
    n_win = n // SC_WINDOW // (SC_CORES * SC_SUBCORES)
    assert n_win * SC_WINDOW * SC_CORES * SC_SUBCORES == n and n_win % 2 == 0
    slot_win = slots.reshape(top_k, n // SC_WINDOW, SC_WINDOW)

    @pl.kernel(out_type=jax.ShapeDtypeStruct((n_slots, d), rows.dtype), mesh=_sc_mesh(),
               scratch_types=[pltpu.VMEM((top_k, n_win, SC_WINDOW), jnp.int32),
                              pltpu.VMEM((2, SC_WINDOW, d), rows.dtype),
                              pltpu.SemaphoreType.DMA, pltpu.SemaphoreType.DMA])
    def scatter(x_hbm, i_hbm, o_hbm, idx_v, rows_v, sem0, sem1):
        wid = lax.axis_index("subcore") * SC_CORES + lax.axis_index("core")
        first = wid * n_win
        for k in range(top_k):
            pltpu.sync_copy(i_hbm.at[k, pl.ds(first, n_win)], idx_v.at[k])
        sems = (sem0, sem1)

        @pl.loop(0, n_win // 2)
        def _(pair):
            copies = []
            for b in range(2):
                w = pair * 2 + b
                pltpu.sync_copy(x_hbm.at[pl.ds((first + w) * SC_WINDOW, SC_WINDOW)], rows_v.at[b])
                for k in range(top_k):
                    cp = pltpu.make_async_copy(rows_v.at[b], o_hbm.at[idx_v.at[k, w]], sems[b])
                    cp.start()
                    copies.append(cp)
            for cp in copies:
                cp.wait()

    return scatter(rows, slot_win)


def _sc_gather_rows(table, slot_flat):
    d = table.shape[1]
    m = slot_flat.shape[0]
    n_win, slot_win = _sc_windows(slot_flat)

    @pl.kernel(out_type=jax.ShapeDtypeStruct((m, d), table.dtype), mesh=_sc_mesh(),
               scratch_types=_sc_scratch(n_win, d, table.dtype))
    def gather(t_hbm, i_hbm, o_hbm, idx_v, rows_v, sem0, sem1):
        wid = lax.axis_index("subcore") * SC_CORES + lax.axis_index("core")
        first = wid * n_win
        pltpu.sync_copy(i_hbm.at[pl.ds(first, n_win)], idx_v)
        sems = (sem0, sem1)

        def fetch(c, b):
            return pltpu.make_async_copy(t_hbm.at[idx_v.at[c]], rows_v.at[b], sems[b])

        def flush(c, b):
            pltpu.sync_copy(rows_v.at[b], o_hbm.at[pl.ds((first + c) * SC_WINDOW, SC_WINDOW)])

        fetch(0, 0).start()

        @pl.loop(0, n_win // 2)
        def _(pair):
            c = pair * 2
            fetch(c + 1, 1).start()
            fetch(c, 0).wait()
            flush(c, 0)

            @pl.when(c + 2 < n_win)
            def _():
                fetch(c + 2, 0).start()

            fetch(c + 1, 1).wait()
            flush(c + 1, 1)

    return gather(table, slot_win)


EXPERT_BLOCK = 512
BLOCKS_PER_STEP = 2


def _experts_kernel(bexp_ref, bvalid_ref, nused_ref, x_ref, *rest):
    o_ref = rest[2 * BLOCKS_PER_STEP]
    for half in range(BLOCKS_PER_STEP):
        blk = pl.program_id(0) * BLOCKS_PER_STEP + half
        wgu_ref, wd_ref = rest[2 * half], rest[2 * half + 1]
        rows = pl.ds(half * EXPERT_BLOCK, EXPERT_BLOCK)

        @pl.when(blk < nused_ref[0])
        def _():
            lo, hi = _unpack_bf16_pairs(x_ref[rows, :])
            x = jnp.concatenate([lo, hi], axis=1)
            rowid = lax.broadcasted_iota(jnp.int32, (EXPERT_BLOCK, 1), 0)
            x = jnp.where(rowid < bvalid_ref[blk], x, 0.0).astype(BF16)
            gu = jnp.dot(x, wgu_ref[...], preferred_element_type=F32)
            hmid = (jax.nn.silu(gu[:, :D_EXPERT]) * gu[:, D_EXPERT:]).astype(BF16)
            o_ref[rows, :] = _pack_bf16_pairs(jnp.dot(hmid, wd_ref[...], preferred_element_type=F32))


def _experts(block_exp, block_valid, n_used, xs, wgu, wd):
    n_slots, dh = xs.shape
    d = 2 * dh
    step_rows = BLOCKS_PER_STEP * EXPERT_BLOCK
    n_steps = n_slots // step_rows
    last_step = lambda nu: (nu[0] - 1) // BLOCKS_PER_STEP
    rows = lambda i, be, bv, nu: (jnp.minimum(i, last_step(nu)), 0)

    def weights(half):
        return lambda i, be, bv, nu: (be[jnp.minimum(i, last_step(nu)) * BLOCKS_PER_STEP + half], 0, 0)

    w_specs = []
    for half in range(BLOCKS_PER_STEP):
        w_specs += [pl.BlockSpec((None, d, 2 * D_EXPERT), weights(half)),
                    pl.BlockSpec((None, D_EXPERT, d), weights(half))]
    grid_spec = pltpu.PrefetchScalarGridSpec(
        num_scalar_prefetch=3,
        grid=(n_steps,),
        in_specs=[pl.BlockSpec((step_rows, dh), rows)] + w_specs,
        out_specs=pl.BlockSpec((step_rows, dh), rows),
    )
    return pl.pallas_call(
        _experts_kernel,
        grid_spec=grid_spec,
        out_shape=jax.ShapeDtypeStruct((n_slots, dh), U32),
        compiler_params=_cparams(("arbitrary",)),
        name="experts",
    )(block_exp, block_valid, n_used, xs, *([wgu, wd] * BLOCKS_PER_STEP))


def _final_kernel(h_ref, y_ref, w_ref, wgu_ref, wd_ref, g_ref, b_ref, o_ref):
    h1 = h_ref[...]
    gu = jnp.dot(h1.astype(BF16), wgu_ref[...], preferred_element_type=F32)
    hmid = (jax.nn.silu(gu[:, :D_SHARED]) * gu[:, D_SHARED:]).astype(BF16)
    shared = jnp.dot(hmid, wd_ref[...], preferred_element_type=F32)
    w = w_ref[...]
    r_lo = r_hi = None
    for k in range(TOP_K):
        lo, hi = _unpack_bf16_pairs(y_ref[k])
        wk = w[:, k:k + 1]
        r_lo = wk * lo if r_lo is None else r_lo + wk * lo
        r_hi = wk * hi if r_hi is None else r_hi + wk * hi
    ffn = jnp.concatenate([r_lo, r_hi], axis=1) + shared
    o_ref[...] = _layer_norm_rows(ALPHA * h1 + ffn, g_ref[...], b_ref[...])


def _final(h1, ytok, wtok, wgu, wd, g, b):
    n, d = h1.shape
    tm = min(n, ROW_TILE)
    row = pl.BlockSpec((tm, d), lambda i: (i, 0))
    vec = pl.BlockSpec((1, d), lambda i: (0, 0))
    return pl.pallas_call(
        _final_kernel,
        grid=(n // tm,),
        in_specs=[row, pl.BlockSpec((TOP_K, tm, d // 2), lambda i: (0, i, 0)),
                  pl.BlockSpec((tm, TOP_K), lambda i: (i, 0)),
                  pl.BlockSpec((d, 2 * D_SHARED), lambda i: (0, 0)),
                  pl.BlockSpec((D_SHARED, d), lambda i: (0, 0)), vec, vec],
        out_specs=row,
        out_shape=jax.ShapeDtypeStruct((n, d), F32),
        compiler_params=_cparams(("parallel",)),
        name="shared_ln2",
    )(h1, ytok, wtok, wgu, wd, g, b)


def _segments(counts, n_assign):
    counts = counts.reshape(N_EXPERTS)
    padded = (counts + EXPERT_BLOCK - 1) // EXPERT_BLOCK * EXPERT_BLOCK
    seg_end = jnp.cumsum(padded)
    seg_start = seg_end - padded
    n_blocks = n_assign // EXPERT_BLOCK + N_EXPERTS
    block_start = jnp.arange(n_blocks, dtype=jnp.int32) * EXPERT_BLOCK
    block_exp = jnp.minimum(jnp.sum(seg_end[None, :] <= block_start[:, None], axis=1), N_EXPERTS - 1)
    mine = block_exp[:, None] == jnp.arange(N_EXPERTS)[None, :]
    rows_left = jnp.sum(jnp.where(mine, (counts + seg_start)[None, :], 0), axis=1) - block_start
    block_valid = jnp.clip(rows_left, 0, EXPERT_BLOCK)
    n_used = (seg_end[-1] // EXPERT_BLOCK).reshape(1)
    i32 = lambda v: v.astype(jnp.int32)
    return i32(seg_start), i32(block_exp), i32(block_valid), i32(n_used), n_blocks * EXPERT_BLOCK


def kernel(x, meta_tokens, emb_ln_g, emb_ln_b, w_in, b_gate, conv_w, conv_b, lru_w_a, lru_b_a, lru_w_i, lru_b_i, lru_lambda, lam_q1, lam_k1, lam_q2, lam_k2, subln_g, w_rec_proj, w_attn_proj, w_o, ln1_g, ln1_b, router_w, router_bias, w_e_gate, w_e_up, w_e_down, w_s_gate, w_s_up, w_s_down, ln2_g, ln2_b):
    nb, s, d = x.shape
    n = nb * s
    row = lambda v: v.reshape(1, -1).astype(F32)
    l = 0

    x2d = x.reshape(n, d)
    w_in_b = w_in[l].astype(BF16)
    eg, eb = row(emb_ln_g), row(emb_ln_b)
    proj = _ln_inproj(x2d, eg, eb, w_in_b)
    projm = _ln_inproj(meta_tokens.astype(F32), eg, eb, w_in_b)
    proj3 = proj.reshape(nb, s, IN_COLS)

    wai = jnp.concatenate([lru_w_a[l], lru_w_i[l]], axis=-1).astype(BF16)
    lru_args = (conv_w[l].astype(F32), row(conv_b[l]), wai, row(lru_b_a[l]), row(lru_b_i[l]),
                row(lru_lambda[l]))
    zeros8 = jnp.zeros((SUBLANES, D_RNN), F32)
    nct = D_RNN // LANES
    _, h_meta = _rglru(projm.reshape(1, N_META, IN_COLS), zeros8, jnp.zeros((nct, 1, LANES), F32),
                       *lru_args, first=True)
    ctx0 = projm[N_META - SUBLANES:, :D_RNN].astype(F32)
    rec, _ = _rglru(proj3, ctx0, jnp.broadcast_to(h_meta, (nct, nb, LANES)), *lru_args, first=False)

    pad = ((0, LANES - N_META), (0, 0))
    km = jnp.pad(projm[:, COL_K * D_ATTN:(COL_K + 1) * D_ATTN], pad)
    vm = jnp.pad(projm[:, COL_V * D_ATTN:(COL_V + 1) * D_ATTN], pad)
    att, (wgu_b, wd_b) = _diff_attn(
        proj3, km, vm, row(lam_q1[l]), row(lam_k1[l]), row(lam_q2[l]), row(lam_k2[l]), row(subln_g[l]),
        to_bf16=((w_e_gate[l], w_e_up[l]), (w_e_down[l],)))

    bg = b_gate[l].astype(F32)
    h1, h1p = _merge(x2d, rec.reshape(n, D_RNN), att.reshape(n, D_ATTN), proj,
                     bg[:D_MODEL].reshape(1, -1), bg[D_MODEL:].reshape(1, -1),
                     w_rec_proj[l].astype(BF16), w_attn_proj[l].astype(BF16), w_o[l].astype(BF16),
                     eg, eb, row(ln1_g[l]), row(ln1_b[l]))

    eidx, wts, rank, counts = _router(h1, router_w[l].T.astype(F32),
                                      router_bias[l].reshape(-1, 1).astype(F32))
    seg_start, block_exp, block_valid, n_used, n_slots = _segments(counts, TOP_K * n)
    slots = _slots(eidx, rank, seg_start.reshape(-1, 1))

    xs = _sc_scatter_rows(h1p, slots, n_slots)
    ys = _experts(block_exp, block_valid, n_used, xs, wgu_b, wd_b)
    ytok = _sc_gather_rows(ys, slots.reshape(TOP_K * n)).reshape(TOP_K, n, d // 2)

    wgu = jnp.concatenate([w_s_gate[l], w_s_up[l]], axis=-1).astype(BF16)
    out = _final(h1, ytok, wts.T, wgu, w_s_down[l].astype(BF16), row(ln2_g[l]), row(ln2_b[l]))
    return out.reshape(nb, s, d)
```

```python
import functools
import math

import jax
import jax.numpy as jnp
from jax import lax
from jax.experimental import pallas as pl
from jax.experimental.pallas import tpu as pltpu
from jax.experimental.pallas import tpu_sc as plsc

F32 = jnp.float32
BF16 = jnp.bfloat16
U32 = jnp.uint32

N_META = 16
D_MODEL = 1024
D_RNN = 1024
N_RNN_BLOCKS = 8
RNN_BLOCK = D_RNN // N_RNN_BLOCKS
CONV_WIDTH = 4
LRU_C = 8.0
N_HEADS = 8
HEAD_DIM = 64
V_DIM = 2 * HEAD_DIM
D_ATTN = N_HEADS * V_DIM
N_EXPERTS = 256
TOP_K = 8
N_GROUPS = 8
GROUP_SIZE = N_EXPERTS // N_GROUPS
TOPK_GROUPS = 4
D_EXPERT = 256
D_SHARED = 256
ROUTED_SCALE = 2.5
LN_EPS = 1e-5
DEPTH = 1
IN_COLS = 2 * D_RNN + 3 * D_ATTN + 2 * D_MODEL
ALPHA = (2.0 * DEPTH) ** 0.25
LAMBDA_INIT = 0.8 - 0.6 * math.exp(-0.3 * 0)

LANES = 128
SUBLANES = 8
VMEM_LIMIT = 56 * 1024 * 1024
NEG_BIG = -1e30
LOG2_E = math.log2(math.e)

IN_PROJ_TILE = (2048, 1024)
LRU_TIME_BLOCK = 128
ATTN_Q_BLOCK = 512
ROW_TILE = 512
SLOT_TILE = 2048

COL_XR, COL_YR, COL_Q, COL_K, COL_V, COL_GREC, COL_GATT = range(7)


def _cparams(sem):
    return pltpu.CompilerParams(dimension_semantics=sem, vmem_limit_bytes=VMEM_LIMIT)


def _layer_norm_rows(x, g, b):
    mu = jnp.mean(x, axis=-1, keepdims=True)
    xc = x - mu
    var = jnp.mean(xc * xc, axis=-1, keepdims=True)
    return xc * lax.rsqrt(var + LN_EPS) * g + b


def _ln_inproj_kernel(x_ref, g_ref, b_ref, w_ref, o_ref, xn_ref):
    @pl.when(pl.program_id(1) == 0)
    def _():
        xn_ref[...] = _layer_norm_rows(x_ref[...], g_ref[...], b_ref[...]).astype(BF16)

    o_ref[...] = jnp.dot(xn_ref[...], w_ref[...], preferred_element_type=F32).astype(o_ref.dtype)


def _ln_inproj(x2d, g, b, w_bf16):
    n, d = x2d.shape
    cols = w_bf16.shape[1]
    tm = min(n, IN_PROJ_TILE[0])
    tn = IN_PROJ_TILE[1]
    return pl.pallas_call(
        _ln_inproj_kernel,
        grid=(n // tm, cols // tn),
        in_specs=[
            pl.BlockSpec((tm, d), lambda i, j: (i, 0)),
            pl.BlockSpec((1, d), lambda i, j: (0, 0)),
            pl.BlockSpec((1, d), lambda i, j: (0, 0)),
            pl.BlockSpec((d, tn), lambda i, j: (0, j)),
        ],
        out_specs=pl.BlockSpec((tm, tn), lambda i, j: (i, j)),
        out_shape=jax.ShapeDtypeStruct((n, cols), BF16),
        scratch_shapes=[pltpu.VMEM((tm, d), BF16)],
        compiler_params=_cparams(("parallel", "arbitrary")),
        name="ln_inproj",
    )(x2d, g, b, w_bf16)


def _sigmoid_tanh(v):
    return 0.5 * jnp.tanh(0.5 * v) + 0.5


def _rglru_kernel(x_ref, y_ref, ctx0_ref, h0_ref, cw_ref, cb_ref, wai_ref, ba_ref, bi_ref,
                  lam_ref, o_ref, hl_ref, ctx_scr, a_scr, b_scr, hs_scr, h_scr, *, nb, tb, first):
    step = pl.program_id(0)
    nct = D_RNN // LANES

    @pl.when(step == 0)
    def _():
        for bb in range(nb):
            ctx_scr[bb] = ctx0_ref[...]
        h_scr[...] = h0_ref[...]

    lam = lam_ref[...]
    sp = jnp.maximum(-lam, 0.0) + jnp.log1p(jnp.exp(-jnp.abs(lam)))
    cw = cw_ref[...]
    cb = cb_ref[...]
    ba = ba_ref[...]
    bi = bi_ref[...]
    row = lax.broadcasted_iota(jnp.int32, (tb, 1), 0)

    def slab(bb, carry):
        xb = x_ref[bb].astype(F32)
        xcat = jnp.concatenate([ctx_scr[bb], xb], axis=0)
        u = cb + cw[CONV_WIDTH - 1:CONV_WIDTH, :] * xb
        for s in range(1, CONV_WIDTH):
            shifted = pltpu.roll(xcat, s, 0)[SUBLANES:, :]
            u = u + cw[CONV_WIDTH - 1 - s:CONV_WIDTH - s, :] * shifted
        ctx_scr[bb] = xb[tb - SUBLANES:, :]
        ub = u.astype(BF16)
        for n in range(N_RNN_BLOCKS):
            cs = slice(n * RNN_BLOCK, (n + 1) * RNN_BLOCK)
            pre = jnp.dot(ub[:, cs], wai_ref[n], preferred_element_type=F32)
            r = _sigmoid_tanh(pre[:, :RNN_BLOCK] + ba[:, cs])
            gi = _sigmoid_tanh(pre[:, RNN_BLOCK:] + bi[:, cs])
            log_a = -LRU_C * r * sp[:, cs]
            a = jnp.exp(log_a)
            mult = jnp.sqrt(1.0 - a * a)
            if first:
                mult = jnp.where((row == 0) & (step == 0), 1.0, mult)
            a_scr[n, pl.ds(bb, tb, stride=nb), :] = a
            b_scr[n, pl.ds(bb, tb, stride=nb), :] = mult * (gi * u[:, cs])
        return carry

    lax.fori_loop(0, nb, slab, 0)

    def tstep(t, h):
        base = pl.multiple_of(t * nb, nb)
        h = a_scr[:, pl.ds(base, nb), :] * h + b_scr[:, pl.ds(base, nb), :]
        hs_scr[:, pl.ds(base, nb), :] = h
        return h

    h_fin = lax.fori_loop(0, tb, tstep, h_scr[...], unroll=8)
    h_scr[...] = h_fin
    hl_ref[...] = h_fin

    def gate(bb, carry):
        hb = jnp.concatenate([hs_scr[n, pl.ds(bb, tb, stride=nb), :] for n in range(nct)], axis=1)
        o_ref[bb] = (hb * jax.nn.gelu(y_ref[bb].astype(F32))).astype(o_ref.dtype)
        return carry

    lax.fori_loop(0, nb, gate, 0)


def _rglru(proj3, ctx0, h0, cw, cb, wai, ba, bi, lam, *, first):
    nb, s, _ = proj3.shape
    tb = min(s, LRU_TIME_BLOCK)
    c = D_RNN
    nct = c // LANES
    kern = functools.partial(_rglru_kernel, nb=nb, tb=tb, first=first)
    full = lambda shape: pl.BlockSpec(shape, lambda t: (0,) * len(shape))
    return pl.pallas_call(
        kern,
        grid=(s // tb,),
        in_specs=[
            pl.BlockSpec((nb, tb, c), lambda t: (0, t, COL_XR)),
            pl.BlockSpec((nb, tb, c), lambda t: (0, t, COL_YR)),
            full((SUBLANES, c)), full((nct, nb, LANES)), full((CONV_WIDTH, c)), full((1, c)),
            full((N_RNN_BLOCKS, RNN_BLOCK, 2 * RNN_BLOCK)), full((1, c)), full((1, c)), full((1, c)),
        ],
        out_specs=[pl.BlockSpec((nb, tb, c), lambda t: (0, t, 0)), full((nct, nb, LANES))],
        out_shape=[jax.ShapeDtypeStruct((nb, s, c), BF16), jax.ShapeDtypeStruct((nct, nb, LANES), F32)],
        scratch_shapes=[
            pltpu.VMEM((nb, SUBLANES, c), F32),
            pltpu.VMEM((nct, nb * tb, LANES), F32),
            pltpu.VMEM((nct, nb * tb, LANES), F32),
            pltpu.VMEM((nct, nb * tb, LANES), F32),
            pltpu.VMEM((nct, nb, LANES), F32),
        ],
        compiler_params=_cparams(("arbitrary",)),
        name="rglru_first" if first else "rglru",
    )(proj3, proj3, ctx0, h0, cw, cb, wai, ba, bi, lam)


ATTN_Q_PER_STEP = 4


def _attn_schedule(nq):
    assert nq % ATTN_Q_PER_STEP == 0
    qids, slots, kblocks = [], [], []
    for j in range(nq // ATTN_Q_PER_STEP):
        ids = [2 * j, nq - 1 - 2 * j, 2 * j + 1, nq - 2 - 2 * j]
        items = [(sl, kb) for sl, q in enumerate(ids) for kb in range(q)] + list(enumerate(ids))
        qids += ids
        slots += [it[0] for it in items]
        kblocks += [it[1] for it in items]
    as_i32 = lambda v: jnp.asarray(v, jnp.int32)
    return as_i32(qids), as_i32(slots), as_i32(kblocks), 2 * (nq - 1)


def _attn_kernel(qid_ref, slot_ref, kb_ref, lq1_ref, lk1_ref, lq2_ref, lk2_ref, g_ref, q_ref, k_ref, v_ref,
                 km_ref, vm_ref, *rest, tq, cast_groups, n_full):
    n_in, n_out = sum(cast_groups), len(cast_groups)
    cast_in, o_ref, cast_out = rest[:n_in], rest[n_in], rest[n_in + 1:n_in + 1 + n_out]
    q2_scr, m_scr, acc_scr, s_scr = rest[n_in + 1 + n_out:]
    first = 0
    for dst, n_src in zip(cast_out, cast_groups):
        parts = [src[...].astype(BF16) for src in cast_in[first:first + n_src]]
        dst[...] = parts[0] if n_src == 1 else jnp.concatenate(parts, axis=-1)
        first += n_src

    step = pl.program_id(2)
    n_items = n_full + ATTN_Q_PER_STEP
    hq = tq // 2
    lane = lax.broadcasted_iota(jnp.int32, (1, V_DIM), 1)

    def qrows(sl):
        return pl.multiple_of(qid_ref[step * ATTN_Q_PER_STEP + sl] * tq, tq)

    for sl in range(ATTN_Q_PER_STEP):
        q = (q_ref[pl.ds(qrows(sl), tq), :].astype(F32) * (HEAD_DIM ** -0.5 * LOG2_E)).astype(BF16)
        zero = jnp.zeros_like(q)
        q_m1, q_m2 = jnp.where(lane < HEAD_DIM, q, zero), jnp.where(lane >= HEAD_DIM, q, zero)
        q2_scr[sl] = jnp.concatenate([q_m1[:hq], q_m2[:hq], q_m1[hq:], q_m2[hq:]], axis=0)

    def scores(sl, r0, kblk):
        return lax.dot_general(q2_scr[sl, r0:, :], kblk, (((1,), (1,)), ((), ())),
                               preferred_element_type=F32)

    def consume(sl, r0, s, vblk, mask, fresh):
        width = s.shape[1]
        vext = jnp.concatenate([vblk, jnp.ones_like(vblk)], axis=1)
        if mask is not None:
            s = jnp.where(mask, s, NEG_BIG)
        row_max = jnp.max(s, axis=1, keepdims=True)
        if fresh:
            m_next = jnp.broadcast_to(row_max, (s.shape[0], LANES))
        else:
            m_prev = m_scr[sl, r0:, :]
            m_next = jnp.maximum(m_prev, row_max)
        p = jnp.exp2(s - jnp.tile(m_next, (1, width // LANES)))
        pv = jnp.dot(p.astype(BF16), vext, preferred_element_type=F32)
        if fresh:
            acc_scr[sl, r0:, :] = pv
        else:
            corr = jnp.exp2(m_prev - m_next)
            acc_scr[sl, r0:, :] = jnp.tile(corr, (1, 2)) * acc_scr[sl, r0:, :] + pv
        m_scr[sl, r0:, :] = m_next

    mcol = lax.broadcasted_iota(jnp.int32, (2 * tq, LANES), 1)
    r_i = lax.broadcasted_iota(jnp.int32, (2 * tq, hq), 0)
    c_i = lax.broadcasted_iota(jnp.int32, (2 * tq, hq), 1)
    within = c_i <= (r_i & (hq - 1))
    first_half_mask = (r_i >= tq) | within
    second_half_mask = within[tq:, :]

    work = [(sl, 0, None, LANES, mcol < N_META) for sl in range(ATTN_Q_PER_STEP)]
    for t in range(n_items):
        sl, kb = slot_ref[step * n_items + t], kb_ref[step * n_items + t]
        if t < n_full:
            work.append((sl, 0, pl.ds(pl.multiple_of(kb * tq, tq), tq), tq, None))
        else:
            work.append((sl, 0, pl.ds(pl.multiple_of(kb * tq, hq), hq), hq, first_half_mask))
            work.append((sl, tq, pl.ds(pl.multiple_of(kb * tq + hq, hq), hq), hq, second_half_mask))

    def keys(rows):
        return km_ref[...] if rows is None else k_ref[rows, :]

    def values(rows):
        return vm_ref[...] if rows is None else v_ref[rows, :]

    def put_scores(t):
        sl, r0, rows, width, _ = work[t]
        s_scr[t % 2, r0:, :width] = scores(sl, r0, keys(rows))

    put_scores(0)
    for t, (sl, r0, rows, width, mask) in enumerate(work):
        if t + 1 < len(work):
            put_scores(t + 1)
        consume(sl, r0, s_scr[t % 2, r0:, :width], values(rows), mask, fresh=t < ATTN_Q_PER_STEP)

    lam = (jnp.exp(jnp.sum(lq1_ref[...] * lk1_ref[...], axis=1, keepdims=True))
           - jnp.exp(jnp.sum(lq2_ref[...] * lk2_ref[...], axis=1, keepdims=True)) + LAMBDA_INIT)
    for sl in range(ATTN_Q_PER_STEP):
        for half in range(2):
            a1 = acc_scr[sl, pl.ds((2 * half) * hq, hq), :]
            a2 = acc_scr[sl, pl.ds((2 * half + 1) * hq, hq), :]
            o = a1[:, :V_DIM] / a1[:, V_DIM:] - lam * (a2[:, :V_DIM] / a2[:, V_DIM:])
            o = o * lax.rsqrt(jnp.mean(o * o, axis=-1, keepdims=True) + LN_EPS) * g_ref[...]
            o_ref[pl.ds(qrows(sl) + half * hq, hq), :] = (o * (1.0 - LAMBDA_INIT)).astype(o_ref.dtype)


def _diff_attn(proj3, km, vm, lq1, lk1, lq2, lk2, g, to_bf16):
    nb, s, _ = proj3.shape
    tq = min(s // ATTN_Q_PER_STEP, ATTN_Q_BLOCK)
    nq = s // tq
    qids, slots, kblocks, n_full = _attn_schedule(nq)
    n_steps = nq // ATTN_Q_PER_STEP
    steps = nb * N_HEADS * n_steps
    hb = D_ATTN // V_DIM
    small = lambda shape: pl.BlockSpec(shape, lambda b, h, j, *_: (0,) * len(shape))
    seq = lambda col: pl.BlockSpec((None, s, V_DIM), lambda b, h, j, *_: (b, 0, col * hb + h))
    meta = pl.BlockSpec((LANES, V_DIM), lambda b, h, j, *_: (0, h))
    chunk = lambda w: w.reshape(steps, w.size // w.shape[-1] // steps, w.shape[-1])
    chunk_spec = lambda shape: pl.BlockSpec((None,) + shape[1:],
                                            lambda b, h, j, *_: ((b * N_HEADS + h) * n_steps + j, 0, 0))
    chunked = [chunk(w) for group in to_bf16 for w in group]
    joined = [chunk(group[0]).shape[:2] + (sum(w.shape[-1] for w in group),) for group in to_bf16]
    grid_spec = pltpu.PrefetchScalarGridSpec(
        num_scalar_prefetch=3,
        grid=(nb, N_HEADS, n_steps),
        in_specs=[small((1, HEAD_DIM)), small((1, HEAD_DIM)), small((1, HEAD_DIM)), small((1, HEAD_DIM)),
                  small((1, V_DIM)), seq(COL_Q), seq(COL_K), seq(COL_V), meta, meta]
                 + [chunk_spec(c.shape) for c in chunked],
        out_specs=[pl.BlockSpec((None, s, V_DIM), lambda b, h, j, *_: (b, 0, h))]
                  + [chunk_spec(shape) for shape in joined],
        scratch_shapes=[
            pltpu.VMEM((ATTN_Q_PER_STEP, 2 * tq, V_DIM), BF16),
            pltpu.VMEM((ATTN_Q_PER_STEP, 2 * tq, LANES), F32),
            pltpu.VMEM((ATTN_Q_PER_STEP, 2 * tq, 2 * V_DIM), F32),
            pltpu.VMEM((2, 2 * tq, tq), F32),
        ],
    )
    outs = pl.pallas_call(
        functools.partial(_attn_kernel, tq=tq, cast_groups=tuple(len(group) for group in to_bf16),
                          n_full=n_full),
        grid_spec=grid_spec,
        out_shape=[jax.ShapeDtypeStruct((nb, s, D_ATTN), BF16)]
                  + [jax.ShapeDtypeStruct(shape, BF16) for shape in joined],
        compiler_params=_cparams(("parallel", "parallel", "arbitrary")),
        name="diff_attn",
    )(qids, slots, kblocks, lq1, lk1, lq2, lk2, g, proj3, proj3, proj3, km, vm, *chunked)
    return outs[0], [o.reshape(group[0].shape[:-1] + (o.shape[-1],)) for o, group in zip(outs[1:], to_bf16)]


def _merge_kernel(x_ref, rec_ref, att_ref, gr_ref, ga_ref, bgr_ref, bga_ref, wr_ref, wa_ref, wo_ref,
                  eg_ref, eb_ref, g1_ref, b1_ref, h_ref, hp_ref):
    r = jnp.dot(rec_ref[...], wr_ref[...], preferred_element_type=F32)
    a = jnp.dot(att_ref[...], wa_ref[...], preferred_element_type=F32)
    g_rec = jax.nn.sigmoid(gr_ref[...].astype(F32) + bgr_ref[...])
    g_att = jax.nn.sigmoid(ga_ref[...].astype(F32) + bga_ref[...])
    mix = jnp.dot((g_rec * r + g_att * a).astype(BF16), wo_ref[...], preferred_element_type=F32)
    h0 = _layer_norm_rows(x_ref[...], eg_ref[...], eb_ref[...])
    h1 = _layer_norm_rows(ALPHA * h0 + mix, g1_ref[...], b1_ref[...])
    h_ref[...] = h1
    hp_ref[...] = _pack_bf16_pairs(h1)


def _pack_bf16_pairs(v):
    half = v.shape[1] // 2
    lo = lax.bitcast_convert_type(v[:, :half].astype(BF16).astype(F32), U32) >> 16
    hi = lax.bitcast_convert_type(v[:, half:].astype(BF16).astype(F32), U32) & jnp.uint32(0xFFFF0000)
    return lo | hi


def _unpack_bf16_pairs(p):
    lo = lax.bitcast_convert_type(p << 16, F32)
    hi = lax.bitcast_convert_type(p & jnp.uint32(0xFFFF0000), F32)
    return lo, hi


def _merge(x2d, rec2d, att2d, proj2d, bgr, bga, wr, wa, wo, eg, eb, g1, b1):
    n, d = x2d.shape
    tm = min(n, ROW_TILE)
    row = lambda col: pl.BlockSpec((tm, d), lambda i: (i, col))
    vec = pl.BlockSpec((1, d), lambda i: (0, 0))
    mat = pl.BlockSpec((d, d), lambda i: (0, 0))
    return pl.pallas_call(
        _merge_kernel,
        grid=(n // tm,),
        in_specs=[row(0), row(0), row(0), row(COL_GREC), row(COL_GATT), vec, vec, mat, mat, mat,
                  vec, vec, vec, vec],
        out_specs=[row(0), pl.BlockSpec((tm, d // 2), lambda i: (i, 0))],
        out_shape=[jax.ShapeDtypeStruct((n, d), F32), jax.ShapeDtypeStruct((n, d // 2), U32)],
        compiler_params=_cparams(("parallel",)),
        name="merge_ln1",
    )(x2d, rec2d, att2d, proj2d, proj2d, bgr, bga, wr, wa, wo, eg, eb, g1, b1)


def _split_bf16(v):
    hi = lax.bitcast_convert_type(lax.bitcast_convert_type(v, U32) & jnp.uint32(0xFFFF0000), F32)
    return hi.astype(BF16), (v - hi).astype(BF16)


def _router_kernel(h_ref, wt_hi_ref, wt_lo_ref, bias_ref, e_ref, w_ref, r_ref, c_ref, cnt_scr):
    tm = h_ref.shape[0]

    @pl.when(pl.program_id(0) == 0)
    def _():
        cnt_scr[...] = jnp.zeros(cnt_scr.shape, F32)

    h_hi, h_lo = _split_bf16(h_ref[...])
    nt = lambda a, b: lax.dot_general(a, b, (((1,), (1,)), ((), ())), preferred_element_type=F32)
    logits = nt(wt_hi_ref[...], h_hi) + nt(wt_hi_ref[...], h_lo) + nt(wt_lo_ref[...], h_hi)
    scores = jax.nn.sigmoid(logits)
    sel = scores + bias_ref[...]
    grp = sel.reshape(N_GROUPS, GROUP_SIZE, tm)
    gi = lax.broadcasted_iota(jnp.int32, grp.shape, 1)
    m1 = jnp.max(grp, axis=1, keepdims=True)
    first = jnp.min(jnp.where(grp == m1, gi, GROUP_SIZE), axis=1, keepdims=True)
    m2 = jnp.max(jnp.where(gi == first, -jnp.inf, grp), axis=1, keepdims=True)
    gscore = (m1 + m2).reshape(N_GROUPS, tm)

    gidx = lax.broadcasted_iota(jnp.int32, (N_GROUPS, tm), 0)
    keep = jnp.zeros((N_GROUPS, tm), jnp.bool_)
    cur = gscore
    for _ in range(TOPK_GROUPS):
        mx = jnp.max(cur, axis=0, keepdims=True)
        pick = gidx == jnp.min(jnp.where(cur == mx, gidx, N_GROUPS), axis=0, keepdims=True)
        keep = keep | pick
        cur = jnp.where(pick, -jnp.inf, cur)

    cur = jnp.where(keep[:, None, :], grp, -jnp.inf).reshape(N_EXPERTS, tm)
    eidx = lax.broadcasted_iota(jnp.int32, (N_EXPERTS, tm), 0)
    picked_e, picked_w, picks = [], [], []
    for _ in range(TOP_K):
        mx = jnp.max(cur, axis=0, keepdims=True)
        ei = jnp.min(jnp.where(cur == mx, eidx, N_EXPERTS), axis=0, keepdims=True)
        pick = eidx == ei
        picked_e.append(ei)
        picked_w.append(jnp.sum(jnp.where(pick, scores, 0.0), axis=0, keepdims=True))
        picks.append(pick)
        cur = jnp.where(pick, -jnp.inf, cur)
    w = jnp.concatenate(picked_w, axis=0)
    e_ref[...] = jnp.concatenate(picked_e, axis=0)
    w_ref[...] = w / jnp.sum(w, axis=0, keepdims=True) * ROUTED_SCALE

    member = functools.reduce(jnp.logical_or, picks)
    t_row = lax.broadcasted_iota(jnp.int32, (tm, tm), 0)
    t_col = lax.broadcasted_iota(jnp.int32, (tm, tm), 1)
    earlier = jnp.where(t_row < t_col, 1.0, 0.0).astype(BF16)
    member_f = jnp.where(member, 1.0, 0.0)
    before = cnt_scr[...] + jnp.dot(member_f.astype(BF16), earlier, preferred_element_type=F32)
    ranks = [jnp.sum(jnp.where(p, before, 0.0), axis=0, keepdims=True) for p in picks]
    r_ref[...] = jnp.concatenate(ranks, axis=0).astype(jnp.int32)
    total = cnt_scr[...] + jnp.sum(member_f, axis=1, keepdims=True)
    cnt_scr[...] = total
    c_ref[...] = total.astype(jnp.int32)


def _router(h1, router_wt, bias_col):
    n, d = h1.shape
    tm = min(n, ROW_TILE)
    tok = pl.BlockSpec((TOP_K, tm), lambda i: (0, i))
    wt_hi, wt_lo = _split_bf16(router_wt)
    return pl.pallas_call(
        _router_kernel,
        grid=(n // tm,),
        in_specs=[
            pl.BlockSpec((tm, d), lambda i: (i, 0)),
            pl.BlockSpec((N_EXPERTS, d), lambda i: (0, 0)),
            pl.BlockSpec((N_EXPERTS, d), lambda i: (0, 0)),
            pl.BlockSpec((N_EXPERTS, 1), lambda i: (0, 0)),
        ],
        out_specs=[tok, tok, tok, pl.BlockSpec((N_EXPERTS, 1), lambda i: (0, 0))],
        out_shape=[jax.ShapeDtypeStruct((TOP_K, n), jnp.int32), jax.ShapeDtypeStruct((TOP_K, n), F32),
                   jax.ShapeDtypeStruct((TOP_K, n), jnp.int32),
                   jax.ShapeDtypeStruct((N_EXPERTS, 1), jnp.int32)],
        scratch_shapes=[pltpu.VMEM((N_EXPERTS, 1), F32)],
        compiler_params=_cparams(("arbitrary",)),
        name="router",
    )(h1, wt_hi, wt_lo, bias_col)


def _slots_kernel(e_ref, r_ref, ps_ref, o_ref):
    tm = e_ref.shape[1]
    eidx = lax.broadcasted_iota(jnp.int32, (N_EXPERTS, tm), 0)
    e = e_ref[...]
    ps = ps_ref[...]
    base = [jnp.sum(jnp.where(eidx == e[k:k + 1, :], ps, 0), axis=0, keepdims=True) for k in range(TOP_K)]
    o_ref[...] = jnp.concatenate(base, axis=0) + r_ref[...]


def _slots(eidx, rank, pstart_col):
    k, n = eidx.shape
    tm = min(n, SLOT_TILE)
    tok = pl.BlockSpec((k, tm), lambda i: (0, i))
    return pl.pallas_call(
        _slots_kernel,
        grid=(n // tm,),
        in_specs=[tok, tok, pl.BlockSpec((N_EXPERTS, 1), lambda i: (0, 0))],
        out_specs=tok,
        out_shape=jax.ShapeDtypeStruct((k, n), jnp.int32),
        compiler_params=_cparams(("parallel",)),
        name="slots",
    )(eidx, rank, pstart_col)


SC_CORES = 2
SC_SUBCORES = 16
SC_WINDOW = 64


def _sc_mesh():
    return plsc.VectorSubcoreMesh(core_axis_name="core", subcore_axis_name="subcore")


def _sc_windows(slot_flat):
    m = slot_flat.shape[0]
    n_win = m // SC_WINDOW // (SC_CORES * SC_SUBCORES)
    assert n_win * SC_WINDOW * SC_CORES * SC_SUBCORES == m and n_win % 2 == 0
    return n_win, slot_flat.reshape(m // SC_WINDOW, SC_WINDOW)


def _sc_scratch(n_win, d, dtype):
    return [pltpu.VMEM((n_win, SC_WINDOW), jnp.int32), pltpu.VMEM((2, SC_WINDOW, d), dtype),
            pltpu.SemaphoreType.DMA, pltpu.SemaphoreType.DMA]


def _sc_scatter_rows(rows, slots, n_slots):
    n, d = rows.shape
    top_k = slots.shape[0]
    n_win = n // SC_WINDOW // (SC_CORES * SC_SUBCORES)
    assert n_win * SC_WINDOW * SC_CORES * SC_SUBCORES == n and n_win % 2 == 0
    slot_win = slots.reshape(top_k, n // SC_WINDOW, SC_WINDOW)

    @pl.kernel(out_type=jax.ShapeDtypeStruct((n_slots, d), rows.dtype), mesh=_sc_mesh(),
               scratch_types=[pltpu.VMEM((top_k, n_win, SC_WINDOW), jnp.int32),
                              pltpu.VMEM((2, SC_WINDOW, d), rows.dtype),
                              pltpu.SemaphoreType.DMA, pltpu.SemaphoreType.DMA])
    def scatter(x_hbm, i_hbm, o_hbm, idx_v, rows_v, sem0, sem1):
        wid = lax.axis_index("subcore") * SC_CORES + lax.axis_index("core")
        first = wid * n_win
        for k in range(top_k):
            pltpu.sync_copy(i_hbm.at[k, pl.ds(first, n_win)], idx_v.at[k])
        sems = (sem0, sem1)

        @pl.loop(0, n_win // 2)
        def _(pair):
            copies = []
            for b in range(2):
                w = pair * 2 + b
                pltpu.sync_copy(x_hbm.at[pl.ds((first + w) * SC_WINDOW, SC_WINDOW)], rows_v.at[b])
                for k in range(top_k):
                    cp = pltpu.make_async_copy(rows_v.at[b], o_hbm.at[idx_v.at[k, w]], sems[b])
                    cp.start()
                    copies.append(cp)
            for cp in copies:
                cp.wait()

    return scatter(rows, slot_win)


def _sc_gather_rows(table, slot_flat):
    d = table.shape[1]
    m = slot_flat.shape[0]
    n_win, slot_win = _sc_windows(slot_flat)

    @pl.kernel(out_type=jax.ShapeDtypeStruct((m, d), table.dtype), mesh=_sc_mesh(),
               scratch_types=_sc_scratch(n_win, d, table.dtype))
    def gather(t_hbm, i_hbm, o_hbm, idx_v, rows_v, sem0, sem1):
        wid = lax.axis_index("subcore") * SC_CORES + lax.axis_index("core")
        first = wid * n_win
        pltpu.sync_copy(i_hbm.at[pl.ds(first, n_win)], idx_v)
        sems = (sem0, sem1)

        def fetch(c, b):
            return pltpu.make_async_copy(t_hbm.at[idx_v.at[c]], rows_v.at[b], sems[b])

        def flush(c, b):
            pltpu.sync_copy(rows_v.at[b], o_hbm.at[pl.ds((first + c) * SC_WINDOW, SC_WINDOW)])

        fetch(0, 0).start()

        @pl.loop(0, n_win // 2)
        def _(pair):
            c = pair * 2
            fetch(c + 1, 1).start()
            fetch(c, 0).wait()
            flush(c, 0)

            @pl.when(c + 2 < n_win)
            def _():
                fetch(c + 2, 0).start()

            fetch(c + 1, 1).wait()
            flush(c + 1, 1)

    return gather(table, slot_win)


EXPERT_BLOCK = 512
BLOCKS_PER_STEP = 2


def _experts_kernel(bset_ref, bvalid_ref, nused_ref, set0_ref, set1_ref, x_ref, wgu0_ref, wd0_ref,
                    wgu1_ref, wd1_ref, o_ref):
    for half in range(BLOCKS_PER_STEP):
        blk = pl.program_id(0) * BLOCKS_PER_STEP + half
        rows = pl.ds(half * EXPERT_BLOCK, EXPERT_BLOCK)

        @pl.when(blk < nused_ref[0])
        def _():
            in_set0 = bset_ref[blk] == 0
            wgu = jnp.where(in_set0, wgu0_ref[...], wgu1_ref[...])
            wd = jnp.where(in_set0, wd0_ref[...], wd1_ref[...])
            lo, hi = _unpack_bf16_pairs(x_ref[rows, :])
            x = jnp.concatenate([lo, hi], axis=1)
            rowid = lax.broadcasted_iota(jnp.int32, (EXPERT_BLOCK, 1), 0)
            x = jnp.where(rowid < bvalid_ref[blk], x, 0.0).astype(BF16)
            gu = jnp.dot(x, wgu, preferred_element_type=F32)
            hmid = (jax.nn.silu(gu[:, :D_EXPERT]) * gu[:, D_EXPERT:]).astype(BF16)
            o_ref[rows, :] = _pack_bf16_pairs(jnp.dot(hmid, wd, preferred_element_type=F32))


def _experts(block_set, block_valid, n_used, step_set0, step_set1, xs, wgu, wd):
    n_slots, dh = xs.shape
    d = 2 * dh
    step_rows = BLOCKS_PER_STEP * EXPERT_BLOCK
    n_steps = n_slots // step_rows
    last_step = lambda nu: (nu[0] - 1) // BLOCKS_PER_STEP
    rows = lambda i, bs, bv, nu, s0, s1: (jnp.minimum(i, last_step(nu)), 0)
    set0 = lambda i, bs, bv, nu, s0, s1: (s0[jnp.minimum(i, last_step(nu))], 0, 0)
    set1 = lambda i, bs, bv, nu, s0, s1: (s1[jnp.minimum(i, last_step(nu))], 0, 0)
    grid_spec = pltpu.PrefetchScalarGridSpec(
        num_scalar_prefetch=5,
        grid=(n_steps,),
        in_specs=[pl.BlockSpec((step_rows, dh), rows),
                  pl.BlockSpec((None, d, 2 * D_EXPERT), set0), pl.BlockSpec((None, D_EXPERT, d), set0),
                  pl.BlockSpec((None, d, 2 * D_EXPERT), set1), pl.BlockSpec((None, D_EXPERT, d), set1)],
        out_specs=pl.BlockSpec((step_rows, dh), rows),
    )
    return pl.pallas_call(
        _experts_kernel,
        grid_spec=grid_spec,
        out_shape=jax.ShapeDtypeStruct((n_slots, dh), U32),
        compiler_params=_cparams(("arbitrary",)),
        name="experts",
    )(block_set, block_valid, n_used, step_set0, step_set1, xs, wgu, wd, wgu, wd)


def _final_kernel(h_ref, y_ref, w_ref, wgu_ref, wd_ref, g_ref, b_ref, o_ref):
    h1 = h_ref[...]
    gu = jnp.dot(h1.astype(BF16), wgu_ref[...], preferred_element_type=F32)
    hmid = (jax.nn.silu(gu[:, :D_SHARED]) * gu[:, D_SHARED:]).astype(BF16)
    shared = jnp.dot(hmid, wd_ref[...], preferred_element_type=F32)
    w = w_ref[...]
    r_lo = r_hi = None
    for k in range(TOP_K):
        lo, hi = _unpack_bf16_pairs(y_ref[k])
        wk = w[:, k:k + 1]
        r_lo = wk * lo if r_lo is None else r_lo + wk * lo
        r_hi = wk * hi if r_hi is None else r_hi + wk * hi
    ffn = jnp.concatenate([r_lo, r_hi], axis=1) + shared
    o_ref[...] = _layer_norm_rows(ALPHA * h1 + ffn, g_ref[...], b_ref[...])


def _final(h1, ytok, wtok, wgu, wd, g, b):
    n, d = h1.shape
    tm = min(n, ROW_TILE)
    row = pl.BlockSpec((tm, d), lambda i: (i, 0))
    vec = pl.BlockSpec((1, d), lambda i: (0, 0))
    return pl.pallas_call(
        _final_kernel,
        grid=(n // tm,),
        in_specs=[row, pl.BlockSpec((TOP_K, tm, d // 2), lambda i: (0, i, 0)),
                  pl.BlockSpec((tm, TOP_K), lambda i: (i, 0)),
                  pl.BlockSpec((d, 2 * D_SHARED), lambda i: (0, 0)),
                  pl.BlockSpec((D_SHARED, d), lambda i: (0, 0)), vec, vec],
        out_specs=row,
        out_shape=jax.ShapeDtypeStruct((n, d), F32),
        compiler_params=_cparams(("parallel",)),
        name="shared_ln2",
    )(h1, ytok, wtok, wgu, wd, g, b)


def _segments(counts, n_assign):
    counts = counts.reshape(N_EXPERTS)
    padded = (counts + EXPERT_BLOCK - 1) // EXPERT_BLOCK * EXPERT_BLOCK
    seg_end = jnp.cumsum(padded)
    seg_start = seg_end - padded
    n_blocks = n_assign // EXPERT_BLOCK + N_EXPERTS
    block_start = jnp.arange(n_blocks, dtype=jnp.int32) * EXPERT_BLOCK
    block_exp = jnp.minimum(jnp.sum(seg_end[None, :] <= block_start[:, None], axis=1), N_EXPERTS - 1)
    mine = block_exp[:, None] == jnp.arange(N_EXPERTS)[None, :]
    rows_left = jnp.sum(jnp.where(mine, (counts + seg_start)[None, :], 0), axis=1) - block_start
    block_valid = jnp.clip(rows_left, 0, EXPERT_BLOCK)
    n_used = (seg_end[-1] // EXPERT_BLOCK).reshape(1)
    in_use = block_start < seg_end[-1]
    new_expert = in_use & (block_exp != jnp.concatenate([jnp.full((1,), -1, block_exp.dtype), block_exp[:-1]]))
    block_set = (jnp.cumsum(new_expert) - 1) % 2
    latest = lambda s: jnp.maximum(lax.cummax(jnp.where(in_use & (block_set == s), block_exp, -1)), 0)
    step_last = slice(BLOCKS_PER_STEP - 1, None, BLOCKS_PER_STEP)
    i32 = lambda v: v.astype(jnp.int32)
    return (i32(seg_start), i32(block_set), i32(block_valid), i32(n_used), i32(latest(0)[step_last]),
            i32(latest(1)[step_last]), n_blocks * EXPERT_BLOCK)


def kernel(x, meta_tokens, emb_ln_g, emb_ln_b, w_in, b_gate, conv_w, conv_b, lru_w_a, lru_b_a, lru_w_i, lru_b_i, lru_lambda, lam_q1, lam_k1, lam_q2, lam_k2, subln_g, w_rec_proj, w_attn_proj, w_o, ln1_g, ln1_b, router_w, router_bias, w_e_gate, w_e_up, w_e_down, w_s_gate, w_s_up, w_s_down, ln2_g, ln2_b):
    nb, s, d = x.shape
    n = nb * s
    row = lambda v: v.reshape(1, -1).astype(F32)
    l = 0

    x2d = x.reshape(n, d)
    w_in_b = w_in[l].astype(BF16)
    eg, eb = row(emb_ln_g), row(emb_ln_b)
    proj = _ln_inproj(x2d, eg, eb, w_in_b)
    projm = _ln_inproj(meta_tokens.astype(F32), eg, eb, w_in_b)
    proj3 = proj.reshape(nb, s, IN_COLS)

    wai = jnp.concatenate([lru_w_a[l], lru_w_i[l]], axis=-1).astype(BF16)
    lru_args = (conv_w[l].astype(F32), row(conv_b[l]), wai, row(lru_b_a[l]), row(lru_b_i[l]),
                row(lru_lambda[l]))
    zeros8 = jnp.zeros((SUBLANES, D_RNN), F32)
    nct = D_RNN // LANES
    _, h_meta = _rglru(projm.reshape(1, N_META, IN_COLS), zeros8, jnp.zeros((nct, 1, LANES), F32),
                       *lru_args, first=True)
    ctx0 = projm[N_META - SUBLANES:, :D_RNN].astype(F32)
    rec, _ = _rglru(proj3, ctx0, jnp.broadcast_to(h_meta, (nct, nb, LANES)), *lru_args, first=False)

    pad = ((0, LANES - N_META), (0, 0))
    km = jnp.pad(projm[:, COL_K * D_ATTN:(COL_K + 1) * D_ATTN], pad)
    vm = jnp.pad(projm[:, COL_V * D_ATTN:(COL_V + 1) * D_ATTN], pad)
    att, (wgu_b, wd_b) = _diff_attn(
        proj3, km, vm, row(lam_q1[l]), row(lam_k1[l]), row(lam_q2[l]), row(lam_k2[l]), row(subln_g[l]),
        to_bf16=((w_e_gate[l], w_e_up[l]), (w_e_down[l],)))

    bg = b_gate[l].astype(F32)
    h1, h1p = _merge(x2d, rec.reshape(n, D_RNN), att.reshape(n, D_ATTN), proj,
                     bg[:D_MODEL].reshape(1, -1), bg[D_MODEL:].reshape(1, -1),
                     w_rec_proj[l].astype(BF16), w_attn_proj[l].astype(BF16), w_o[l].astype(BF16),
                     eg, eb, row(ln1_g[l]), row(ln1_b[l]))

    eidx, wts, rank, counts = _router(h1, router_w[l].T.astype(F32),
                                      router_bias[l].reshape(-1, 1).astype(F32))
    seg_start, block_set, block_valid, n_used, step_set0, step_set1, n_slots = _segments(counts, TOP_K * n)
    slots = _slots(eidx, rank, seg_start.reshape(-1, 1))

    xs = _sc_scatter_rows(h1p, slots, n_slots)
    ys = _experts(block_set, block_valid, n_used, step_set0, step_set1, xs, wgu_b, wd_b)
    ytok = _sc_gather_rows(ys, slots.reshape(TOP_K * n)).reshape(TOP_K, n, d // 2)

    wgu = jnp.concatenate([w_s_gate[l], w_s_up[l]], axis=-1).astype(BF16)
    out = _final(h1, ytok, wts.T, wgu, w_s_down[l].astype(BF16), row(ln2_g[l]), row(ln2_b[l]))
    return out.reshape(nb, s, d)
```

```python
import functools
import math

import jax
import jax.numpy as jnp
from jax import lax
from jax.experimental import pallas as pl
from jax.experimental.pallas import tpu as pltpu
from jax.experimental.pallas import tpu_sc as plsc

F32 = jnp.float32
BF16 = jnp.bfloat16
U32 = jnp.uint32

N_META = 16
D_MODEL = 1024
D_RNN = 1024
N_RNN_BLOCKS = 8
RNN_BLOCK = D_RNN // N_RNN_BLOCKS
CONV_WIDTH = 4
LRU_C = 8.0
N_HEADS = 8
HEAD_DIM = 64
V_DIM = 2 * HEAD_DIM
D_ATTN = N_HEADS * V_DIM
N_EXPERTS = 256
TOP_K = 8
N_GROUPS = 8
GROUP_SIZE = N_EXPERTS // N_GROUPS
TOPK_GROUPS = 4
D_EXPERT = 256
D_SHARED = 256
ROUTED_SCALE = 2.5
LN_EPS = 1e-5
DEPTH = 1
IN_COLS = 2 * D_RNN + 3 * D_ATTN + 2 * D_MODEL
ALPHA = (2.0 * DEPTH) ** 0.25
LAMBDA_INIT = 0.8 - 0.6 * math.exp(-0.3 * 0)

LANES = 128
SUBLANES = 8
VMEM_LIMIT = 56 * 1024 * 1024
NEG_BIG = -1e30
LOG2_E = math.log2(math.e)

IN_PROJ_TILE = (2048, 1792)
LRU_TIME_BLOCK = 128
ATTN_Q_BLOCK = 512
ROW_TILE = 512
SLOT_TILE = 2048

COL_XR, COL_YR, COL_Q, COL_K, COL_V, COL_GREC, COL_GATT = range(7)


def _cparams(sem):
    return pltpu.CompilerParams(dimension_semantics=sem, vmem_limit_bytes=VMEM_LIMIT)


def _layer_norm_rows(x, g, b):
    mu = jnp.mean(x, axis=-1, keepdims=True)
    xc = x - mu
    var = jnp.mean(xc * xc, axis=-1, keepdims=True)
    return xc * lax.rsqrt(var + LN_EPS) * g + b


def _ln_inproj_kernel(x_ref, g_ref, b_ref, w_ref, o_ref, xn_ref):
    @pl.when(pl.program_id(1) == 0)
    def _():
        xn_ref[...] = _layer_norm_rows(x_ref[...], g_ref[...], b_ref[...]).astype(BF16)

    o_ref[...] = jnp.dot(xn_ref[...], w_ref[...], preferred_element_type=F32).astype(o_ref.dtype)


def _ln_inproj(x2d, g, b, w_bf16):
    n, d = x2d.shape
    cols = w_bf16.shape[1]
    tm = min(n, IN_PROJ_TILE[0])
    tn = IN_PROJ_TILE[1]
    return pl.pallas_call(
        _ln_inproj_kernel,
        grid=(n // tm, cols // tn),
        in_specs=[
            pl.BlockSpec((tm, d), lambda i, j: (i, 0)),
            pl.BlockSpec((1, d), lambda i, j: (0, 0)),
            pl.BlockSpec((1, d), lambda i, j: (0, 0)),
            pl.BlockSpec((d, tn), lambda i, j: (0, j)),
        ],
        out_specs=pl.BlockSpec((tm, tn), lambda i, j: (i, j)),
        out_shape=jax.ShapeDtypeStruct((n, cols), BF16),
        scratch_shapes=[pltpu.VMEM((tm, d), BF16)],
        compiler_params=_cparams(("parallel", "arbitrary")),
        name="ln_inproj",
    )(x2d, g, b, w_bf16)


def _sigmoid_tanh(v):
    return 0.5 * jnp.tanh(0.5 * v) + 0.5


def _rglru_kernel(x_ref, y_ref, ctx0_ref, h0_ref, cw_ref, cb_ref, wai_ref, ba_ref, bi_ref,
                  lam_ref, o_ref, hl_ref, ctx_scr, a_scr, b_scr, hs_scr, h_scr, *, nb, tb, first):
    step = pl.program_id(0)
    nct = D_RNN // LANES

    @pl.when(step == 0)
    def _():
        for bb in range(nb):
            ctx_scr[bb] = ctx0_ref[...]
        h_scr[...] = h0_ref[...]

    lam = lam_ref[...]
    sp = jnp.maximum(-lam, 0.0) + jnp.log1p(jnp.exp(-jnp.abs(lam)))
    cw = cw_ref[...]
    cb = cb_ref[...]
    ba = ba_ref[...]
    bi = bi_ref[...]
    row = lax.broadcasted_iota(jnp.int32, (tb, 1), 0)

    def slab(bb, carry):
        xb = x_ref[bb].astype(F32)
        xcat = jnp.concatenate([ctx_scr[bb], xb], axis=0)
        u = cb + cw[CONV_WIDTH - 1:CONV_WIDTH, :] * xb
        for s in range(1, CONV_WIDTH):
            shifted = pltpu.roll(xcat, s, 0)[SUBLANES:, :]
            u = u + cw[CONV_WIDTH - 1 - s:CONV_WIDTH - s, :] * shifted
        ctx_scr[bb] = xb[tb - SUBLANES:, :]
        ub = u.astype(BF16)
        for n in range(N_RNN_BLOCKS):
            cs = slice(n * RNN_BLOCK, (n + 1) * RNN_BLOCK)
            pre = jnp.dot(ub[:, cs], wai_ref[n], preferred_element_type=F32)
            r = _sigmoid_tanh(pre[:, :RNN_BLOCK] + ba[:, cs])
            gi = _sigmoid_tanh(pre[:, RNN_BLOCK:] + bi[:, cs])
            log_a = -LRU_C * r * sp[:, cs]
            a = jnp.exp(log_a)
            mult = jnp.sqrt(1.0 - a * a)
            if first:
                mult = jnp.where((row == 0) & (step == 0), 1.0, mult)
            a_scr[n, pl.ds(bb, tb, stride=nb), :] = a
            b_scr[n, pl.ds(bb, tb, stride=nb), :] = mult * (gi * u[:, cs])
        return carry

    lax.fori_loop(0, nb, slab, 0)

    def tstep(t, h):
        base = pl.multiple_of(t * nb, nb)
        h = a_scr[:, pl.ds(base, nb), :] * h + b_scr[:, pl.ds(base, nb), :]
        hs_scr[:, pl.ds(base, nb), :] = h
        return h

    h_fin = lax.fori_loop(0, tb, tstep, h_scr[...], unroll=8)
    h_scr[...] = h_fin
    hl_ref[...] = h_fin

    def gate(bb, carry):
        hb = jnp.concatenate([hs_scr[n, pl.ds(bb, tb, stride=nb), :] for n in range(nct)], axis=1)
        o_ref[bb] = (hb * jax.nn.gelu(y_ref[bb].astype(F32))).astype(o_ref.dtype)
        return carry

    lax.fori_loop(0, nb, gate, 0)


def _rglru(proj3, ctx0, h0, cw, cb, wai, ba, bi, lam, *, first):
    nb, s, _ = proj3.shape
    tb = min(s, LRU_TIME_BLOCK)
    c = D_RNN
    nct = c // LANES
    kern = functools.partial(_rglru_kernel, nb=nb, tb=tb, first=first)
    full = lambda shape: pl.BlockSpec(shape, lambda t: (0,) * len(shape))
    return pl.pallas_call(
        kern,
        grid=(s // tb,),
        in_specs=[
            pl.BlockSpec((nb, tb, c), lambda t: (0, t, COL_XR)),
            pl.BlockSpec((nb, tb, c), lambda t: (0, t, COL_YR)),
            full((SUBLANES, c)), full((nct, nb, LANES)), full((CONV_WIDTH, c)), full((1, c)),
            full((N_RNN_BLOCKS, RNN_BLOCK, 2 * RNN_BLOCK)), full((1, c)), full((1, c)), full((1, c)),
        ],
        out_specs=[pl.BlockSpec((nb, tb, c), lambda t: (0, t, 0)), full((nct, nb, LANES))],
        out_shape=[jax.ShapeDtypeStruct((nb, s, c), BF16), jax.ShapeDtypeStruct((nct, nb, LANES), F32)],
        scratch_shapes=[
            pltpu.VMEM((nb, SUBLANES, c), F32),
            pltpu.VMEM((nct, nb * tb, LANES), F32),
            pltpu.VMEM((nct, nb * tb, LANES), F32),
            pltpu.VMEM((nct, nb * tb, LANES), F32),
            pltpu.VMEM((nct, nb, LANES), F32),
        ],
        compiler_params=_cparams(("arbitrary",)),
        name="rglru_first" if first else "rglru",
    )(proj3, proj3, ctx0, h0, cw, cb, wai, ba, bi, lam)


ATTN_Q_PER_STEP = 4


def _attn_schedule(nq):
    assert nq % ATTN_Q_PER_STEP == 0
    qids, slots, kblocks = [], [], []
    for j in range(nq // ATTN_Q_PER_STEP):
        ids = [2 * j, nq - 1 - 2 * j, 2 * j + 1, nq - 2 - 2 * j]
        items = [(sl, kb) for sl, q in enumerate(ids) for kb in range(q)] + list(enumerate(ids))
        qids += ids
        slots += [it[0] for it in items]
        kblocks += [it[1] for it in items]
    as_i32 = lambda v: jnp.asarray(v, jnp.int32)
    return as_i32(qids), as_i32(slots), as_i32(kblocks), 2 * (nq - 1)


def _attn_kernel(qid_ref, slot_ref, kb_ref, lq1_ref, lk1_ref, lq2_ref, lk2_ref, g_ref, q_ref, k_ref, v_ref,
                 km_ref, vm_ref, *rest, tq, cast_groups, n_full):
    n_in, n_out = sum(cast_groups), len(cast_groups)
    cast_in, o_ref, cast_out = rest[:n_in], rest[n_in], rest[n_in + 1:n_in + 1 + n_out]
    q2_scr, m_scr, acc_scr, s_scr = rest[n_in + 1 + n_out:]
    first = 0
    for dst, n_src in zip(cast_out, cast_groups):
        parts = [src[...].astype(BF16) for src in cast_in[first:first + n_src]]
        dst[...] = parts[0] if n_src == 1 else jnp.concatenate(parts, axis=-1)
        first += n_src

    step = pl.program_id(2)
    n_items = n_full + ATTN_Q_PER_STEP
    hq = tq // 2
    lane = lax.broadcasted_iota(jnp.int32, (1, V_DIM), 1)

    def qrows(sl):
        return pl.multiple_of(qid_ref[step * ATTN_Q_PER_STEP + sl] * tq, tq)

    for sl in range(ATTN_Q_PER_STEP):
        q = (q_ref[pl.ds(qrows(sl), tq), :].astype(F32) * (HEAD_DIM ** -0.5 * LOG2_E)).astype(BF16)
        zero = jnp.zeros_like(q)
        q_m1, q_m2 = jnp.where(lane < HEAD_DIM, q, zero), jnp.where(lane >= HEAD_DIM, q, zero)
        q2_scr[sl] = jnp.concatenate([q_m1[:hq], q_m2[:hq], q_m1[hq:], q_m2[hq:]], axis=0)

    def scores(sl, r0, kblk):
        return lax.dot_general(q2_scr[sl, r0:, :], kblk, (((1,), (1,)), ((), ())),
                               preferred_element_type=F32)

    def consume(sl, r0, s, vblk, mask, fresh):
        width = s.shape[1]
        vext = jnp.concatenate([vblk, jnp.ones_like(vblk)], axis=1)
        if mask is not None:
            s = jnp.where(mask, s, NEG_BIG)
        row_max = jnp.max(s, axis=1, keepdims=True)
        if fresh:
            m_next = jnp.broadcast_to(row_max, (s.shape[0], LANES))
        else:
            m_prev = m_scr[sl, r0:, :]
            m_next = jnp.maximum(m_prev, row_max)
        p = jnp.exp2(s - jnp.tile(m_next, (1, width // LANES)))
        pv = jnp.dot(p.astype(BF16), vext, preferred_element_type=F32)
        if fresh:
            acc_scr[sl, r0:, :] = pv
        else:
            corr = jnp.exp2(m_prev - m_next)
            acc_scr[sl, r0:, :] = jnp.tile(corr, (1, 2)) * acc_scr[sl, r0:, :] + pv
        m_scr[sl, r0:, :] = m_next

    mcol = lax.broadcasted_iota(jnp.int32, (2 * tq, LANES), 1)
    r_i = lax.broadcasted_iota(jnp.int32, (2 * tq, hq), 0)
    c_i = lax.broadcasted_iota(jnp.int32, (2 * tq, hq), 1)
    within = c_i <= (r_i & (hq - 1))
    first_half_mask = (r_i >= tq) | within
    second_half_mask = within[tq:, :]

    work = [(sl, 0, None, LANES, mcol < N_META) for sl in range(ATTN_Q_PER_STEP)]
    for t in range(n_items):
        sl, kb = slot_ref[step * n_items + t], kb_ref[step * n_items + t]
        if t < n_full:
            work.append((sl, 0, pl.ds(pl.multiple_of(kb * tq, tq), tq), tq, None))
        else:
            work.append((sl, 0, pl.ds(pl.multiple_of(kb * tq, hq), hq), hq, first_half_mask))
            work.append((sl, tq, pl.ds(pl.multiple_of(kb * tq + hq, hq), hq), hq, second_half_mask))

    def keys(rows):
        return km_ref[...] if rows is None else k_ref[rows, :]

    def values(rows):
        return vm_ref[...] if rows is None else v_ref[rows, :]

    def put_scores(t):
        sl, r0, rows, width, _ = work[t]
        s_scr[t % 2, r0:, :width] = scores(sl, r0, keys(rows))

    put_scores(0)
    for t, (sl, r0, rows, width, mask) in enumerate(work):
        if t + 1 < len(work):
            put_scores(t + 1)
        consume(sl, r0, s_scr[t % 2, r0:, :width], values(rows), mask, fresh=t < ATTN_Q_PER_STEP)

    lam = (jnp.exp(jnp.sum(lq1_ref[...] * lk1_ref[...], axis=1, keepdims=True))
           - jnp.exp(jnp.sum(lq2_ref[...] * lk2_ref[...], axis=1, keepdims=True)) + LAMBDA_INIT)
    for sl in range(ATTN_Q_PER_STEP):
        for half in range(2):
            a1 = acc_scr[sl, pl.ds((2 * half) * hq, hq), :]
            a2 = acc_scr[sl, pl.ds((2 * half + 1) * hq, hq), :]
            o = a1[:, :V_DIM] / a1[:, V_DIM:] - lam * (a2[:, :V_DIM] / a2[:, V_DIM:])
            o = o * lax.rsqrt(jnp.mean(o * o, axis=-1, keepdims=True) + LN_EPS) * g_ref[...]
            o_ref[pl.ds(qrows(sl) + half * hq, hq), :] = (o * (1.0 - LAMBDA_INIT)).astype(o_ref.dtype)


def _diff_attn(proj3, km, vm, lq1, lk1, lq2, lk2, g, to_bf16):
    nb, s, _ = proj3.shape
    tq = min(s // ATTN_Q_PER_STEP, ATTN_Q_BLOCK)
    nq = s // tq
    qids, slots, kblocks, n_full = _attn_schedule(nq)
    n_steps = nq // ATTN_Q_PER_STEP
    steps = nb * N_HEADS * n_steps
    hb = D_ATTN // V_DIM
    small = lambda shape: pl.BlockSpec(shape, lambda b, h, j, *_: (0,) * len(shape))
    seq = lambda col: pl.BlockSpec((None, s, V_DIM), lambda b, h, j, *_: (b, 0, col * hb + h))
    meta = pl.BlockSpec((LANES, V_DIM), lambda b, h, j, *_: (0, h))
    chunk = lambda w: w.reshape(steps, w.size // w.shape[-1] // steps, w.shape[-1])
    chunk_spec = lambda shape: pl.BlockSpec((None,) + shape[1:],
                                            lambda b, h, j, *_: ((b * N_HEADS + h) * n_steps + j, 0, 0))
    chunked = [chunk(w) for group in to_bf16 for w in group]
    joined = [chunk(group[0]).shape[:2] + (sum(w.shape[-1] for w in group),) for group in to_bf16]
    grid_spec = pltpu.PrefetchScalarGridSpec(
        num_scalar_prefetch=3,
        grid=(nb, N_HEADS, n_steps),
        in_specs=[small((1, HEAD_DIM)), small((1, HEAD_DIM)), small((1, HEAD_DIM)), small((1, HEAD_DIM)),
                  small((1, V_DIM)), seq(COL_Q), seq(COL_K), seq(COL_V), meta, meta]
                 + [chunk_spec(c.shape) for c in chunked],
        out_specs=[pl.BlockSpec((None, s, V_DIM), lambda b, h, j, *_: (b, 0, h))]
                  + [chunk_spec(shape) for shape in joined],
        scratch_shapes=[
            pltpu.VMEM((ATTN_Q_PER_STEP, 2 * tq, V_DIM), BF16),
            pltpu.VMEM((ATTN_Q_PER_STEP, 2 * tq, LANES), F32),
            pltpu.VMEM((ATTN_Q_PER_STEP, 2 * tq, 2 * V_DIM), F32),
            pltpu.VMEM((2, 2 * tq, tq), F32),
        ],
    )
    outs = pl.pallas_call(
        functools.partial(_attn_kernel, tq=tq, cast_groups=tuple(len(group) for group in to_bf16),
                          n_full=n_full),
        grid_spec=grid_spec,
        out_shape=[jax.ShapeDtypeStruct((nb, s, D_ATTN), BF16)]
                  + [jax.ShapeDtypeStruct(shape, BF16) for shape in joined],
        compiler_params=_cparams(("parallel", "parallel", "arbitrary")),
        name="diff_attn",
    )(qids, slots, kblocks, lq1, lk1, lq2, lk2, g, proj3, proj3, proj3, km, vm, *chunked)
    return outs[0], [o.reshape(group[0].shape[:-1] + (o.shape[-1],)) for o, group in zip(outs[1:], to_bf16)]


def _merge_kernel(x_ref, rec_ref, att_ref, gr_ref, ga_ref, bgr_ref, bga_ref, wr_ref, wa_ref, wo_ref,
                  eg_ref, eb_ref, g1_ref, b1_ref, h_ref, hp_ref):
    r = jnp.dot(rec_ref[...], wr_ref[...], preferred_element_type=F32)
    a = jnp.dot(att_ref[...], wa_ref[...], preferred_element_type=F32)
    g_rec = jax.nn.sigmoid(gr_ref[...].astype(F32) + bgr_ref[...])
    g_att = jax.nn.sigmoid(ga_ref[...].astype(F32) + bga_ref[...])
    mix = jnp.dot((g_rec * r + g_att * a).astype(BF16), wo_ref[...], preferred_element_type=F32)
    h0 = _layer_norm_rows(x_ref[...], eg_ref[...], eb_ref[...])
    h1 = _layer_norm_rows(ALPHA * h0 + mix, g1_ref[...], b1_ref[...])
    h_ref[...] = h1
    hp_ref[...] = _pack_bf16_pairs(h1)


def _pack_bf16_pairs(v):
    half = v.shape[1] // 2
    lo = lax.bitcast_convert_type(v[:, :half].astype(BF16).astype(F32), U32) >> 16
    hi = lax.bitcast_convert_type(v[:, half:].astype(BF16).astype(F32), U32) & jnp.uint32(0xFFFF0000)
    return lo | hi


def _unpack_bf16_pairs(p):
    lo = lax.bitcast_convert_type(p << 16, F32)
    hi = lax.bitcast_convert_type(p & jnp.uint32(0xFFFF0000), F32)
    return lo, hi


def _merge(x2d, rec2d, att2d, proj2d, bgr, bga, wr, wa, wo, eg, eb, g1, b1):
    n, d = x2d.shape
    tm = min(n, ROW_TILE)
    row = lambda col: pl.BlockSpec((tm, d), lambda i: (i, col))
    vec = pl.BlockSpec((1, d), lambda i: (0, 0))
    mat = pl.BlockSpec((d, d), lambda i: (0, 0))
    return pl.pallas_call(
        _merge_kernel,
        grid=(n // tm,),
        in_specs=[row(0), row(0), row(0), row(COL_GREC), row(COL_GATT), vec, vec, mat, mat, mat,
                  vec, vec, vec, vec],
        out_specs=[row(0), pl.BlockSpec((tm, d // 2), lambda i: (i, 0))],
        out_shape=[jax.ShapeDtypeStruct((n, d), F32), jax.ShapeDtypeStruct((n, d // 2), U32)],
        compiler_params=_cparams(("parallel",)),
        name="merge_ln1",
    )(x2d, rec2d, att2d, proj2d, proj2d, bgr, bga, wr, wa, wo, eg, eb, g1, b1)


def _split_bf16(v):
    hi = lax.bitcast_convert_type(lax.bitcast_convert_type(v, U32) & jnp.uint32(0xFFFF0000), F32)
    return hi.astype(BF16), (v - hi).astype(BF16)


def _router_kernel(h_ref, wt_hi_ref, wt_lo_ref, bias_ref, e_ref, w_ref, r_ref, c_ref, cnt_scr):
    tm = h_ref.shape[0]

    @pl.when(pl.program_id(0) == 0)
    def _():
        cnt_scr[...] = jnp.zeros(cnt_scr.shape, F32)

    h_hi, h_lo = _split_bf16(h_ref[...])
    nt = lambda a, b: lax.dot_general(a, b, (((1,), (1,)), ((), ())), preferred_element_type=F32)
    logits = nt(wt_hi_ref[...], h_hi) + nt(wt_hi_ref[...], h_lo) + nt(wt_lo_ref[...], h_hi)
    scores = jax.nn.sigmoid(logits)
    sel = scores + bias_ref[...]
    grp = sel.reshape(N_GROUPS, GROUP_SIZE, tm)
    gi = lax.broadcasted_iota(jnp.int32, grp.shape, 1)
    m1 = jnp.max(grp, axis=1, keepdims=True)
    first = jnp.min(jnp.where(grp == m1, gi, GROUP_SIZE), axis=1, keepdims=True)
    m2 = jnp.max(jnp.where(gi == first, -jnp.inf, grp), axis=1, keepdims=True)
    gscore = (m1 + m2).reshape(N_GROUPS, tm)

    gidx = lax.broadcasted_iota(jnp.int32, (N_GROUPS, tm), 0)
    keep = jnp.zeros((N_GROUPS, tm), jnp.bool_)
    cur = gscore
    for _ in range(TOPK_GROUPS):
        mx = jnp.max(cur, axis=0, keepdims=True)
        pick = gidx == jnp.min(jnp.where(cur == mx, gidx, N_GROUPS), axis=0, keepdims=True)
        keep = keep | pick
        cur = jnp.where(pick, -jnp.inf, cur)

    cur = jnp.where(keep[:, None, :], grp, -jnp.inf).reshape(N_EXPERTS, tm)
    eidx = lax.broadcasted_iota(jnp.int32, (N_EXPERTS, tm), 0)
    picked_e, picked_w, picks = [], [], []
    for _ in range(TOP_K):
        mx = jnp.max(cur, axis=0, keepdims=True)
        ei = jnp.min(jnp.where(cur == mx, eidx, N_EXPERTS), axis=0, keepdims=True)
        pick = eidx == ei
        picked_e.append(ei)
        picked_w.append(jnp.sum(jnp.where(pick, scores, 0.0), axis=0, keepdims=True))
        picks.append(pick)
        cur = jnp.where(pick, -jnp.inf, cur)
    w = jnp.concatenate(picked_w, axis=0)
    e_ref[...] = jnp.concatenate(picked_e, axis=0)
    w_ref[...] = w / jnp.sum(w, axis=0, keepdims=True) * ROUTED_SCALE

    member_f = functools.reduce(jnp.add, [jnp.where(p, 1.0, 0.0) for p in picks])
    t_row = lax.broadcasted_iota(jnp.int32, (tm, tm), 0)
    t_col = lax.broadcasted_iota(jnp.int32, (tm, tm), 1)
    earlier = jnp.where(t_row < t_col, 1.0, 0.0).astype(BF16)
    before = cnt_scr[...] + jnp.dot(member_f.astype(BF16), earlier, preferred_element_type=F32)
    ranks = [jnp.sum(jnp.where(p, before, 0.0), axis=0, keepdims=True) for p in picks]
    r_ref[...] = jnp.concatenate(ranks, axis=0).astype(jnp.int32)
    total = cnt_scr[...] + jnp.sum(member_f, axis=1, keepdims=True)
    cnt_scr[...] = total
    c_ref[...] = total.astype(jnp.int32)


def _router(h1, router_wt, bias_col):
    n, d = h1.shape
    tm = min(n, ROW_TILE)
    tok = pl.BlockSpec((TOP_K, tm), lambda i: (0, i))
    wt_hi, wt_lo = _split_bf16(router_wt)
    return pl.pallas_call(
        _router_kernel,
        grid=(n // tm,),
        in_specs=[
            pl.BlockSpec((tm, d), lambda i: (i, 0)),
            pl.BlockSpec((N_EXPERTS, d), lambda i: (0, 0)),
            pl.BlockSpec((N_EXPERTS, d), lambda i: (0, 0)),
            pl.BlockSpec((N_EXPERTS, 1), lambda i: (0, 0)),
        ],
        out_specs=[tok, tok, tok, pl.BlockSpec((N_EXPERTS, 1), lambda i: (0, 0))],
        out_shape=[jax.ShapeDtypeStruct((TOP_K, n), jnp.int32), jax.ShapeDtypeStruct((TOP_K, n), F32),
                   jax.ShapeDtypeStruct((TOP_K, n), jnp.int32),
                   jax.ShapeDtypeStruct((N_EXPERTS, 1), jnp.int32)],
        scratch_shapes=[pltpu.VMEM((N_EXPERTS, 1), F32)],
        compiler_params=_cparams(("arbitrary",)),
        name="router",
    )(h1, wt_hi, wt_lo, bias_col)


def _slots_kernel(e_ref, r_ref, ps_ref, o_ref):
    tm = e_ref.shape[1]
    eidx = lax.broadcasted_iota(jnp.int32, (N_EXPERTS, tm), 0)
    e = e_ref[...]
    ps = ps_ref[...]
    base = [jnp.sum(jnp.where(eidx == e[k:k + 1, :], ps, 0), axis=0, keepdims=True) for k in range(TOP_K)]
    o_ref[...] = jnp.concatenate(base, axis=0) + r_ref[...]


def _slots(eidx, rank, pstart_col):
    k, n = eidx.shape
    tm = min(n, SLOT_TILE)
    tok = pl.BlockSpec((k, tm), lambda i: (0, i))
    return pl.pallas_call(
        _slots_kernel,
        grid=(n // tm,),
        in_specs=[tok, tok, pl.BlockSpec((N_EXPERTS, 1), lambda i: (0, 0))],
        out_specs=tok,
        out_shape=jax.ShapeDtypeStruct((k, n), jnp.int32),
        compiler_params=_cparams(("parallel",)),
        name="slots",
    )(eidx, rank, pstart_col)


SC_CORES = 2
SC_SUBCORES = 16
SC_WINDOW = 64


def _sc_mesh():
    return plsc.VectorSubcoreMesh(core_axis_name="core", subcore_axis_name="subcore")


def _sc_windows(slot_flat):
    m = slot_flat.shape[0]
    n_win = m // SC_WINDOW // (SC_CORES * SC_SUBCORES)
    assert n_win * SC_WINDOW * SC_CORES * SC_SUBCORES == m and n_win % 2 == 0
    return n_win, slot_flat.reshape(m // SC_WINDOW, SC_WINDOW)


def _sc_scratch(n_win, d, dtype):
    return [pltpu.VMEM((n_win, SC_WINDOW), jnp.int32), pltpu.VMEM((2, SC_WINDOW, d), dtype),
            pltpu.SemaphoreType.DMA, pltpu.SemaphoreType.DMA]


def _sc_scatter_rows(rows, slots, n_slots):
    n, d = rows.shape
    top_k = slots.shape[0]
    n_win = n // SC_WINDOW // (SC_CORES * SC_SUBCORES)
    assert n_win * SC_WINDOW * SC_CORES * SC_SUBCORES == n and n_win % 2 == 0
    slot_win = slots.reshape(top_k, n // SC_WINDOW, SC_WINDOW)

    @pl.kernel(out_type=jax.ShapeDtypeStruct((n_slots, d), rows.dtype), mesh=_sc_mesh(),
               scratch_types=[pltpu.VMEM((top_k, n_win, SC_WINDOW), jnp.int32),
                              pltpu.VMEM((2, SC_WINDOW, d), rows.dtype),
                              pltpu.SemaphoreType.DMA, pltpu.SemaphoreType.DMA])
    def scatter(x_hbm, i_hbm, o_hbm, idx_v, rows_v, sem0, sem1):
        wid = lax.axis_index("subcore") * SC_CORES + lax.axis_index("core")
        first = wid * n_win
        for k in range(top_k):
            pltpu.sync_copy(i_hbm.at[k, pl.ds(first, n_win)], idx_v.at[k])
        sems = (sem0, sem1)

        @pl.loop(0, n_win // 2)
        def _(pair):
            copies = []
            for b in range(2):
                w = pair * 2 + b
                pltpu.sync_copy(x_hbm.at[pl.ds((first + w) * SC_WINDOW, SC_WINDOW)], rows_v.at[b])
                for k in range(top_k):
                    cp = pltpu.make_async_copy(rows_v.at[b], o_hbm.at[idx_v.at[k, w]], sems[b])
                    cp.start()
                    copies.append(cp)
            for cp in copies:
                cp.wait()

    return scatter(rows, slot_win)


def _sc_gather_rows(table, slot_flat):
    d = table.shape[1]
    m = slot_flat.shape[0]
    n_win, slot_win = _sc_windows(slot_flat)

    @pl.kernel(out_type=jax.ShapeDtypeStruct((m, d), table.dtype), mesh=_sc_mesh(),
               scratch_types=_sc_scratch(n_win, d, table.dtype))
    def gather(t_hbm, i_hbm, o_hbm, idx_v, rows_v, sem0, sem1):
        wid = lax.axis_index("subcore") * SC_CORES + lax.axis_index("core")
        first = wid * n_win
        pltpu.sync_copy(i_hbm.at[pl.ds(first, n_win)], idx_v)
        sems = (sem0, sem1)

        def fetch(c, b):
            return pltpu.make_async_copy(t_hbm.at[idx_v.at[c]], rows_v.at[b], sems[b])

        def flush(c, b):
            pltpu.sync_copy(rows_v.at[b], o_hbm.at[pl.ds((first + c) * SC_WINDOW, SC_WINDOW)])

        fetch(0, 0).start()

        @pl.loop(0, n_win // 2)
        def _(pair):
            c = pair * 2
            fetch(c + 1, 1).start()
            fetch(c, 0).wait()
            flush(c, 0)

            @pl.when(c + 2 < n_win)
            def _():
                fetch(c + 2, 0).start()

            fetch(c + 1, 1).wait()
            flush(c + 1, 1)

    return gather(table, slot_win)


EXPERT_BLOCK = 512
BLOCKS_PER_STEP = 2


def _experts_kernel(bset_ref, bvalid_ref, nused_ref, set0_ref, set1_ref, x_ref, wgu0_ref, wd0_ref,
                    wgu1_ref, wd1_ref, o_ref):
    for half in range(BLOCKS_PER_STEP):
        blk = pl.program_id(0) * BLOCKS_PER_STEP + half
        rows = pl.ds(half * EXPERT_BLOCK, EXPERT_BLOCK)

        @pl.when(blk < nused_ref[0])
        def _():
            in_set0 = bset_ref[blk] == 0
            wgu = jnp.where(in_set0, wgu0_ref[...], wgu1_ref[...])
            wd = jnp.where(in_set0, wd0_ref[...], wd1_ref[...])
            lo, hi = _unpack_bf16_pairs(x_ref[rows, :])
            x = jnp.concatenate([lo, hi], axis=1)
            rowid = lax.broadcasted_iota(jnp.int32, (EXPERT_BLOCK, 1), 0)
            x = jnp.where(rowid < bvalid_ref[blk], x, 0.0).astype(BF16)
            gu = jnp.dot(x, wgu, preferred_element_type=F32)
            hmid = (jax.nn.silu(gu[:, :D_EXPERT]) * gu[:, D_EXPERT:]).astype(BF16)
            o_ref[rows, :] = _pack_bf16_pairs(jnp.dot(hmid, wd, preferred_element_type=F32))


def _experts(block_set, block_valid, n_used, step_set0, step_set1, xs, wgu, wd):
    n_slots, dh = xs.shape
    d = 2 * dh
    step_rows = BLOCKS_PER_STEP * EXPERT_BLOCK
    n_steps = n_slots // step_rows
    last_step = lambda nu: (nu[0] - 1) // BLOCKS_PER_STEP
    rows = lambda i, bs, bv, nu, s0, s1: (jnp.minimum(i, last_step(nu)), 0)
    set0 = lambda i, bs, bv, nu, s0, s1: (s0[jnp.minimum(i, last_step(nu))], 0, 0)
    set1 = lambda i, bs, bv, nu, s0, s1: (s1[jnp.minimum(i, last_step(nu))], 0, 0)
    grid_spec = pltpu.PrefetchScalarGridSpec(
        num_scalar_prefetch=5,
        grid=(n_steps,),
        in_specs=[pl.BlockSpec((step_rows, dh), rows),
                  pl.BlockSpec((None, d, 2 * D_EXPERT), set0), pl.BlockSpec((None, D_EXPERT, d), set0),
                  pl.BlockSpec((None, d, 2 * D_EXPERT), set1), pl.BlockSpec((None, D_EXPERT, d), set1)],
        out_specs=pl.BlockSpec((step_rows, dh), rows),
    )
    return pl.pallas_call(
        _experts_kernel,
        grid_spec=grid_spec,
        out_shape=jax.ShapeDtypeStruct((n_slots, dh), U32),
        compiler_params=_cparams(("arbitrary",)),
        name="experts",
    )(block_set, block_valid, n_used, step_set0, step_set1, xs, wgu, wd, wgu, wd)


def _final_kernel(h_ref, y_ref, w_ref, wgu_ref, wd_ref, g_ref, b_ref, o_ref):
    h1 = h_ref[...]
    gu = jnp.dot(h1.astype(BF16), wgu_ref[...], preferred_element_type=F32)
    hmid = (jax.nn.silu(gu[:, :D_SHARED]) * gu[:, D_SHARED:]).astype(BF16)
    shared = jnp.dot(hmid, wd_ref[...], preferred_element_type=F32)
    w = w_ref[...]
    r_lo = r_hi = None
    for k in range(TOP_K):
        lo, hi = _unpack_bf16_pairs(y_ref[k])
        wk = w[:, k:k + 1]
        r_lo = wk * lo if r_lo is None else r_lo + wk * lo
        r_hi = wk * hi if r_hi is None else r_hi + wk * hi
    ffn = jnp.concatenate([r_lo, r_hi], axis=1) + shared
    o_ref[...] = _layer_norm_rows(ALPHA * h1 + ffn, g_ref[...], b_ref[...])


def _final(h1, ytok, wtok, wgu, wd, g, b):
    n, d = h1.shape
    tm = min(n, ROW_TILE)
    row = pl.BlockSpec((tm, d), lambda i: (i, 0))
    vec = pl.BlockSpec((1, d), lambda i: (0, 0))
    return pl.pallas_call(
        _final_kernel,
        grid=(n // tm,),
        in_specs=[row, pl.BlockSpec((TOP_K, tm, d // 2), lambda i: (0, i, 0)),
                  pl.BlockSpec((tm, TOP_K), lambda i: (i, 0)),
                  pl.BlockSpec((d, 2 * D_SHARED), lambda i: (0, 0)),
                  pl.BlockSpec((D_SHARED, d), lambda i: (0, 0)), vec, vec],
        out_specs=row,
        out_shape=jax.ShapeDtypeStruct((n, d), F32),
        compiler_params=_cparams(("parallel",)),
        name="shared_ln2",
    )(h1, ytok, wtok, wgu, wd, g, b)


def _segments(counts, n_assign):
    counts = counts.reshape(N_EXPERTS)
    padded = (counts + EXPERT_BLOCK - 1) // EXPERT_BLOCK * EXPERT_BLOCK
    seg_end = jnp.cumsum(padded)
    seg_start = seg_end - padded
    n_blocks = n_assign // EXPERT_BLOCK + N_EXPERTS
    block_start = jnp.arange(n_blocks, dtype=jnp.int32) * EXPERT_BLOCK
    block_exp = jnp.minimum(jnp.sum(seg_end[None, :] <= block_start[:, None], axis=1), N_EXPERTS - 1)
    mine = block_exp[:, None] == jnp.arange(N_EXPERTS)[None, :]
    rows_left = jnp.sum(jnp.where(mine, (counts + seg_start)[None, :], 0), axis=1) - block_start
    block_valid = jnp.clip(rows_left, 0, EXPERT_BLOCK)
    n_used = (seg_end[-1] // EXPERT_BLOCK).reshape(1)
    in_use = block_start < seg_end[-1]
    new_expert = in_use & (block_exp != jnp.concatenate([jnp.full((1,), -1, block_exp.dtype), block_exp[:-1]]))
    block_set = (jnp.cumsum(new_expert) - 1) % 2
    latest = lambda s: jnp.maximum(lax.cummax(jnp.where(in_use & (block_set == s), block_exp, -1)), 0)
    step_last = slice(BLOCKS_PER_STEP - 1, None, BLOCKS_PER_STEP)
    i32 = lambda v: v.astype(jnp.int32)
    return (i32(seg_start), i32(block_set), i32(block_valid), i32(n_used), i32(latest(0)[step_last]),
            i32(latest(1)[step_last]), n_blocks * EXPERT_BLOCK)


def kernel(x, meta_tokens, emb_ln_g, emb_ln_b, w_in, b_gate, conv_w, conv_b, lru_w_a, lru_b_a, lru_w_i, lru_b_i, lru_lambda, lam_q1, lam_k1, lam_q2, lam_k2, subln_g, w_rec_proj, w_attn_proj, w_o, ln1_g, ln1_b, router_w, router_bias, w_e_gate, w_e_up, w_e_down, w_s_gate, w_s_up, w_s_down, ln2_g, ln2_b):
    nb, s, d = x.shape
    n = nb * s
    row = lambda v: v.reshape(1, -1).astype(F32)
    l = 0

    x2d = x.reshape(n, d)
    w_in_b = w_in[l].astype(BF16)
    eg, eb = row(emb_ln_g), row(emb_ln_b)
    proj = _ln_inproj(x2d, eg, eb, w_in_b)
    projm = _ln_inproj(meta_tokens.astype(F32), eg, eb, w_in_b)
    proj3 = proj.reshape(nb, s, IN_COLS)

    wai = jnp.concatenate([lru_w_a[l], lru_w_i[l]], axis=-1).astype(BF16)
    lru_args = (conv_w[l].astype(F32), row(conv_b[l]), wai, row(lru_b_a[l]), row(lru_b_i[l]),
                row(lru_lambda[l]))
    zeros8 = jnp.zeros((SUBLANES, D_RNN), F32)
    nct = D_RNN // LANES
    _, h_meta = _rglru(projm.reshape(1, N_META, IN_COLS), zeros8, jnp.zeros((nct, 1, LANES), F32),
                       *lru_args, first=True)
    ctx0 = projm[N_META - SUBLANES:, :D_RNN].astype(F32)
    rec, _ = _rglru(proj3, ctx0, jnp.broadcast_to(h_meta, (nct, nb, LANES)), *lru_args, first=False)

    pad = ((0, LANES - N_META), (0, 0))
    km = jnp.pad(projm[:, COL_K * D_ATTN:(COL_K + 1) * D_ATTN], pad)
    vm = jnp.pad(projm[:, COL_V * D_ATTN:(COL_V + 1) * D_ATTN], pad)
    att, (wgu_b, wd_b) = _diff_attn(
        proj3, km, vm, row(lam_q1[l]), row(lam_k1[l]), row(lam_q2[l]), row(lam_k2[l]), row(subln_g[l]),
        to_bf16=((w_e_gate[l], w_e_up[l]), (w_e_down[l],)))

    bg = b_gate[l].astype(F32)
    h1, h1p = _merge(x2d, rec.reshape(n, D_RNN), att.reshape(n, D_ATTN), proj,
                     bg[:D_MODEL].reshape(1, -1), bg[D_MODEL:].reshape(1, -1),
                     w_rec_proj[l].astype(BF16), w_attn_proj[l].astype(BF16), w_o[l].astype(BF16),
                     eg, eb, row(ln1_g[l]), row(ln1_b[l]))

    eidx, wts, rank, counts = _router(h1, router_w[l].T.astype(F32),
                                      router_bias[l].reshape(-1, 1).astype(F32))
    seg_start, block_set, block_valid, n_used, step_set0, step_set1, n_slots = _segments(counts, TOP_K * n)
    slots = _slots(eidx, rank, seg_start.reshape(-1, 1))

    xs = _sc_scatter_rows(h1p, slots, n_slots)
    ys = _experts(block_set, block_valid, n_used, step_set0, step_set1, xs, wgu_b, wd_b)
    ytok = _sc_gather_rows(ys, slots.reshape(TOP_K * n)).reshape(TOP_K, n, d // 2)

    wgu = jnp.concatenate([w_s_gate[l], w_s_up[l]], axis=-1).astype(BF16)
    out = _final(h1, ytok, wts.T, wgu, w_s_down[l].astype(BF16), row(ln2_g[l]), row(ln2_b[l]))
    return out.reshape(nb, s, d)
```
